```python
import math
import numpy as np
import jax
import jax.numpy as jnp
from jax import lax

D_MODEL = 1024
BATCH = 2
SEQ = 8192
DEPTH = 2
DEC_BATCH = 32
DEC_SEQ = 1
PAST_LEN = 16384
PAGE_SIZE = 128

EPS = 1e-6
ROPE_THETA = 500000.0
ROPE_FRACTION = 4
NH_A = 4
DK_A = 128
DV_A = 128
CHUNK_A = 128
NH_B = 4
KVH_B = 2
HD_B = 128
MOBA_BLOCK = 256
MOBA_TOPK = 3
MOBA_QB = 64
NH_C = 8
HD_C = 64
DI_C = NH_C * HD_C
NG_C = 2
DS_C = 128
CONV_C = 4
CONV_DIM_C = DI_C + 2 * NG_C * DS_C
CHUNK_C = 128
SWA_GROUPS = ((128, 1), (512, 4), (2048, 16))
NG_D = 3
HPG_D = 4
HD_D = 64
NH_D = NG_D * HPG_D
SWA_QB = 128
D_FF = 2816
FFN_CONV = 3
L0_SIZES = (NH_A * DK_A, NH_A * DK_A, NH_A * DV_A, NH_A * DV_A, NH_A, NH_A, NH_B * HD_B, KVH_B * HD_B, KVH_B * HD_B)
L0_OUT = NH_A * DV_A + NH_B * HD_B
L1_SIZES = (DI_C, CONV_DIM_C, NH_C, NH_D * HD_D, NH_D * HD_D, NH_D * HD_D)
L1_OUT = DI_C + HPG_D * HD_D

kernel_name = 'hybrid_mlstm_moba_ssd_dilated_step'


def rmsnorm(x, w):
    xf = x.astype(jnp.float32)
    y = xf * lax.rsqrt(jnp.mean(xf * xf, axis=-1, keepdims=True) + EPS)
    return (y * w.astype(jnp.float32)).astype(x.dtype)


def group_rmsnorm(y, w, groups):
    shp = y.shape
    yg = y.astype(jnp.float32).reshape(shp[:-1] + (groups, shp[-1] // groups))
    yg = yg * lax.rsqrt(jnp.mean(yg * yg, axis=-1, keepdims=True) + EPS)
    return yg.reshape(shp) * w.astype(jnp.float32)


def split_cols(p, sizes):
    return jnp.split(p, np.cumsum(sizes)[:-1].tolist(), axis=-1)


def rope_partial(x, pos):
    rd = x.shape[-1] // ROPE_FRACTION
    half = rd // 2
    inv = ROPE_THETA ** (-jnp.arange(half, dtype=jnp.float32) / half)
    ang = pos.astype(jnp.float32)[:, None] * inv[None, :]
    cos = jnp.cos(ang)[None, :, None, :]
    sin = jnp.sin(ang)[None, :, None, :]
    x1 = x[..., :half].astype(jnp.float32)
    x2 = x[..., half:rd].astype(jnp.float32)
    return jnp.concatenate([(x1 * cos - x2 * sin).astype(x.dtype), (x2 * cos + x1 * sin).astype(x.dtype), x[..., rd:]], axis=-1)


def causal_dwconv(u, buf, w, b):
    k = w.shape[0]
    ext = jnp.concatenate([buf.astype(u.dtype), u], axis=1)
    y = lax.conv_general_dilated(ext, w.astype(u.dtype)[:, None, :], window_strides=(1,), padding='VALID',
                                 dimension_numbers=('NWC', 'WIO', 'NWC'), feature_group_count=u.shape[-1])
    return y + b.astype(u.dtype), ext[:, ext.shape[1] - (k - 1):]


def to_chunks(a, size):
    b, s = a.shape[:2]
    a = a.reshape((b, s // size, size) + a.shape[2:])
    return a.transpose((1, 0, 3, 2) + tuple(range(4, a.ndim)))


def from_chunks(a):
    a = a.transpose((1, 0, 3, 2) + tuple(range(4, a.ndim)))
    return a.reshape((a.shape[0], a.shape[1] * a.shape[2]) + a.shape[3:])


def q_blocks(a, qb):
    b, q = a.shape[:2]
    return jnp.moveaxis(a.reshape((b, q // qb, qb) + a.shape[2:]), 1, 0)


def q_unblocks(a):
    a = jnp.moveaxis(a, 0, 1)
    return a.reshape((a.shape[0], a.shape[1] * a.shape[2]) + a.shape[3:])


def mlstm_chunked(q, k, v, ig, fg, c0, n0, m0):
    f32 = jnp.float32
    s, dk = q.shape[1], q.shape[-1]
    size = CHUNK_A if s % CHUNK_A == 0 else s
    causal = jnp.tril(jnp.ones((size, size), dtype=bool))
    xs = (to_chunks(q.astype(f32), size), to_chunks(k.astype(f32) * dk ** -0.5, size), to_chunks(v.astype(f32), size),
          to_chunks(ig.astype(f32), size), to_chunks(jax.nn.log_sigmoid(fg.astype(f32)), size))

    def step(carry, blk):
        c, n, m = carry
        qc, kc, vc, ic, lf = blk
        fcum = jnp.cumsum(lf, axis=-1)
        dmat = jnp.where(causal, fcum[..., :, None] - fcum[..., None, :] + ic[..., None, :], -jnp.inf)
        inter = fcum + m[..., None]
        mt = jnp.maximum(inter, jnp.max(dmat, axis=-1))
        wmat = jnp.einsum('bhtd,bhsd->bhts', qc, kc) * jnp.exp(dmat - mt[..., None])
        a_inter = jnp.exp(inter - mt)
        num = jnp.einsum('bhts,bhsv->bhtv', wmat, vc) + a_inter[..., None] * jnp.einsum('bhtd,bhdv->bhtv', qc, c)
        den = jnp.sum(wmat, axis=-1) + a_inter * jnp.einsum('bhtd,bhd->bht', qc, n)
        h = num / jnp.maximum(jnp.abs(den), jnp.exp(-mt))[..., None]
        f_end = fcum[..., -1]
        w_end = f_end[..., None] - fcum + ic
        m_new = jnp.maximum(f_end + m, jnp.max(w_end, axis=-1))
        w_end = jnp.exp(w_end - m_new[..., None])
        decay = jnp.exp(f_end + m - m_new)
        c_new = decay[..., None, None] * c + jnp.einsum('bhs,bhsd,bhsv->bhdv', w_end, kc, vc)
        n_new = decay[..., None] * n + jnp.einsum('bhs,bhsd->bhd', w_end, kc)
        return (c_new, n_new, m_new), h

    (c1, n1, m1), h = lax.scan(step, (c0.astype(f32), n0.astype(f32), m0.astype(f32)), xs)
    return from_chunks(h), c1, n1, m1


def ssd_chunked(x, dt, a, bm, cm, h0):
    f32 = jnp.float32
    s, nh = x.shape[1], x.shape[2]
    size = CHUNK_C if s % CHUNK_C == 0 else s
    head_group = jnp.arange(nh) // (nh // bm.shape[2])
    causal = jnp.tril(jnp.ones((size, size), dtype=bool))
    xs = (to_chunks(x.astype(f32), size), to_chunks(dt * a, size), to_chunks(dt, size),
          to_chunks(bm.astype(f32)[:, :, head_group], size), to_chunks(cm.astype(f32)[:, :, head_group], size))

    def step(h, blk):
        xc, ac, dc, bc, cc = blk
        acum = jnp.cumsum(ac, axis=-1)
        decay = jnp.exp(jnp.where(causal, acum[..., :, None] - acum[..., None, :], -jnp.inf))
        mmat = jnp.einsum('bhtn,bhsn->bhts', cc, bc) * decay * dc[..., None, :]
        y = jnp.einsum('bhts,bhsp->bhtp', mmat, xc) + jnp.exp(acum)[..., None] * jnp.einsum('bhtn,bhpn->bhtp', cc, h)
        w_end = jnp.exp(acum[..., -1:] - acum) * dc
        h_new = jnp.exp(acum[..., -1])[..., None, None] * h + jnp.einsum('bhs,bhsp,bhsn->bhpn', w_end, xc, bc)
        return h_new, y

    h1, y = lax.scan(step, h0.astype(f32), xs)
    return from_chunks(y), h1


def moba_attend(q, k, v, q_pos):
    f32 = jnp.float32
    b, nq_tot = q.shape[:2]
    lk = k.shape[1]
    grp = NH_B // KVH_B
    nb = -(-lk // MOBA_BLOCK)
    n_full = lk // MOBA_BLOCK
    topk = min(MOBA_TOPK, n_full)
    pad = ((0, 0), (0, nb * MOBA_BLOCK - lk), (0, 0), (0, 0))
    kb = jnp.pad(k, pad).reshape(b, nb, MOBA_BLOCK, KVH_B, HD_B).transpose(0, 3, 1, 2, 4)
    vb = jnp.pad(v, pad).reshape(b, nb, MOBA_BLOCK, KVH_B, HD_B).transpose(0, 3, 1, 2, 4)
    kmean = jnp.mean(kb[:, :, :n_full].astype(f32), axis=3)
    bi = jnp.arange(b)[:, None, None, None]
    hi = (jnp.arange(NH_B) // grp)[None, None, :, None]
    offs = jnp.arange(MOBA_BLOCK, dtype=jnp.int32)
    qb = MOBA_QB if nq_tot % MOBA_QB == 0 else nq_tot
    scale = HD_B ** -0.5

    def one_block(args):
        qc, pc = args
        qf = qc.astype(f32)
        nqb = qc.shape[1]
        own = pc // MOBA_BLOCK
        own_sel = jnp.broadcast_to(own[None, :, None, None], (b, nqb, NH_B, 1))
        if topk > 0:
            bscore = jnp.einsum('bqjgd,bjnd->bqjgn', qf.reshape(b, nqb, KVH_B, grp, HD_B), kmean).reshape(b, nqb, NH_B, n_full)
            past = jnp.arange(n_full)[None, :] < own[:, None]
            bscore = jnp.where(past[None, :, None, :], bscore, -jnp.inf)
            sel = lax.top_k(bscore, topk)[1].astype(jnp.int32)
            blocks = jnp.concatenate([sel, own_sel], axis=-1)
            slot_ok = jnp.concatenate([jnp.arange(topk)[None, :] < own[:, None], jnp.ones((nqb, 1), bool)], axis=-1)
        else:
            blocks = own_sel
            slot_ok = jnp.ones((nqb, 1), bool)
        kg = kb[bi, hi, blocks]
        vg = vb[bi, hi, blocks]
        kpos = blocks[..., None] * MOBA_BLOCK + offs
        ok = (kpos <= pc[None, :, None, None, None]) & slot_ok[None, :, None, :, None]
        s = jnp.einsum('bqhd,bqhtkd->bqhtk', qf, kg.astype(f32)) * scale
        s = jnp.where(ok, s, -jnp.inf)
        p = jax.nn.softmax(s.reshape(s.shape[:3] + (-1,)), axis=-1).reshape(s.shape)
        return jnp.einsum('bqhtk,bqhtkd->bqhd', p, vg.astype(f32))

    out = lax.map(one_block, (q_blocks(q, qb), q_pos.reshape(-1, qb)))
    return q_unblocks(out)


def dilated_attend(q, k, v, q_pos, k_start, window, dil):
    f32 = jnp.float32
    nq_tot, hd = q.shape[1], q.shape[-1]
    dist = jnp.arange(window // dil + 1, dtype=jnp.int32) * dil
    qb = SWA_QB if nq_tot % SWA_QB == 0 else nq_tot
    scale = hd ** -0.5

    def one_block(args):
        qc, pc = args
        idx = pc[:, None] - dist[None, :] - k_start
        ok = idx >= 0
        idx = jnp.maximum(idx, 0)
        kg = k[:, idx].astype(f32)
        vg = v[:, idx].astype(f32)
        s = jnp.einsum('bqhd,bqnhd->bqhn', qc.astype(f32), kg) * scale
        s = jnp.where(ok[None, :, None, :], s, -jnp.inf)
        lse = jax.nn.logsumexp(s, axis=-1)
        o = jnp.einsum('bqhn,bqnhd->bqhd', jnp.exp(s - lse[..., None]), vg)
        return o, lse

    o, lse = lax.map(one_block, (q_blocks(q, qb), q_pos.reshape(-1, qb)))
    return q_unblocks(o), q_unblocks(lse)


def mixer_even(h, pos, state, w_in, b_gates, norm_mlstm, w_out):
    f32 = jnp.float32
    b, s, _ = h.shape
    qa, ka, va, oa, ia, fa, qb, kb, vb = split_cols(h @ w_in, L0_SIZES)
    if state is None:
        c0 = jnp.zeros((b, NH_A, DK_A, DV_A), f32)
        n0 = jnp.zeros((b, NH_A, DK_A), f32)
        m0 = jnp.zeros((b, NH_A), f32)
        k_past = None
        v_past = None
    else:
        c0, n0, m0, k_past, v_past = state
    ig = ia.astype(f32) + b_gates[:NH_A].astype(f32)
    fg = fa.astype(f32) + b_gates[NH_A:].astype(f32)
    ha, c1, n1, m1 = mlstm_chunked(qa.reshape(b, s, NH_A, DK_A), ka.reshape(b, s, NH_A, DK_A),
                                   va.reshape(b, s, NH_A, DV_A), ig, fg, c0, n0, m0)
    ha = jax.nn.sigmoid(oa.reshape(b, s, NH_A, DV_A).astype(f32)) * ha
    ha = group_rmsnorm(ha.reshape(b, s, NH_A * DV_A), norm_mlstm, NH_A)
    qr = rope_partial(qb.reshape(b, s, NH_B, HD_B), pos)
    kr = rope_partial(kb.reshape(b, s, KVH_B, HD_B), pos)
    vr = vb.reshape(b, s, KVH_B, HD_B)
    if k_past is None:
        k_all, v_all = kr, vr
    else:
        k_all = jnp.concatenate([k_past.astype(kr.dtype), kr], axis=1)
        v_all = jnp.concatenate([v_past.astype(vr.dtype), vr], axis=1)
    hb = moba_attend(qr, k_all, v_all, pos).reshape(b, s, NH_B * HD_B)
    y = jnp.concatenate([ha, hb], axis=-1).astype(h.dtype) @ w_out
    dt = h.dtype
    return y, (c1.astype(dt), n1.astype(dt), m1.astype(dt), kr, vr)


def mixer_odd(h, pos, state, w_in, conv_w, conv_b, dt_bias, a_log, d_skip, norm_ssd, w_out):
    f32 = jnp.float32
    b, s, _ = h.shape
    z, xbc, dtr, qd, kd, vd = split_cols(h @ w_in, L1_SIZES)
    if state is None:
        h0 = jnp.zeros((b, NH_C, HD_C, DS_C), f32)
        conv0 = jnp.zeros((b, CONV_C - 1, CONV_DIM_C), h.dtype)
        bufs = (None,) * NG_D
    else:
        h0, conv0, bufs = state
    xbc, conv1 = causal_dwconv(xbc, conv0, conv_w, conv_b)
    xbc = jax.nn.silu(xbc.astype(f32))
    xs, bm, cm = split_cols(xbc, (DI_C, NG_C * DS_C, NG_C * DS_C))
    xs = xs.reshape(b, s, NH_C, HD_C)
    dt = jax.nn.softplus(dtr.astype(f32) + dt_bias.astype(f32))
    a = -jnp.exp(a_log.astype(f32))
    ys, h1 = ssd_chunked(xs, dt, a, bm.reshape(b, s, NG_C, DS_C), cm.reshape(b, s, NG_C, DS_C), h0)
    ys = ys + d_skip.astype(f32)[:, None] * xs
    yc = group_rmsnorm(ys.reshape(b, s, DI_C) * jax.nn.silu(z.astype(f32)), norm_ssd, NG_C)
    qr = rope_partial(qd.reshape(b, s, NH_D, HD_D), pos)
    kr = rope_partial(kd.reshape(b, s, NH_D, HD_D), pos)
    vr = vd.reshape(b, s, NH_D, HD_D)
    outs, lses, rows = [], [], []
    for g, (win, dil) in enumerate(SWA_GROUPS):
        hs = slice(g * HPG_D, (g + 1) * HPG_D)
        kg, vg = kr[:, :, hs], vr[:, :, hs]
        kv_new = jnp.stack([kg, vg], axis=2)
        if bufs[g] is None:
            k_seq, v_seq, k_start = kg, vg, 0
            rows.append(kv_new[:, s - min(win, s):])
        else:
            buf = bufs[g].astype(kg.dtype)
            k_seq = jnp.concatenate([buf[:, :, 0], kg], axis=1)
            v_seq = jnp.concatenate([buf[:, :, 1], vg], axis=1)
            k_start = pos[0] - buf.shape[1]
            rows.append(kv_new)
        o, lse = dilated_attend(qr[:, :, hs], k_seq, v_seq, pos, k_start, win, dil)
        outs.append(o)
        lses.append(lse)
    wts = jax.nn.softmax(jnp.stack(lses), axis=0)
    od = jnp.sum(wts[..., None] * jnp.stack(outs), axis=0).reshape(b, s, HPG_D * HD_D)
    y = jnp.concatenate([yc, od], axis=-1).astype(h.dtype) @ w_out
    return y, (h1.astype(h.dtype), conv1, rows[0], rows[1], rows[2])


def conv_ffn(h, buf, w_up, conv_w, conv_b, w_down):
    u, new_buf = causal_dwconv(h @ w_up, buf, conv_w, conv_b)
    gate, val = jnp.split(u, 2, axis=-1)
    return (jax.nn.silu(gate) * val) @ w_down, new_buf


def run_trunk(x, pos0, even_state, odd_state, ffn_state, even_w, odd_w, ffn_w, norm_mix, norm_ffn, norm_final):
    b, s, _ = x.shape
    pos = pos0 + jnp.arange(s, dtype=jnp.int32)
    ffn_up, ffn_conv_w, ffn_conv_b, ffn_down = ffn_w
    mixer_states = []
    ffn_bufs = []
    for layer in range(DEPTH):
        hn = rmsnorm(x, norm_mix[layer])
        if layer % 2 == 0:
            mix, st = mixer_even(hn, pos, even_state, *even_w)
        else:
            mix, st = mixer_odd(hn, pos, odd_state, *odd_w)
        mixer_states.append(st)
        x = x + mix
        buf = jnp.zeros((b, FFN_CONV - 1, 2 * D_FF), x.dtype) if ffn_state is None else ffn_state[layer]
        f, fbuf = conv_ffn(rmsnorm(x, norm_ffn[layer]), buf, ffn_up[layer], ffn_conv_w[layer], ffn_conv_b[layer], ffn_down[layer])
        x = x + f
        ffn_bufs.append(fbuf)
    return rmsnorm(x, norm_final), mixer_states[0], mixer_states[1], jnp.stack(ffn_bufs)


def setup_inputs(seed: int = 0) -> dict:
    key = jax.random.key(seed)
    keys = iter(jax.random.split(key, 48))
    f32 = jnp.float32
    n_pages = PAST_LEN // PAGE_SIZE
    n_used = DEC_BATCH * n_pages
    n_pool = n_used + max(1, n_used // 4)

    def normal(shape, scale):
        return scale * jax.random.normal(next(keys), shape, f32)

    def gain(shape):
        return 1.0 + 0.01 * jax.random.normal(next(keys), shape, f32)

    def uniform(shape, lo, hi):
        return jax.random.uniform(next(keys), shape, f32, lo, hi)

    x_prompt = normal((BATCH, SEQ, D_MODEL), 1.0)
    x_sample = normal((DEC_BATCH, DEC_SEQ, D_MODEL), 1.0)
    state_l0_mlstm_c = normal((DEC_BATCH, NH_A, DK_A, DV_A), 0.5)
    state_l0_mlstm_n = normal((DEC_BATCH, NH_A, DK_A), 0.5)
    state_l0_mlstm_m = normal((DEC_BATCH, NH_A), 1.0)
    cache_l0_moba_k = normal((n_pool, PAGE_SIZE, KVH_B, HD_B), 1.0)
    cache_l0_moba_v = normal((n_pool, PAGE_SIZE, KVH_B, HD_B), 1.0)
    state_l1_ssd_h = normal((DEC_BATCH, NH_C, HD_C, DS_C), 0.5)
    state_l1_ssd_conv = normal((DEC_BATCH, CONV_C - 1, CONV_DIM_C), 1.0)
    cache_l1_swa_kv0 = normal((DEC_BATCH, min(SWA_GROUPS[0][0], PAST_LEN), 2, HPG_D, HD_D), 1.0)
    cache_l1_swa_kv1 = normal((DEC_BATCH, min(SWA_GROUPS[1][0], PAST_LEN), 2, HPG_D, HD_D), 1.0)
    cache_l1_swa_kv2 = normal((DEC_BATCH, min(SWA_GROUPS[2][0], PAST_LEN), 2, HPG_D, HD_D), 1.0)
    state_ffn_conv = normal((DEPTH, DEC_BATCH, FFN_CONV - 1, 2 * D_FF), 1.0)
    page_table = jax.random.permutation(next(keys), n_pool)[:n_used].reshape(DEC_BATCH, n_pages).astype(jnp.int32)
    dt0 = jnp.exp(uniform((NH_C,), math.log(1e-3), math.log(1e-1)))
    return {
        'x_prompt': x_prompt,
        'x_sample': x_sample,
        'state_l0_mlstm_c': state_l0_mlstm_c,
        'state_l0_mlstm_n': state_l0_mlstm_n,
        'state_l0_mlstm_m': state_l0_mlstm_m,
        'cache_l0_moba_k': cache_l0_moba_k,
        'cache_l0_moba_v': cache_l0_moba_v,
        'state_l1_ssd_h': state_l1_ssd_h,
        'state_l1_ssd_conv': state_l1_ssd_conv,
        'cache_l1_swa_kv0': cache_l1_swa_kv0,
        'cache_l1_swa_kv1': cache_l1_swa_kv1,
        'cache_l1_swa_kv2': cache_l1_swa_kv2,
        'state_ffn_conv': state_ffn_conv,
        'page_table': page_table,
        'norm_mix': gain((DEPTH, D_MODEL)),
        'norm_ffn': gain((DEPTH, D_MODEL)),
        'norm_final': gain((D_MODEL,)),
        'w_in_l0': normal((D_MODEL, sum(L0_SIZES)), D_MODEL ** -0.5),
        'b_gates_l0': jnp.concatenate([normal((NH_A,), 0.1), uniform((NH_A,), 3.0, 6.0)]),
        'norm_mlstm_l0': gain((NH_A * DV_A,)),
        'w_out_l0': normal((L0_OUT, D_MODEL), L0_OUT ** -0.5),
        'w_in_l1': normal((D_MODEL, sum(L1_SIZES)), D_MODEL ** -0.5),
        'conv_w_l1': normal((CONV_C, CONV_DIM_C), CONV_C ** -0.5),
        'conv_b_l1': normal((CONV_DIM_C,), 0.01),
        'dt_bias_l1': dt0 + jnp.log(-jnp.expm1(-dt0)),
        'a_log_l1': jnp.log(uniform((NH_C,), 1.0, 16.0)),
        'd_skip_l1': 1.0 + 0.1 * jax.random.normal(next(keys), (NH_C,), f32),
        'norm_ssd_l1': gain((DI_C,)),
        'w_out_l1': normal((L1_OUT, D_MODEL), L1_OUT ** -0.5),
        'ffn_up': normal((DEPTH, D_MODEL, 2 * D_FF), D_MODEL ** -0.5),
        'ffn_conv_w': normal((DEPTH, FFN_CONV, 2 * D_FF), FFN_CONV ** -0.5),
        'ffn_conv_b': normal((DEPTH, 2 * D_FF), 0.01),
        'ffn_down': normal((DEPTH, D_FF, D_MODEL), D_FF ** -0.5),
    }


def reference(x_prompt, x_sample, state_l0_mlstm_c, state_l0_mlstm_n, state_l0_mlstm_m, cache_l0_moba_k,
              cache_l0_moba_v, state_l1_ssd_h, state_l1_ssd_conv, cache_l1_swa_kv0, cache_l1_swa_kv1,
              cache_l1_swa_kv2, state_ffn_conv, page_table, norm_mix, norm_ffn, norm_final, w_in_l0, b_gates_l0,
              norm_mlstm_l0, w_out_l0, w_in_l1, conv_w_l1, conv_b_l1, dt_bias_l1, a_log_l1, d_skip_l1, norm_ssd_l1,
              w_out_l1, ffn_up, ffn_conv_w, ffn_conv_b, ffn_down):
    n_seq, n_pages = page_table.shape
    past_len = n_pages * PAGE_SIZE
    k_past = cache_l0_moba_k[page_table].reshape(n_seq, past_len, KVH_B, HD_B)
    v_past = cache_l0_moba_v[page_table].reshape(n_seq, past_len, KVH_B, HD_B)
    even_w = (w_in_l0, b_gates_l0, norm_mlstm_l0, w_out_l0)
    odd_w = (w_in_l1, conv_w_l1, conv_b_l1, dt_bias_l1, a_log_l1, d_skip_l1, norm_ssd_l1, w_out_l1)
    ffn_w = (ffn_up, ffn_conv_w, ffn_conv_b, ffn_down)
    y_prompt, ev_p, od_p, ffn_p = run_trunk(x_prompt, 0, None, None, None, even_w, odd_w, ffn_w,
                                            norm_mix, norm_ffn, norm_final)
    even_state = (state_l0_mlstm_c, state_l0_mlstm_n, state_l0_mlstm_m, k_past, v_past)
    odd_state = (state_l1_ssd_h, state_l1_ssd_conv, (cache_l1_swa_kv0, cache_l1_swa_kv1, cache_l1_swa_kv2))
    y_sample, ev_s, od_s, ffn_s = run_trunk(x_sample, past_len, even_state, odd_state, state_ffn_conv, even_w, odd_w,
                                            ffn_w, norm_mix, norm_ffn, norm_final)
    c_p, n_p, m_p, k_p, v_p = ev_p
    c_s, n_s, m_s, k_s, v_s = ev_s
    h_p, conv_p, sw0_p, sw1_p, sw2_p = od_p
    h_s, conv_s, sw0_s, sw1_s, sw2_s = od_s
    return (y_prompt, y_sample, c_p, c_s, n_p, n_s, m_p, m_s, k_p, k_s, v_p, v_s, h_p, h_s, conv_p, conv_s,
            sw0_p, sw0_s, sw1_p, sw1_s, sw2_p, sw2_s, ffn_p, ffn_s)
```

```python
import functools
import math

import jax
import jax.numpy as jnp
from jax import lax
from jax.experimental import pallas as pl
from jax.experimental.pallas import tpu as pltpu

F32 = jnp.float32
BF16 = jnp.bfloat16
HIGHEST = lax.Precision.HIGHEST

EPS = 1e-6
ROPE_THETA = 500000.0
ROPE_FRACTION = 4
LANES = 128
SUBLANES = 8
VMEM_LIMIT = 56 * 1024 * 1024

NH_A, DK_A, DV_A, CHUNK_A = 4, 128, 128, 128
NH_B, KVH_B, HD_B, MOBA_BLOCK, MOBA_TOPK = 4, 2, 128, 256, 3
NH_C, HD_C, NG_C, DS_C, CONV_C, CHUNK_C = 8, 64, 2, 128, 4, 128
DI_C = NH_C * HD_C
CONV_DIM_C = DI_C + 2 * NG_C * DS_C
SWA_GROUPS = ((128, 1), (512, 4), (2048, 16))
HPG_D, HD_D = 4, 64
SWA_W = HPG_D * HD_D
SWA_KEYS = 128
D_FF, FFN_CONV = 2816, 3
FFN_CW = 256

NN = (((1,), (0,)), ((), ()))
NT = (((1,), (1,)), ((), ()))
TN = (((0,), (0,)), ((), ()))


def _dot(a, b, dims=NN, precision=None):
    return lax.dot_general(a, b, dims, precision=precision, preferred_element_type=F32)


def _bdot(a, b, dims=NN):
    return _dot(a.astype(BF16), b.astype(BF16), dims)


def _params(sem, vmem=VMEM_LIMIT):
    return pltpu.CompilerParams(dimension_semantics=sem, vmem_limit_bytes=vmem)


def _rms(x, w):
    return x * lax.rsqrt(jnp.mean(x * x, axis=-1, keepdims=True) + EPS) * w


def _sigmoid(x):
    return 1.0 / (1.0 + jnp.exp(-x))


def _silu(x):
    return x * _sigmoid(x)


def _log_sigmoid(x):
    return jnp.minimum(x, 0.0) - jnp.log1p(jnp.exp(-jnp.abs(x)))


def _softplus(x):
    return jnp.maximum(x, 0.0) + jnp.log1p(jnp.exp(-jnp.abs(x)))


def _rope_lanes(y, cos, sa, sb, half):
    parts = []
    for g in range(y.shape[1] // LANES):
        yg = y[:, g * LANES:(g + 1) * LANES]
        parts.append(yg * cos + pltpu.roll(yg, LANES - half, axis=1) * sa + pltpu.roll(yg, half, axis=1) * sb)
    return parts


def _inproj_kernel(rope_half, x_ref, nw_ref, cos_ref, sa_ref, sb_ref, *refs):
    n = len(rope_half)
    w_refs, ws_ref, wst_ref = refs[:n], refs[n], refs[n + 1]
    out_refs, sc_ref, sr_ref = refs[n + 2:2 * n + 2], refs[2 * n + 2], refs[2 * n + 3]
    xn = _rms(x_ref[...], nw_ref[...])
    xb = xn.astype(BF16)
    for w_ref, o_ref, half in zip(w_refs, out_refs, rope_half):
        y = _dot(xb, w_ref[...])
        if half:
            parts = _rope_lanes(y, cos_ref[...], sa_ref[...], sb_ref[...], half)
            for g, p in enumerate(parts):
                o_ref[:, g * LANES:(g + 1) * LANES] = p
        else:
            o_ref[...] = y
    sc_ref[...] = _dot(xn, ws_ref[...], NN, HIGHEST)
    sr_ref[...] = _dot(wst_ref[...], xn, NT, HIGHEST)


def _inproj(x, nw, tabs, weights, rope_half, w_small, tm, name):
    m, d = x.shape
    cos, sa, sb = tabs
    trows = cos.shape[0]
    tb = trows // tm if trows >= tm else 1
    ws = jnp.zeros((d, LANES), F32).at[:, :w_small.shape[1]].set(w_small)
    wst = jnp.transpose(w_small)
    row = lambda i: (i, 0)
    const = lambda i: (0, 0)
    tab_spec = pl.BlockSpec((tm, LANES), lambda i: (i % tb, 0))
    in_specs = [pl.BlockSpec((tm, d), row), pl.BlockSpec((1, d), const), tab_spec, tab_spec, tab_spec]
    in_specs += [pl.BlockSpec(w.shape, const) for w in weights]
    in_specs += [pl.BlockSpec(ws.shape, const), pl.BlockSpec(wst.shape, const)]
    out_shape = [jax.ShapeDtypeStruct((m, w.shape[1]), F32) for w in weights]
    out_shape += [jax.ShapeDtypeStruct((m, LANES), F32), jax.ShapeDtypeStruct((SUBLANES, m), F32)]
    out_specs = [pl.BlockSpec((tm, w.shape[1]), row) for w in weights]
    out_specs += [pl.BlockSpec((tm, LANES), row), pl.BlockSpec((SUBLANES, tm), lambda i: (0, i))]
    return pl.pallas_call(
        functools.partial(_inproj_kernel, tuple(rope_half)),
        grid=(m // tm,), in_specs=in_specs, out_specs=out_specs, out_shape=out_shape,
        compiler_params=_params(("arbitrary",)), name=name,
    )(x, nw.reshape(1, d), cos, sa, sb, *weights, ws, wst)


def _rope_tables(pos, head_dim):
    rd = head_dim // ROPE_FRACTION
    half = rd // 2
    inv = ROPE_THETA ** (-jnp.arange(half, dtype=F32) / half)
    ang = pos.astype(F32)[:, None] * inv[None, :]
    cos, sin = jnp.cos(ang), jnp.sin(ang)
    ones = jnp.ones((pos.shape[0], head_dim - rd), F32)
    zeros = jnp.zeros((pos.shape[0], head_dim - rd), F32)
    zh = jnp.zeros_like(sin)
    reps = LANES // head_dim
    cos_t = jnp.tile(jnp.concatenate([cos, cos, ones], axis=1), (1, reps))
    sa_t = jnp.tile(jnp.concatenate([-sin, zh, zeros], axis=1), (1, reps))
    sb_t = jnp.tile(jnp.concatenate([zh, sin, zeros], axis=1), (1, reps))
    return (cos_t, sa_t, sb_t), half


def _mlstm_kernel(q_ref, k_ref, v_ref, o_ref, gc_ref, gr_ref, bc_ref, br_ref, nw_ref,
                  h_ref, cout_ref, nout_ref, mout_ref, c_scr, n_scr, m_scr):
    ci = pl.program_id(1)
    ln = CHUNK_A

    @pl.when(ci == 0)
    def _():
        c_scr[...] = jnp.zeros_like(c_scr)
        n_scr[...] = jnp.zeros_like(n_scr)
        m_scr[...] = jnp.zeros_like(m_scr)

    row = lax.broadcasted_iota(jnp.int32, (ln, ln), 0)
    col = lax.broadcasted_iota(jnp.int32, (ln, ln), 1)
    tril = row >= col
    gc = gc_ref[...] + bc_ref[...]
    gr = gr_ref[...] + br_ref[...]
    fcum_c = _dot(tril.astype(F32), _log_sigmoid(gc), NN, HIGHEST)
    fcum_r = _dot(_log_sigmoid(gr), (row <= col).astype(F32), NN, HIGHEST)
    for h in range(NH_A):
        sl = slice(h * DK_A, (h + 1) * DK_A)
        ic_r, ic_c = gr[h:h + 1, :], gc[:, h:h + 1]
        fc_c, fc_r = fcum_c[:, NH_A + h:NH_A + h + 1], fcum_r[NH_A + h:NH_A + h + 1, :]
        m = m_scr[h:h + 1, 0:1]
        c = c_scr[h]
        nrow = n_scr[h:h + 1, :]
        qh = q_ref[:, sl]
        kh = k_ref[:, sl] * (DK_A ** -0.5)
        vh = v_ref[:, sl]
        dmat = jnp.where(tril, fc_c - fc_r + ic_r, -jnp.inf)
        inter = fc_c + m
        mt = jnp.maximum(inter, jnp.max(dmat, axis=1, keepdims=True))
        wmat = _bdot(qh, kh, NT) * jnp.exp(dmat - mt)
        a_inter = jnp.exp(inter - mt)
        num = _bdot(wmat, vh) + a_inter * _bdot(qh, c)
        den = jnp.sum(wmat, axis=1, keepdims=True) + a_inter * jnp.sum(qh * nrow, axis=1, keepdims=True)
        hh = num / jnp.maximum(jnp.abs(den), jnp.exp(-mt))
        f_end = fc_c[ln - 1:ln, :]
        m_new = jnp.maximum(f_end + m, jnp.max(f_end - fc_r + ic_r, axis=1, keepdims=True))
        kw = kh * jnp.exp(f_end - fc_c + ic_c - m_new)
        decay = jnp.exp(f_end + m - m_new)
        c_scr[h] = decay * c + _bdot(kw, vh, TN)
        n_scr[h:h + 1, :] = decay * nrow + jnp.sum(kw, axis=0, keepdims=True)
        m_scr[h:h + 1, :] = jnp.broadcast_to(m_new, (1, LANES))
        ha = _sigmoid(o_ref[:, sl]) * hh
        h_ref[:, sl] = _rms(ha, nw_ref[:, sl])

    @pl.when(ci == pl.num_programs(1) - 1)
    def _():
        cout_ref[0] = c_scr[...]
        nout_ref[0] = n_scr[0:NH_A, :]
        mout_ref[0] = m_scr[0:NH_A, :]


def _mlstm_prompt(a_main, g_col, g_row, b_gates, norm_w, batch, seq):
    m = a_main.shape[0]
    ln = CHUNK_A
    nc = seq // ln
    w = NH_A * DK_A
    blk = lambda j: pl.BlockSpec((ln, w), lambda b, c, j=j: (b * nc + c, j))
    bc = jnp.zeros((1, LANES), F32).at[0, :2 * NH_A].set(b_gates)
    br = b_gates.reshape(2 * NH_A, 1)
    const = lambda b, c: (0, 0)
    return pl.pallas_call(
        _mlstm_kernel, grid=(batch, nc),
        in_specs=[blk(0), blk(1), blk(2), blk(3),
                  pl.BlockSpec((ln, LANES), lambda b, c: (b * nc + c, 0)),
                  pl.BlockSpec((SUBLANES, ln), lambda b, c: (0, b * nc + c)),
                  pl.BlockSpec((1, LANES), const), pl.BlockSpec((SUBLANES, 1), const),
                  pl.BlockSpec((1, w), const)],
        out_specs=[pl.BlockSpec((ln, w), lambda b, c: (b * nc + c, 0)),
                   pl.BlockSpec((1, NH_A, DK_A, DV_A), lambda b, c: (b, 0, 0, 0)),
                   pl.BlockSpec((1, NH_A, DK_A), lambda b, c: (b, 0, 0)),
                   pl.BlockSpec((1, NH_A, LANES), lambda b, c: (b, 0, 0))],
        out_shape=[jax.ShapeDtypeStruct((m, w), F32),
                   jax.ShapeDtypeStruct((batch, NH_A, DK_A, DV_A), F32),
                   jax.ShapeDtypeStruct((batch, NH_A, DK_A), F32),
                   jax.ShapeDtypeStruct((batch, NH_A, LANES), F32)],
        scratch_shapes=[pltpu.VMEM((NH_A, DK_A, DV_A), F32), pltpu.VMEM((SUBLANES, LANES), F32),
                        pltpu.VMEM((SUBLANES, LANES), F32)],
        compiler_params=_params(("arbitrary", "arbitrary")), name="mlstm_prompt",
    )(a_main, a_main, a_main, a_main, g_col, g_row, bc, br, norm_w.reshape(1, w))


def _mlstm_step_kernel(a_ref, qc_ref, kc_ref, g_ref, b_ref, m_ref, c_ref, n_ref, nw_ref,
                       h_ref, cout_ref, nout_ref, mout_ref):
    g = g_ref[0] + b_ref[...]
    lf_all = _log_sigmoid(g)
    w = NH_A * DK_A
    scale = DK_A ** -0.5
    for h in range(NH_A):
        sl = slice(h * DK_A, (h + 1) * DK_A)
        q_row = a_ref[0, :, sl]
        k_row = a_ref[0, :, w + h * DK_A:w + (h + 1) * DK_A] * scale
        v_row = a_ref[0, :, 2 * w + h * DV_A:2 * w + (h + 1) * DV_A]
        o_row = a_ref[0, :, 3 * w + h * DV_A:3 * w + (h + 1) * DV_A]
        q_col = qc_ref[0, sl, :]
        k_col = kc_ref[0, sl, :] * scale
        c = c_ref[0, h]
        n_row = n_ref[0, h:h + 1, :]
        m = m_ref[0, :, h:h + 1]
        ic = g[:, h:h + 1]
        lf = lf_all[:, NH_A + h:NH_A + h + 1]
        inter = lf + m
        mt = jnp.maximum(inter, ic)
        wm = jnp.sum(q_row * k_row, axis=1, keepdims=True) * jnp.exp(ic - mt)
        a_inter = jnp.exp(inter - mt)
        num = wm * v_row + a_inter * jnp.sum(q_col * c, axis=0, keepdims=True)
        den = wm + a_inter * jnp.sum(q_row * n_row, axis=1, keepdims=True)
        hh = num / jnp.maximum(jnp.abs(den), jnp.exp(-mt))
        m_new = jnp.maximum(inter, ic)
        wgt = jnp.exp(ic - m_new)
        decay = jnp.exp(inter - m_new)
        cout_ref[0, h] = decay * c + (wgt * k_col) * v_row
        nout_ref[0, h:h + 1, :] = decay * n_row + wgt * k_row
        mout_ref[0, h:h + 1, :] = jnp.broadcast_to(m_new, (1, LANES))
        ha = _sigmoid(o_row) * hh
        h_ref[0, :, sl] = _rms(ha, nw_ref[:, sl])


def _mlstm_step(a_main, g_col, b_gates, norm_w, c0, n0, m0):
    ns = a_main.shape[0]
    w = NH_A * DK_A
    a3 = a_main.reshape(ns, 1, 4 * w)
    q_col = a_main[:, :w].reshape(ns, w, 1)
    k_col = a_main[:, w:2 * w].reshape(ns, w, 1)
    bc = jnp.zeros((1, LANES), F32).at[0, :2 * NH_A].set(b_gates)
    s3 = lambda s: (s, 0, 0)
    s4 = lambda s: (s, 0, 0, 0)
    const = lambda s: (0, 0)
    return pl.pallas_call(
        _mlstm_step_kernel, grid=(ns,),
        in_specs=[pl.BlockSpec((1, 1, 4 * w), s3), pl.BlockSpec((1, w, 1), s3), pl.BlockSpec((1, w, 1), s3),
                  pl.BlockSpec((1, 1, LANES), s3), pl.BlockSpec((1, LANES), const),
                  pl.BlockSpec((1, 1, NH_A), s3), pl.BlockSpec((1, NH_A, DK_A, DV_A), s4),
                  pl.BlockSpec((1, NH_A, DK_A), s3), pl.BlockSpec((1, w), const)],
        out_specs=[pl.BlockSpec((1, 1, w), s3), pl.BlockSpec((1, NH_A, DK_A, DV_A), s4),
                   pl.BlockSpec((1, NH_A, DK_A), s3), pl.BlockSpec((1, NH_A, LANES), s3)],
        out_shape=[jax.ShapeDtypeStruct((ns, 1, w), F32), jax.ShapeDtypeStruct((ns, NH_A, DK_A, DV_A), F32),
                   jax.ShapeDtypeStruct((ns, NH_A, DK_A), F32), jax.ShapeDtypeStruct((ns, NH_A, LANES), F32)],
        compiler_params=_params(("arbitrary",)), name="mlstm_step",
    )(a3, q_col, k_col, g_col.reshape(ns, 1, LANES), bc, m0.reshape(ns, 1, NH_A), c0, n0, norm_w.reshape(1, w))


def _top_blocks(bs, topk):
    lane = lax.broadcasted_iota(jnp.int32, bs.shape, 1).astype(F32)
    sel = jnp.zeros(bs.shape, F32)
    for _ in range(topk):
        mx = jnp.max(bs, axis=1, keepdims=True)
        idx = jnp.min(jnp.where(bs == mx, lane, float(LANES)), axis=1, keepdims=True)
        pick = (lane == idx) & (mx > -jnp.inf)
        sel = jnp.where(pick, 1.0, sel)
        bs = jnp.where(pick, -jnp.inf, bs)
    return sel


def _moba_kernel(n_full, q_ref, k_ref, v_ref, o_ref, kmean_scr, m_scr, l_scr, acc_scr):
    qi = pl.program_id(2)
    blk = MOBA_BLOCK
    grp = NH_B // KVH_B

    @pl.when(qi == 0)
    def _():
        kmean_scr[...] = jnp.zeros_like(kmean_scr)
        for n in range(n_full):
            kmean_scr[n:n + 1, :] = jnp.mean(k_ref[n * blk:(n + 1) * blk, :], axis=0, keepdims=True)

    q = jnp.concatenate([q_ref[:, g * HD_B:(g + 1) * HD_B] for g in range(grp)], axis=0)
    rows = grp * blk
    bs = _dot(q, kmean_scr[...], NT, HIGHEST)
    lane = lax.broadcasted_iota(jnp.int32, bs.shape, 1)
    bs = jnp.where(lane < qi, bs, -jnp.inf)
    sel = _top_blocks(bs, min(MOBA_TOPK, n_full))
    qs = (q * (HD_B ** -0.5)).astype(BF16)

    start = pl.multiple_of(qi * blk, blk)
    s = _dot(qs, k_ref[pl.ds(start, blk), :].astype(BF16), NT)
    r_in = lax.broadcasted_iota(jnp.int32, (rows, blk), 0) % blk
    c_in = lax.broadcasted_iota(jnp.int32, (rows, blk), 1)
    s = jnp.where(c_in <= r_in, s, -jnp.inf)
    m0 = jnp.max(s, axis=1, keepdims=True)
    p = jnp.exp(s - m0)
    m_scr[...] = m0
    l_scr[...] = jnp.sum(p, axis=1, keepdims=True)
    acc_scr[...] = _dot(p.astype(BF16), v_ref[pl.ds(start, blk), :].astype(BF16))

    def body(n, carry):
        st = pl.multiple_of(n * blk, blk)
        picked = jnp.max(jnp.where(lane == n, sel, 0.0), axis=1, keepdims=True) > 0.0
        sn = _dot(qs, k_ref[pl.ds(st, blk), :].astype(BF16), NT)
        sn = jnp.where(picked, sn, -jnp.inf)
        m_old = m_scr[...]
        m_new = jnp.maximum(m_old, jnp.max(sn, axis=1, keepdims=True))
        alpha = jnp.exp(m_old - m_new)
        pn = jnp.exp(sn - m_new)
        m_scr[...] = m_new
        l_scr[...] = alpha * l_scr[...] + jnp.sum(pn, axis=1, keepdims=True)
        acc_scr[...] = alpha * acc_scr[...] + _dot(pn.astype(BF16), v_ref[pl.ds(st, blk), :].astype(BF16))
        return carry

    lax.fori_loop(0, qi, body, 0)
    out = acc_scr[...] / l_scr[...]
    for g in range(grp):
        o_ref[:, g * HD_B:(g + 1) * HD_B] = out[g * blk:(g + 1) * blk, :]


def _moba_prompt(qb, kb, vb, batch, seq):
    m = qb.shape[0]
    blk = MOBA_BLOCK
    nq = seq // blk
    grp = NH_B // KVH_B
    return pl.pallas_call(
        functools.partial(_moba_kernel, seq // blk), grid=(batch, KVH_B, nq),
        in_specs=[pl.BlockSpec((blk, grp * HD_B), lambda b, j, i: (b * nq + i, j)),
                  pl.BlockSpec((seq, HD_B), lambda b, j, i: (b, j)),
                  pl.BlockSpec((seq, HD_B), lambda b, j, i: (b, j))],
        out_specs=pl.BlockSpec((blk, grp * HD_B), lambda b, j, i: (b * nq + i, j)),
        out_shape=jax.ShapeDtypeStruct((m, NH_B * HD_B), F32),
        scratch_shapes=[pltpu.VMEM((LANES, HD_B), F32), pltpu.VMEM((grp * blk, 1), F32),
                        pltpu.VMEM((grp * blk, 1), F32), pltpu.VMEM((grp * blk, HD_B), F32)],
        compiler_params=_params(("arbitrary", "arbitrary", "arbitrary")), name="moba_prompt",
    )(qb, kb, vb)


MOBA_PAGES_PER_STEP = 16


def _moba_select_kernel(n_steps, pt_ref, q_ref, *refs):
    pages, (kmean_ref, sel_ref) = refs[:MOBA_PAGES_PER_STEP], refs[MOBA_PAGES_PER_STEP:]
    j = pl.program_id(1)
    sums = [jnp.sum(p[0], axis=0, keepdims=True) for p in pages]
    per_blk = [(sums[2 * i] + sums[2 * i + 1]) * (1.0 / MOBA_BLOCK) for i in range(MOBA_PAGES_PER_STEP // 2)]
    kmean_ref[0, pl.ds(pl.multiple_of(j * SUBLANES, SUBLANES), SUBLANES), :] = jnp.concatenate(per_blk, axis=0)

    @pl.when(j == n_steps - 1)
    def _():
        grp = NH_B // KVH_B
        nb = kmean_ref.shape[1]
        rowi = lax.broadcasted_iota(jnp.int32, (nb, 1), 0).astype(F32)
        orow = lax.broadcasted_iota(jnp.int32, (SUBLANES, LANES), 0)
        olane = lax.broadcasted_iota(jnp.int32, (SUBLANES, LANES), 1)
        out = jnp.zeros((SUBLANES, LANES), F32)
        for h in range(NH_B):
            kv = h // grp
            km = kmean_ref[0, :, kv * HD_B:(kv + 1) * HD_B]
            qh = q_ref[0, :, h * HD_B:(h + 1) * HD_B]
            bs = jnp.sum(km * qh, axis=1, keepdims=True)
            for r in range(MOBA_TOPK):
                mx = jnp.max(bs, axis=0, keepdims=True)
                idx = jnp.min(jnp.where(bs == mx, rowi, float(nb)), axis=0, keepdims=True)
                out = jnp.where((orow == h) & (olane == r), idx, out)
                bs = jnp.where(rowi == idx, -jnp.inf, bs)
        sel_ref[0] = out.astype(jnp.int32)


def _moba_select(qb, k_cache, page_table):
    ns, n_pages = page_table.shape
    page = k_cache.shape[1]
    assert 2 * page == MOBA_BLOCK and n_pages % MOBA_PAGES_PER_STEP == 0
    n_steps = n_pages // MOBA_PAGES_PER_STEP
    kw = KVH_B * HD_B
    kc = k_cache.reshape(k_cache.shape[0], page, kw)
    page_spec = lambda i: pl.BlockSpec(
        (1, page, kw), lambda s, j, pt, i=i: (pt[s * n_pages + j * MOBA_PAGES_PER_STEP + i], 0, 0))
    nb = n_pages * page // MOBA_BLOCK
    grid_spec = pltpu.PrefetchScalarGridSpec(
        num_scalar_prefetch=1, grid=(ns, n_steps),
        in_specs=[pl.BlockSpec((1, 1, NH_B * HD_B), lambda s, j, pt: (s, 0, 0))]
        + [page_spec(i) for i in range(MOBA_PAGES_PER_STEP)],
        out_specs=[pl.BlockSpec((1, nb, kw), lambda s, j, pt: (s, 0, 0)),
                   pl.BlockSpec((1, SUBLANES, LANES), lambda s, j, pt: (s, 0, 0))])
    return pl.pallas_call(
        functools.partial(_moba_select_kernel, n_steps), grid_spec=grid_spec,
        out_shape=[jax.ShapeDtypeStruct((ns, nb, kw), F32), jax.ShapeDtypeStruct((ns, SUBLANES, LANES), jnp.int32)],
        compiler_params=_params(("arbitrary", "arbitrary")), name="moba_select",
    )(page_table.reshape(-1), qb.reshape(ns, 1, NH_B * HD_B), *([kc] * MOBA_PAGES_PER_STEP))


def _moba_step_kernel(sel_ref, pt_ref, q_ref, kn_ref, vn_ref, ka_ref, kb_ref, va_ref, vb_ref, o_ref,
                      m_scr, l_scr, acc_scr):
    r = pl.program_id(2)
    scale = HD_B ** -0.5
    q = q_ref[0, 0]

    @pl.when(r == 0)
    def _():
        m_scr[...] = jnp.sum(q * kn_ref[0, 0], axis=1, keepdims=True) * scale
        l_scr[...] = jnp.ones_like(l_scr)
        acc_scr[...] = vn_ref[0, 0]

    kk = jnp.concatenate([ka_ref[0], kb_ref[0]], axis=0)
    vv = jnp.concatenate([va_ref[0], vb_ref[0]], axis=0)
    s = jnp.sum(kk * q, axis=1, keepdims=True) * scale
    m_old = m_scr[...]
    m_new = jnp.maximum(m_old, jnp.max(s, axis=0, keepdims=True))
    alpha = jnp.exp(m_old - m_new)
    p = jnp.exp(s - m_new)
    m_scr[...] = m_new
    l_scr[...] = alpha * l_scr[...] + jnp.sum(p, axis=0, keepdims=True)
    acc_scr[...] = alpha * acc_scr[...] + jnp.sum(p * vv, axis=0, keepdims=True)

    @pl.when(r == pl.num_programs(2) - 1)
    def _():
        o_ref[0, 0] = acc_scr[...] / l_scr[...]


def _moba_step(qb, k_new, v_new, k_cache, v_cache, page_table, sel):
    ns, n_pages = page_table.shape
    page = k_cache.shape[1]
    kw = KVH_B * HD_B
    grp = NH_B // KVH_B
    kc = k_cache.reshape(k_cache.shape[0], page, kw)
    vc = v_cache.reshape(v_cache.shape[0], page, kw)

    def page_spec(half):
        def imap(s, h, r, sel_r, pt_r):
            blk = sel_r[(s * NH_B + h) * MOBA_TOPK + r]
            return (pt_r[s * n_pages + 2 * blk + half], 0, h // grp)
        return pl.BlockSpec((1, page, HD_B), imap)

    qspec = pl.BlockSpec((1, 1, 1, HD_B), lambda s, h, r, a, b: (s, h, 0, 0))
    nspec = pl.BlockSpec((1, 1, 1, HD_B), lambda s, h, r, a, b: (s, h // grp, 0, 0))
    grid_spec = pltpu.PrefetchScalarGridSpec(
        num_scalar_prefetch=2, grid=(ns, NH_B, MOBA_TOPK),
        in_specs=[qspec, nspec, nspec, page_spec(0), page_spec(1), page_spec(0), page_spec(1)],
        out_specs=qspec,
        scratch_shapes=[pltpu.VMEM((1, 1), F32), pltpu.VMEM((1, 1), F32), pltpu.VMEM((1, HD_B), F32)])
    out = pl.pallas_call(
        _moba_step_kernel, grid_spec=grid_spec,
        out_shape=jax.ShapeDtypeStruct((ns, NH_B, 1, HD_B), F32),
        compiler_params=_params(("arbitrary", "arbitrary", "arbitrary")), name="moba_step",
    )(sel.reshape(-1), page_table.reshape(-1), qb.reshape(ns, NH_B, 1, HD_B),
      k_new.reshape(ns, KVH_B, 1, HD_B), v_new.reshape(ns, KVH_B, 1, HD_B), kc, kc, vc, vc)
    return out.reshape(ns, NH_B * HD_B)


def _outproj_kernel(a_ref, b_ref, wa_ref, wb_ref, x_ref, o_ref):
    o_ref[...] = x_ref[...] + _bdot(a_ref[...], wa_ref[...]) + _bdot(b_ref[...], wb_ref[...])


def _outproj(a, b, wa, wb, x, tm, name):
    m, d = x.shape
    row = lambda i: (i, 0)
    const = lambda i: (0, 0)
    return pl.pallas_call(
        _outproj_kernel, grid=(m // tm,),
        in_specs=[pl.BlockSpec((tm, a.shape[1]), row), pl.BlockSpec((tm, b.shape[1]), row),
                  pl.BlockSpec(wa.shape, const), pl.BlockSpec(wb.shape, const), pl.BlockSpec((tm, d), row)],
        out_specs=pl.BlockSpec((tm, d), row), out_shape=jax.ShapeDtypeStruct((m, d), F32),
        compiler_params=_params(("arbitrary",)), name=name,
    )(a, b, wa, wb, x)


def _ffn_kernel(final, x_ref, nw_ref, wup_ref, cw_ref, cb_ref, wdn_ref, nf_ref, o_ref, cs_ref,
                carry_scr, extg_scr, extv_scr):
    tm = x_ref.shape[0]

    @pl.when(pl.program_id(1) == 0)
    def _():
        carry_scr[...] = jnp.zeros_like(carry_scr)

    x = x_ref[...]
    xb = _rms(x, nw_ref[...]).astype(BF16)
    acc = x
    pad = SUBLANES

    def conv(ext_scr, col0):
        cols = slice(col0, col0 + FFN_CW)
        u = _dot(xb, wup_ref[:, cols])
        ext_scr[0:pad, :] = carry_scr[:, cols]
        ext_scr[pad:pad + tm, :] = u
        carry_scr[:, cols] = ext_scr[tm:tm + pad, :]
        y = cb_ref[:, cols] + cw_ref[FFN_CONV - 1:FFN_CONV, cols] * u
        for j in range(FFN_CONV - 1):
            off = pad - (FFN_CONV - 1) + j
            y = y + cw_ref[j:j + 1, cols] * ext_scr[off:off + tm, :]
        return y

    for c in range(D_FF // FFN_CW):
        gate = conv(extg_scr, c * FFN_CW)
        val = conv(extv_scr, D_FF + c * FFN_CW)
        act = (_silu(gate) * val).astype(BF16)
        acc = acc + _dot(act, wdn_ref[c * FFN_CW:(c + 1) * FFN_CW, :])
    if final:
        acc = _rms(acc, nf_ref[...])
    o_ref[...] = acc
    cs_ref[0] = carry_scr[...]


def _ffn_prompt(x, nw, w_up, conv_w, conv_b, w_down, nf, final, batch, seq, tm, name):
    m, d = x.shape
    nt = seq // tm
    row = lambda b, i: (b * nt + i, 0)
    const = lambda b, i: (0, 0)
    out, cs = pl.pallas_call(
        functools.partial(_ffn_kernel, final), grid=(batch, nt),
        in_specs=[pl.BlockSpec((tm, d), row), pl.BlockSpec((1, d), const), pl.BlockSpec(w_up.shape, const),
                  pl.BlockSpec(conv_w.shape, const), pl.BlockSpec((1, 2 * D_FF), const),
                  pl.BlockSpec(w_down.shape, const), pl.BlockSpec((1, d), const)],
        out_specs=[pl.BlockSpec((tm, d), row), pl.BlockSpec((1, SUBLANES, 2 * D_FF), lambda b, i: (b, 0, 0))],
        out_shape=[jax.ShapeDtypeStruct((m, d), F32), jax.ShapeDtypeStruct((batch, SUBLANES, 2 * D_FF), F32)],
        scratch_shapes=[pltpu.VMEM((SUBLANES, 2 * D_FF), F32), pltpu.VMEM((tm + SUBLANES, FFN_CW), F32),
                        pltpu.VMEM((tm + SUBLANES, FFN_CW), F32)],
        compiler_params=_params(("arbitrary", "arbitrary")), name=name,
    )(x, nw.reshape(1, d), w_up, conv_w, conv_b.reshape(1, -1), w_down, nf.reshape(1, d))
    return out, cs[:, SUBLANES - (FFN_CONV - 1):, :]


def _ffn_step_kernel(final, x_ref, nw_ref, wg_ref, wv_ref, g0_ref, g1_ref, v0_ref, v1_ref,
                     cwg_ref, cwv_ref, cbg_ref, cbv_ref, wdn_ref, nf_ref, o_ref, ug_ref, uv_ref, acc_scr):
    c = pl.program_id(0)

    @pl.when(c == 0)
    def _():
        acc_scr[...] = x_ref[...]

    xb = _rms(x_ref[...], nw_ref[...]).astype(BF16)
    ug = _dot(xb, wg_ref[...])
    uv = _dot(xb, wv_ref[...])
    ug_ref[...] = ug
    uv_ref[...] = uv
    gate = cbg_ref[...] + cwg_ref[0:1, :] * g0_ref[...] + cwg_ref[1:2, :] * g1_ref[...] + cwg_ref[2:3, :] * ug
    val = cbv_ref[...] + cwv_ref[0:1, :] * v0_ref[...] + cwv_ref[1:2, :] * v1_ref[...] + cwv_ref[2:3, :] * uv
    acc_scr[...] += _dot((_silu(gate) * val).astype(BF16), wdn_ref[...])

    @pl.when(c == pl.num_programs(0) - 1)
    def _():
        acc = acc_scr[...]
        o_ref[...] = _rms(acc, nf_ref[...]) if final else acc


def _ffn_step(x, nw, w_up, conv_w, conv_b, w_down, nf, final, buf, name):
    ns, d = x.shape
    cw = FFN_CW
    nch = D_FF // cw
    ntot = 2 * D_FF // cw
    bufw = buf.reshape(ns, 2 * 2 * D_FF)
    cb = conv_b.reshape(1, -1)
    const = lambda c: (0, 0)
    gcol = lambda c: (0, c)
    vcol = lambda c: (0, nch + c)
    out, u_g, u_v = pl.pallas_call(
        functools.partial(_ffn_step_kernel, final), grid=(nch,),
        in_specs=[pl.BlockSpec((ns, d), const), pl.BlockSpec((1, d), const),
                  pl.BlockSpec((d, cw), gcol), pl.BlockSpec((d, cw), vcol),
                  pl.BlockSpec((ns, cw), gcol), pl.BlockSpec((ns, cw), lambda c: (0, ntot + c)),
                  pl.BlockSpec((ns, cw), vcol), pl.BlockSpec((ns, cw), lambda c: (0, ntot + nch + c)),
                  pl.BlockSpec((FFN_CONV, cw), gcol), pl.BlockSpec((FFN_CONV, cw), vcol),
                  pl.BlockSpec((1, cw), gcol), pl.BlockSpec((1, cw), vcol),
                  pl.BlockSpec((cw, d), lambda c: (c, 0)), pl.BlockSpec((1, d), const)],
        out_specs=[pl.BlockSpec((ns, d), const), pl.BlockSpec((ns, cw), gcol), pl.BlockSpec((ns, cw), gcol)],
        out_shape=[jax.ShapeDtypeStruct((ns, d), F32), jax.ShapeDtypeStruct((ns, D_FF), F32),
                   jax.ShapeDtypeStruct((ns, D_FF), F32)],
        scratch_shapes=[pltpu.VMEM((ns, d), F32)],
        compiler_params=_params(("arbitrary",)), name=name,
    )(x, nw.reshape(1, d), w_up, w_up, bufw, bufw, bufw, bufw, conv_w, conv_w, cb, cb, w_down, nf.reshape(1, d))
    u = jnp.concatenate([u_g, u_v], axis=1)
    return out, jnp.stack([buf[:, 1], u], axis=1)


def _ssd_kernel(z_ref, xbc_ref, dc_ref, dr_ref, cw_ref, cb_ref, dbc_ref, dbr_ref, alc_ref, alr_ref,
                dsk_ref, nw_ref, y_ref, hout_ref, cs_ref, h_scr, carry_scr, ext_scr, y_scr):
    ci = pl.program_id(1)
    ln = CHUNK_C
    pad = SUBLANES

    @pl.when(ci == 0)
    def _():
        h_scr[...] = jnp.zeros_like(h_scr)
        carry_scr[...] = jnp.zeros_like(carry_scr)

    raw = xbc_ref[...]
    ext_scr[0:pad, :] = carry_scr[...]
    ext_scr[pad:pad + ln, :] = raw
    carry_scr[...] = ext_scr[ln:ln + pad, :]
    y = cb_ref[...] + cw_ref[CONV_C - 1:CONV_C, :] * raw
    for j in range(CONV_C - 1):
        off = pad - (CONV_C - 1) + j
        y = y + cw_ref[j:j + 1, :] * ext_scr[off:off + ln, :]
    xbc = _silu(y)
    bw = NG_C * DS_C
    xs, bm, cm = xbc[:, :DI_C], xbc[:, DI_C:DI_C + bw], xbc[:, DI_C + bw:]

    row = lax.broadcasted_iota(jnp.int32, (ln, ln), 0)
    col = lax.broadcasted_iota(jnp.int32, (ln, ln), 1)
    tril = row >= col
    lane_lo = lax.broadcasted_iota(jnp.int32, (ln, LANES), 1) < HD_C
    row_lo = lax.broadcasted_iota(jnp.int32, (2 * HD_C, DS_C), 0) < HD_C
    dt_c = _softplus(dc_ref[...] + dbc_ref[...])
    dt_r = _softplus(dr_ref[...] + dbr_ref[...])
    ac_c = dt_c * (-jnp.exp(alc_ref[...]))
    ac_r = dt_r * (-jnp.exp(alr_ref[...]))
    acum_c = _dot(tril.astype(F32), ac_c, NN, HIGHEST)
    acum_r = _dot(ac_r, (row <= col).astype(F32), NN, HIGHEST)
    hpg = NH_C // NG_C
    for g in range(NG_C):
        bg = bm[:, g * DS_C:(g + 1) * DS_C]
        cg = cm[:, g * DS_C:(g + 1) * DS_C]
        cb_mat = _bdot(cg, bg, NT)
        for pi in range(hpg // 2):
            pair = g * (hpg // 2) + pi
            h0, h1 = 2 * pair, 2 * pair + 1
            xp = xs[:, pair * LANES:(pair + 1) * LANES]
            hp = h_scr[pair]
            a0, a1 = acum_c[:, h0:h0 + 1], acum_c[:, h1:h1 + 1]

            def mmat(h, a_col):
                decay = jnp.exp(jnp.where(tril, a_col - acum_r[h:h + 1, :], -jnp.inf))
                return cb_mat * decay * dt_r[h:h + 1, :]

            yy = _bdot(mmat(h0, a0), jnp.where(lane_lo, xp, 0.0)) + _bdot(mmat(h1, a1), jnp.where(lane_lo, 0.0, xp))
            yy = yy + jnp.where(lane_lo, jnp.exp(a0), jnp.exp(a1)) * _bdot(cg, hp, NT)
            e0, e1 = a0[ln - 1:ln, :], a1[ln - 1:ln, :]
            wend = jnp.where(lane_lo, jnp.exp(e0 - a0) * dt_c[:, h0:h0 + 1], jnp.exp(e1 - a1) * dt_c[:, h1:h1 + 1])
            h_scr[pair] = jnp.where(row_lo, jnp.exp(e0), jnp.exp(e1)) * hp + _bdot(xp * wend, bg, TN)
            y_scr[:, pair * LANES:(pair + 1) * LANES] = yy + dsk_ref[:, pair * LANES:(pair + 1) * LANES] * xp
    gated = y_scr[...] * _silu(z_ref[...])
    gw = DI_C // NG_C
    for g in range(NG_C):
        sl = slice(g * gw, (g + 1) * gw)
        y_ref[:, sl] = _rms(gated[:, sl], nw_ref[:, sl])
    cs_ref[0] = carry_scr[...]

    @pl.when(ci == pl.num_programs(1) - 1)
    def _():
        hout_ref[0] = h_scr[...]


def _head_lanes(v, width):
    return jnp.repeat(v.astype(F32), width).reshape(1, -1)


def _ssd_prompt(z, xbc, d_col, d_row, conv_w, conv_b, dt_bias, a_log, d_skip, norm_w, batch, seq):
    m = z.shape[0]
    ln = CHUNK_C
    nc = seq // ln
    npair = NH_C // 2
    lane8 = lambda v: jnp.zeros((1, LANES), F32).at[0, :NH_C].set(v)
    row = lambda b, c: (b * nc + c, 0)
    const = lambda b, c: (0, 0)
    y, h1, cs = pl.pallas_call(
        _ssd_kernel, grid=(batch, nc),
        in_specs=[pl.BlockSpec((ln, DI_C), row), pl.BlockSpec((ln, CONV_DIM_C), row),
                  pl.BlockSpec((ln, LANES), row), pl.BlockSpec((SUBLANES, ln), lambda b, c: (0, b * nc + c)),
                  pl.BlockSpec((CONV_C, CONV_DIM_C), const), pl.BlockSpec((1, CONV_DIM_C), const),
                  pl.BlockSpec((1, LANES), const), pl.BlockSpec((SUBLANES, 1), const),
                  pl.BlockSpec((1, LANES), const), pl.BlockSpec((SUBLANES, 1), const),
                  pl.BlockSpec((1, DI_C), const), pl.BlockSpec((1, DI_C), const)],
        out_specs=[pl.BlockSpec((ln, DI_C), row),
                   pl.BlockSpec((1, npair, 2 * HD_C, DS_C), lambda b, c: (b, 0, 0, 0)),
                   pl.BlockSpec((1, SUBLANES, CONV_DIM_C), lambda b, c: (b, 0, 0))],
        out_shape=[jax.ShapeDtypeStruct((m, DI_C), F32),
                   jax.ShapeDtypeStruct((batch, npair, 2 * HD_C, DS_C), F32),
                   jax.ShapeDtypeStruct((batch, SUBLANES, CONV_DIM_C), F32)],
        scratch_shapes=[pltpu.VMEM((npair, 2 * HD_C, DS_C), F32), pltpu.VMEM((SUBLANES, CONV_DIM_C), F32),
                        pltpu.VMEM((ln + SUBLANES, CONV_DIM_C), F32), pltpu.VMEM((ln, DI_C), F32)],
        compiler_params=_params(("arbitrary", "arbitrary")), name="ssd_prompt",
    )(z, xbc, d_col, d_row, conv_w, conv_b.reshape(1, -1), lane8(dt_bias), dt_bias.reshape(NH_C, 1),
      lane8(a_log), a_log.reshape(NH_C, 1), _head_lanes(d_skip, HD_C), norm_w.reshape(1, DI_C))
    return y, h1.reshape(batch, NH_C, HD_C, DS_C), cs[:, SUBLANES - (CONV_C - 1):, :]


def _lane_bcast_col(row_vec):
    return jnp.transpose(jnp.broadcast_to(row_vec, (LANES, LANES)))


def _ssd_step_kernel(z_ref, x_ref, cs_ref, d_ref, cw_ref, cb_ref, db_ref, al_ref, dsk_ref, nw_ref, h_ref,
                     y_ref, hout_ref):
    y = cb_ref[...] + cw_ref[CONV_C - 1:CONV_C, :] * x_ref[0]
    for j in range(CONV_C - 1):
        y = y + cw_ref[j:j + 1, :] * cs_ref[0, j:j + 1, :]
    xbc = _silu(y)
    bw = NG_C * DS_C
    dt = _softplus(d_ref[0] + db_ref[...])
    ea = jnp.exp(dt * (-jnp.exp(al_ref[...])))
    row_lo = lax.broadcasted_iota(jnp.int32, (LANES, 1), 0) < HD_C
    lane_lo = lax.broadcasted_iota(jnp.int32, (1, LANES), 1) < HD_C
    hpg = NH_C // NG_C
    parts = []
    for pair in range(NH_C // 2):
        g = (2 * pair) // hpg
        h0, h1 = 2 * pair, 2 * pair + 1
        xp = xbc[:, pair * LANES:(pair + 1) * LANES]
        bg = xbc[:, DI_C + g * DS_C:DI_C + (g + 1) * DS_C]
        cg = xbc[:, DI_C + bw + g * DS_C:DI_C + bw + (g + 1) * DS_C]
        hp = h_ref[0, pair]
        x_col = _lane_bcast_col(xp)
        ea_col = jnp.where(row_lo, ea[:, h0:h0 + 1], ea[:, h1:h1 + 1])
        dt_col = jnp.where(row_lo, dt[:, h0:h0 + 1], dt[:, h1:h1 + 1])
        hout_ref[0, pair] = ea_col * hp + (dt_col * x_col) * bg
        y_inter = jnp.sum(hp * cg, axis=1, keepdims=True)
        y_col = ea_col * y_inter
        y_row = jnp.transpose(jnp.broadcast_to(y_col, (LANES, LANES)))[0:1, :]
        dt_row = jnp.where(lane_lo, dt[:, h0:h0 + 1], dt[:, h1:h1 + 1])
        cb = jnp.sum(cg * bg, axis=1, keepdims=True)
        parts.append(y_row + (cb * dt_row) * xp + dsk_ref[:, pair * LANES:(pair + 1) * LANES] * xp)
    gated = jnp.concatenate(parts, axis=1) * _silu(z_ref[0])
    gw = DI_C // NG_C
    for g in range(NG_C):
        sl = slice(g * gw, (g + 1) * gw)
        y_ref[0, :, sl] = _rms(gated[:, sl], nw_ref[:, sl])


def _ssd_step(z, xbc, d_col, conv_state, conv_w, conv_b, dt_bias, a_log, d_skip, norm_w, h0):
    ns = z.shape[0]
    npair = NH_C // 2
    lane8 = lambda v: jnp.zeros((1, LANES), F32).at[0, :NH_C].set(v)
    s3 = lambda s: (s, 0, 0)
    s4 = lambda s: (s, 0, 0, 0)
    const = lambda s: (0, 0)
    y, h1 = pl.pallas_call(
        _ssd_step_kernel, grid=(ns,),
        in_specs=[pl.BlockSpec((1, 1, DI_C), s3), pl.BlockSpec((1, 1, CONV_DIM_C), s3),
                  pl.BlockSpec((1, CONV_C - 1, CONV_DIM_C), s3), pl.BlockSpec((1, 1, LANES), s3),
                  pl.BlockSpec((CONV_C, CONV_DIM_C), const), pl.BlockSpec((1, CONV_DIM_C), const),
                  pl.BlockSpec((1, LANES), const), pl.BlockSpec((1, LANES), const),
                  pl.BlockSpec((1, DI_C), const), pl.BlockSpec((1, DI_C), const),
                  pl.BlockSpec((1, npair, 2 * HD_C, DS_C), s4)],
        out_specs=[pl.BlockSpec((1, 1, DI_C), s3), pl.BlockSpec((1, npair, 2 * HD_C, DS_C), s4)],
        out_shape=[jax.ShapeDtypeStruct((ns, 1, DI_C), F32), jax.ShapeDtypeStruct((ns, npair, 2 * HD_C, DS_C), F32)],
        compiler_params=_params(("arbitrary",)), name="ssd_step",
    )(z.reshape(ns, 1, DI_C), xbc.reshape(ns, 1, CONV_DIM_C), conv_state, d_col.reshape(ns, 1, LANES),
      conv_w, conv_b.reshape(1, -1), lane8(dt_bias), lane8(a_log), _head_lanes(d_skip, HD_C),
      norm_w.reshape(1, DI_C), h0.reshape(ns, npair, 2 * HD_C, DS_C))
    return y.reshape(ns, DI_C), h1.reshape(ns, NH_C, HD_C, DS_C)


def _head_mask(width):
    lane = lax.broadcasted_iota(jnp.int32, (1, width), 1)
    return [(lane >= h * HD_D) & (lane < (h + 1) * HD_D) for h in range(width // HD_D)]


def _swa_kernel(q_ref, kp_ref, kc_ref, vp_ref, vc_ref, acc_ref, m_ref, l_ref):
    t = pl.program_id(2)
    nk = SWA_KEYS
    q = q_ref[0] * (HD_D ** -0.5)
    kk = jnp.concatenate([kp_ref[0], kc_ref[0]], axis=0).astype(BF16)
    vv = jnp.concatenate([vp_ref[0], vc_ref[0]], axis=0).astype(BF16)
    qi = lax.broadcasted_iota(jnp.int32, (nk, 2 * nk), 0)
    kj = lax.broadcasted_iota(jnp.int32, (nk, 2 * nk), 1)
    dist = nk + qi - kj
    ok = (dist >= 0) & (dist <= nk) & ((t > 0) | (kj >= nk))
    acc = jnp.zeros((nk, SWA_W), F32)
    mm = jnp.zeros((nk, SWA_W), F32)
    ll = jnp.zeros((nk, SWA_W), F32)
    for hm in _head_mask(SWA_W):
        s = _dot(jnp.where(hm, q, 0.0).astype(BF16), kk, NT)
        s = jnp.where(ok, s, -jnp.inf)
        mx = jnp.max(s, axis=1, keepdims=True)
        p = jnp.exp(s - mx)
        acc = jnp.where(hm, _dot(p.astype(BF16), vv), acc)
        mm = jnp.where(hm, mx, mm)
        ll = jnp.where(hm, jnp.sum(p, axis=1, keepdims=True), ll)
    acc_ref[0] = acc
    m_ref[0] = mm
    l_ref[0] = ll


def _swa_prompt(qd, kd, vd, g, dil, batch, seq):
    ngrp = qd.shape[1] // SWA_W
    sub = seq // dil
    nt = sub // SWA_KEYS
    view = lambda a: a.reshape(batch, sub, dil * ngrp * SWA_W)
    cur = pl.BlockSpec((1, SWA_KEYS, SWA_W), lambda b, r, t: (b, t, r * ngrp + g))
    prev = pl.BlockSpec((1, SWA_KEYS, SWA_W), lambda b, r, t: (b, jnp.maximum(t - 1, 0), r * ngrp + g))
    ospec = pl.BlockSpec((1, SWA_KEYS, SWA_W), lambda b, r, t: (b, t, r))
    oshape = jax.ShapeDtypeStruct((batch, sub, dil * SWA_W), F32)
    outs = pl.pallas_call(
        _swa_kernel, grid=(batch, dil, nt),
        in_specs=[cur, prev, cur, prev, cur], out_specs=[ospec] * 3, out_shape=[oshape] * 3,
        compiler_params=_params(("arbitrary", "arbitrary", "arbitrary")), name=f"swa_prompt_{g}",
    )(view(qd), view(kd), view(kd), view(vd), view(vd))
    return [o.reshape(batch * seq, SWA_W) for o in outs]


def _swa_step_kernel(q_ref, kn_ref, vn_ref, c0_ref, c1_ref, c2_ref, acc_ref, m_ref, l_ref):
    r = lax.broadcasted_iota(jnp.int32, (SWA_W, SWA_W), 0) // HD_D
    c = lax.broadcasted_iota(jnp.int32, (SWA_W, SWA_W), 1) // HD_D
    ones_blk = (r == c).astype(F32)
    for g, c_ref in enumerate((c0_ref, c1_ref, c2_ref)):
        sl = slice(g * SWA_W, (g + 1) * SWA_W)
        q = q_ref[0, :, sl] * (HD_D ** -0.5)
        kc, vc = c_ref[0, :, :SWA_W], c_ref[0, :, SWA_W:]
        kn, vn = kn_ref[0, :, sl], vn_ref[0, :, sl]
        s = _dot(kc * q, ones_blk, NN, HIGHEST)
        sn = _dot(jnp.broadcast_to(kn * q, (SUBLANES, SWA_W)), ones_blk, NN, HIGHEST)[0:1, :]
        mx = jnp.maximum(jnp.max(s, axis=0, keepdims=True), sn)
        p = jnp.exp(s - mx)
        pn = jnp.exp(sn - mx)
        acc_ref[g, 0] = jnp.sum(p * vc, axis=0, keepdims=True) + pn * vn
        m_ref[g, 0] = mx
        l_ref[g, 0] = jnp.sum(p, axis=0, keepdims=True) + pn


def _swa_step(qd, kd, vd, caches):
    ns = qd.shape[0]
    w3 = qd.shape[1]
    views = []
    for (win, dil), cache in zip(SWA_GROUPS, caches):
        assert cache.shape[1] == win
        views.append(cache.reshape(ns, win // dil, dil * 2 * SWA_W))
    s3 = lambda s: (s, 0, 0)
    row = pl.BlockSpec((1, 1, w3), s3)
    ospec = pl.BlockSpec((len(SWA_GROUPS), 1, 1, SWA_W), lambda s: (0, s, 0, 0))
    oshape = jax.ShapeDtypeStruct((len(SWA_GROUPS), ns, 1, SWA_W), F32)
    outs = pl.pallas_call(
        _swa_step_kernel, grid=(ns,),
        in_specs=[row, row, row] + [pl.BlockSpec((1, SWA_KEYS, 2 * SWA_W), s3)] * 3,
        out_specs=[ospec] * 3, out_shape=[oshape] * 3,
        compiler_params=_params(("arbitrary",)), name="swa_step",
    )(qd.reshape(ns, 1, w3), kd.reshape(ns, 1, w3), vd.reshape(ns, 1, w3), *views)
    return [[o[g].reshape(ns, SWA_W) for o in outs] for g in range(len(SWA_GROUPS))]


def _swa_merge_kernel(*refs):
    ins, o_ref = refs[:-1], refs[-1]
    ng = len(ins) // 3
    mx = ins[1][...]
    for g in range(1, ng):
        mx = jnp.maximum(mx, ins[3 * g + 1][...])
    num = jnp.zeros_like(mx)
    den = jnp.zeros_like(mx)
    for g in range(ng):
        wgt = jnp.exp(ins[3 * g + 1][...] - mx)
        num = num + wgt * ins[3 * g][...]
        den = den + wgt * ins[3 * g + 2][...]
    o_ref[...] = num / den


def _swa_merge(parts, tm):
    flat = [a for grp in parts for a in grp]
    m = flat[0].shape[0]
    spec = pl.BlockSpec((tm, SWA_W), lambda i: (i, 0))
    return pl.pallas_call(
        _swa_merge_kernel, grid=(m // tm,), in_specs=[spec] * len(flat), out_specs=spec,
        out_shape=jax.ShapeDtypeStruct((m, SWA_W), F32),
        compiler_params=_params(("arbitrary",)), name="swa_merge",
    )(*flat)


def _split_cols(w, sizes):
    out, start = [], 0
    for s in sizes:
        out.append(w[:, start:start + s])
        start += s
    return out


def _trunk(x3, pos0, state, wts, tm):
    batch, seq, d = x3.shape
    m = batch * seq
    x = x3.reshape(m, d)
    pos = pos0 + jnp.arange(seq, dtype=jnp.int32)
    prompt = state is None
    if not prompt:
        pos = jnp.broadcast_to(pos, (m,))
    tabs_b, half_b = _rope_tables(pos, HD_B)
    tabs_d, half_d = _rope_tables(pos, HD_D)
    wa = NH_A * DK_A

    a_main, qb, kb, vb, g_col, g_row = _inproj(
        x, wts["norm_mix"][0], tabs_b, wts["l0_w"], (0, half_b, half_b, 0), wts["l0_small"], tm, "inproj_l0")
    if prompt:
        ha, c1, n1, m1 = _mlstm_prompt(a_main, g_col, g_row, wts["b_gates"], wts["norm_mlstm"], batch, seq)
        hb = _moba_prompt(qb, kb, vb, batch, seq)
    else:
        c0, n0, m0, k_cache, v_cache, page_table = state["even"]
        ha, c1, n1, m1 = _mlstm_step(a_main, g_col, wts["b_gates"], wts["norm_mlstm"], c0, n0, m0)
        ha = ha.reshape(m, wa)
        _, sel = _moba_select(qb, k_cache, page_table)
        hb = _moba_step(qb, kb, vb, k_cache, v_cache, page_table, sel[:, :NH_B, :MOBA_TOPK])
    m1 = m1[:, :, 0]
    x = _outproj(ha, hb, wts["l0_out_a"], wts["l0_out_b"], x, tm, "outproj_l0")
    ffn = wts["ffn"]
    if prompt:
        x, fbuf0 = _ffn_prompt(x, wts["norm_ffn"][0], *ffn[0], wts["norm_final"], False, batch, seq,
                               min(tm, 256), "ffn_l0")
    else:
        x, fbuf0 = _ffn_step(x, wts["norm_ffn"][0], *ffn[0], wts["norm_final"], False, state["ffn"][0], "ffn_step_l0")

    z, xbc, qd, kd, vd, d_col, d_row = _inproj(
        x, wts["norm_mix"][1], tabs_d, wts["l1_w"], (0, 0, half_d, half_d, 0), wts["l1_small"], tm, "inproj_l1")
    ssd_w = wts["ssd"]
    if prompt:
        yc, h1, conv1 = _ssd_prompt(z, xbc, d_col, d_row, *ssd_w, batch, seq)
        parts = [_swa_prompt(qd, kd, vd, g, dil, batch, seq) for g, (_, dil) in enumerate(SWA_GROUPS)]
    else:
        h0, conv0, caches = state["odd"]
        yc, h1 = _ssd_step(z, xbc, d_col, conv0, *ssd_w, h0)
        conv1 = jnp.concatenate([conv0[:, 1:], xbc[:, None, :]], axis=1)
        parts = _swa_step(qd, kd, vd, caches)
    od = _swa_merge(parts, tm)
    x = _outproj(yc, od, wts["l1_out_a"], wts["l1_out_b"], x, tm, "outproj_l1")
    if prompt:
        x, fbuf1 = _ffn_prompt(x, wts["norm_ffn"][1], *ffn[1], wts["norm_final"], True, batch, seq,
                               min(tm, 256), "ffn_l1")
    else:
        x, fbuf1 = _ffn_step(x, wts["norm_ffn"][1], *ffn[1], wts["norm_final"], True, state["ffn"][1], "ffn_step_l1")

    kr = kb.reshape(batch, seq, KVH_B, HD_B)
    vr = vb.reshape(batch, seq, KVH_B, HD_B)
    rows = []
    for g, (win, _) in enumerate(SWA_GROUPS):
        keep = min(win, seq) if prompt else seq
        kg = kd.reshape(batch, seq, -1)[:, seq - keep:, g * SWA_W:(g + 1) * SWA_W].reshape(batch, keep, HPG_D, HD_D)
        vg = vd.reshape(batch, seq, -1)[:, seq - keep:, g * SWA_W:(g + 1) * SWA_W].reshape(batch, keep, HPG_D, HD_D)
        rows.append(jnp.stack([kg, vg], axis=2))
    return (x.reshape(batch, seq, d), (c1, n1, m1, kr, vr), (h1, conv1, rows[0], rows[1], rows[2]),
            jnp.stack([fbuf0, fbuf1]))


def kernel(x_prompt, x_sample, state_l0_mlstm_c, state_l0_mlstm_n, state_l0_mlstm_m, cache_l0_moba_k, cache_l0_moba_v, state_l1_ssd_h, state_l1_ssd_conv, cache_l1_swa_kv0, cache_l1_swa_kv1, cache_l1_swa_kv2, state_ffn_conv, page_table, norm_mix, norm_ffn, norm_final, w_in_l0, b_gates_l0, norm_mlstm_l0, w_out_l0, w_in_l1, conv_w_l1, conv_b_l1, dt_bias_l1, a_log_l1, d_skip_l1, norm_ssd_l1, w_out_l1, ffn_up, ffn_conv_w, ffn_conv_b, ffn_down):
    wa = NH_A * DK_A
    qa, ka, va, oa, ia, fa, qb, kb, vb = _split_cols(
        w_in_l0, (wa, wa, wa, wa, NH_A, NH_A, NH_B * HD_B, KVH_B * HD_B, KVH_B * HD_B))
    wd = 3 * SWA_W
    z, xbc, dtr, qd, kd, vd = _split_cols(w_in_l1, (DI_C, CONV_DIM_C, NH_C, wd, wd, wd))
    bf = lambda a: a.astype(BF16)
    wts = {
        "norm_mix": norm_mix, "norm_ffn": norm_ffn, "norm_final": norm_final,
        "l0_w": [bf(jnp.concatenate([qa, ka, va, oa], axis=1)), bf(qb), bf(kb), bf(vb)],
        "l0_small": jnp.concatenate([ia, fa], axis=1),
        "b_gates": b_gates_l0, "norm_mlstm": norm_mlstm_l0,
        "l0_out_a": bf(w_out_l0[:wa]), "l0_out_b": bf(w_out_l0[wa:]),
        "l1_w": [bf(z), bf(xbc), bf(qd), bf(kd), bf(vd)],
        "l1_small": dtr,
        "ssd": (conv_w_l1, conv_b_l1, dt_bias_l1, a_log_l1, d_skip_l1, norm_ssd_l1),
        "l1_out_a": bf(w_out_l1[:DI_C]), "l1_out_b": bf(w_out_l1[DI_C:]),
        "ffn": [(bf(ffn_up[l]), ffn_conv_w[l], ffn_conv_b[l], bf(ffn_down[l])) for l in range(ffn_up.shape[0])],
    }
    n_seq, n_pages = page_table.shape
    past_len = n_pages * cache_l0_moba_k.shape[1]
    y_p, ev_p, od_p, ffn_p = _trunk(x_prompt, 0, None, wts, 512)
    state = {
        "even": (state_l0_mlstm_c, state_l0_mlstm_n, state_l0_mlstm_m, cache_l0_moba_k, cache_l0_moba_v, page_table),
        "odd": (state_l1_ssd_h, state_l1_ssd_conv, (cache_l1_swa_kv0, cache_l1_swa_kv1, cache_l1_swa_kv2)),
        "ffn": state_ffn_conv,
    }
    y_s, ev_s, od_s, ffn_s = _trunk(x_sample, past_len, state, wts, x_sample.shape[0] * x_sample.shape[1])
    c_p, n_p, m_p, k_p, v_p = ev_p
    c_s, n_s, m_s, k_s, v_s = ev_s
    h_p, conv_p, sw0_p, sw1_p, sw2_p = od_p
    h_s, conv_s, sw0_s, sw1_s, sw2_s = od_s
    return (y_p, y_s, c_p, c_s, n_p, n_s, m_p, m_s, k_p, k_s, v_p, v_s, h_p, h_s, conv_p, conv_s,
            sw0_p, sw0_s, sw1_p, sw1_s, sw2_p, sw2_s, ffn_p, ffn_s)
```

```python
import functools
import math

import jax
import jax.numpy as jnp
from jax import lax
from jax.experimental import pallas as pl
from jax.experimental.pallas import tpu as pltpu

F32 = jnp.float32
BF16 = jnp.bfloat16
HIGHEST = lax.Precision.HIGHEST

EPS = 1e-6
ROPE_THETA = 500000.0
ROPE_FRACTION = 4
LANES = 128
SUBLANES = 8
VMEM_LIMIT = 56 * 1024 * 1024

NH_A, DK_A, DV_A, CHUNK_A = 4, 128, 128, 128
NH_B, KVH_B, HD_B, MOBA_BLOCK, MOBA_TOPK = 4, 2, 128, 256, 3
NH_C, HD_C, NG_C, DS_C, CONV_C, CHUNK_C = 8, 64, 2, 128, 4, 128
DI_C = NH_C * HD_C
CONV_DIM_C = DI_C + 2 * NG_C * DS_C
SWA_GROUPS = ((128, 1), (512, 4), (2048, 16))
HPG_D, HD_D = 4, 64
SWA_W = HPG_D * HD_D
SWA_KEYS = 128
D_FF, FFN_CONV = 2816, 3
FFN_CW = 256

NN = (((1,), (0,)), ((), ()))
NT = (((1,), (1,)), ((), ()))
TN = (((0,), (0,)), ((), ()))


def _dot(a, b, dims=NN, precision=None):
    return lax.dot_general(a, b, dims, precision=precision, preferred_element_type=F32)


def _bdot(a, b, dims=NN):
    return _dot(a.astype(BF16), b.astype(BF16), dims)


def _params(sem, vmem=VMEM_LIMIT):
    return pltpu.CompilerParams(dimension_semantics=sem, vmem_limit_bytes=vmem)


def _rms(x, w):
    return x * lax.rsqrt(jnp.mean(x * x, axis=-1, keepdims=True) + EPS) * w


def _sigmoid(x):
    return 1.0 / (1.0 + jnp.exp(-x))


def _silu(x):
    return x * _sigmoid(x)


def _log_sigmoid(x):
    return jnp.minimum(x, 0.0) - jnp.log1p(jnp.exp(-jnp.abs(x)))


def _softplus(x):
    return jnp.maximum(x, 0.0) + jnp.log1p(jnp.exp(-jnp.abs(x)))


def _rope_lanes(y, cos, sa, sb, half):
    parts = []
    for g in range(y.shape[1] // LANES):
        yg = y[:, g * LANES:(g + 1) * LANES]
        parts.append(yg * cos + pltpu.roll(yg, LANES - half, axis=1) * sa + pltpu.roll(yg, half, axis=1) * sb)
    return parts


def _inproj_kernel(rope_half, x_ref, nw_ref, cos_ref, sa_ref, sb_ref, *refs):
    n = len(rope_half)
    w_refs, ws_ref, wst_ref = refs[:n], refs[n], refs[n + 1]
    out_refs, sc_ref, sr_ref = refs[n + 2:2 * n + 2], refs[2 * n + 2], refs[2 * n + 3]
    xn = _rms(x_ref[...], nw_ref[...])
    xb = xn.astype(BF16)
    for w_ref, o_ref, half in zip(w_refs, out_refs, rope_half):
        y = _dot(xb, w_ref[...])
        if half:
            parts = _rope_lanes(y, cos_ref[...], sa_ref[...], sb_ref[...], half)
            for g, p in enumerate(parts):
                o_ref[:, g * LANES:(g + 1) * LANES] = p
        else:
            o_ref[...] = y
    sc_ref[...] = _dot(xn, ws_ref[...], NN, HIGHEST)
    sr_ref[...] = _dot(wst_ref[...], xn, NT, HIGHEST)


def _inproj(x, nw, tabs, weights, rope_half, w_small, tm, name):
    m, d = x.shape
    cos, sa, sb = tabs
    trows = cos.shape[0]
    tb = trows // tm if trows >= tm else 1
    ws = jnp.zeros((d, LANES), F32).at[:, :w_small.shape[1]].set(w_small)
    wst = jnp.transpose(w_small)
    row = lambda i: (i, 0)
    const = lambda i: (0, 0)
    tab_spec = pl.BlockSpec((tm, LANES), lambda i: (i % tb, 0))
    in_specs = [pl.BlockSpec((tm, d), row), pl.BlockSpec((1, d), const), tab_spec, tab_spec, tab_spec]
    in_specs += [pl.BlockSpec(w.shape, const) for w in weights]
    in_specs += [pl.BlockSpec(ws.shape, const), pl.BlockSpec(wst.shape, const)]
    out_shape = [jax.ShapeDtypeStruct((m, w.shape[1]), F32) for w in weights]
    out_shape += [jax.ShapeDtypeStruct((m, LANES), F32), jax.ShapeDtypeStruct((SUBLANES, m), F32)]
    out_specs = [pl.BlockSpec((tm, w.shape[1]), row) for w in weights]
    out_specs += [pl.BlockSpec((tm, LANES), row), pl.BlockSpec((SUBLANES, tm), lambda i: (0, i))]
    return pl.pallas_call(
        functools.partial(_inproj_kernel, tuple(rope_half)),
        grid=(m // tm,), in_specs=in_specs, out_specs=out_specs, out_shape=out_shape,
        compiler_params=_params(("arbitrary",)), name=name,
    )(x, nw.reshape(1, d), cos, sa, sb, *weights, ws, wst)


def _rope_tables(pos, head_dim):
    rd = head_dim // ROPE_FRACTION
    half = rd // 2
    inv = ROPE_THETA ** (-jnp.arange(half, dtype=F32) / half)
    ang = pos.astype(F32)[:, None] * inv[None, :]
    cos, sin = jnp.cos(ang), jnp.sin(ang)
    ones = jnp.ones((pos.shape[0], head_dim - rd), F32)
    zeros = jnp.zeros((pos.shape[0], head_dim - rd), F32)
    zh = jnp.zeros_like(sin)
    reps = LANES // head_dim
    cos_t = jnp.tile(jnp.concatenate([cos, cos, ones], axis=1), (1, reps))
    sa_t = jnp.tile(jnp.concatenate([-sin, zh, zeros], axis=1), (1, reps))
    sb_t = jnp.tile(jnp.concatenate([zh, sin, zeros], axis=1), (1, reps))
    return (cos_t, sa_t, sb_t), half


def _mlstm_kernel(q_ref, k_ref, v_ref, o_ref, gc_ref, gr_ref, bc_ref, br_ref, nw_ref,
                  h_ref, cout_ref, nout_ref, mout_ref, c_scr, n_scr, m_scr):
    ci = pl.program_id(1)
    ln = CHUNK_A

    @pl.when(ci == 0)
    def _():
        c_scr[...] = jnp.zeros_like(c_scr)
        n_scr[...] = jnp.zeros_like(n_scr)
        m_scr[...] = jnp.zeros_like(m_scr)

    row = lax.broadcasted_iota(jnp.int32, (ln, ln), 0)
    col = lax.broadcasted_iota(jnp.int32, (ln, ln), 1)
    tril = row >= col
    gc = gc_ref[...] + bc_ref[...]
    gr = gr_ref[...] + br_ref[...]
    fcum_c = _dot(tril.astype(F32), _log_sigmoid(gc), NN, HIGHEST)
    fcum_r = _dot(_log_sigmoid(gr), (row <= col).astype(F32), NN, HIGHEST)
    for h in range(NH_A):
        sl = slice(h * DK_A, (h + 1) * DK_A)
        ic_r, ic_c = gr[h:h + 1, :], gc[:, h:h + 1]
        fc_c, fc_r = fcum_c[:, NH_A + h:NH_A + h + 1], fcum_r[NH_A + h:NH_A + h + 1, :]
        m = m_scr[h:h + 1, 0:1]
        c = c_scr[h]
        nrow = n_scr[h:h + 1, :]
        qh = q_ref[:, sl]
        kh = k_ref[:, sl] * (DK_A ** -0.5)
        vh = v_ref[:, sl]
        dmat = jnp.where(tril, fc_c - fc_r + ic_r, -jnp.inf)
        inter = fc_c + m
        mt = jnp.maximum(inter, jnp.max(dmat, axis=1, keepdims=True))
        wmat = _bdot(qh, kh, NT) * jnp.exp(dmat - mt)
        a_inter = jnp.exp(inter - mt)
        num = _bdot(wmat, vh) + a_inter * _bdot(qh, c)
        den = jnp.sum(wmat, axis=1, keepdims=True) + a_inter * jnp.sum(qh * nrow, axis=1, keepdims=True)
        hh = num / jnp.maximum(jnp.abs(den), jnp.exp(-mt))
        f_end = fc_c[ln - 1:ln, :]
        m_new = jnp.maximum(f_end + m, jnp.max(f_end - fc_r + ic_r, axis=1, keepdims=True))
        kw = kh * jnp.exp(f_end - fc_c + ic_c - m_new)
        decay = jnp.exp(f_end + m - m_new)
        c_scr[h] = decay * c + _bdot(kw, vh, TN)
        n_scr[h:h + 1, :] = decay * nrow + jnp.sum(kw, axis=0, keepdims=True)
        m_scr[h:h + 1, :] = jnp.broadcast_to(m_new, (1, LANES))
        ha = _sigmoid(o_ref[:, sl]) * hh
        h_ref[:, sl] = _rms(ha, nw_ref[:, sl])

    @pl.when(ci == pl.num_programs(1) - 1)
    def _():
        cout_ref[0] = c_scr[...]
        nout_ref[0] = n_scr[0:NH_A, :]
        mout_ref[0] = m_scr[0:NH_A, :]


def _mlstm_prompt(a_main, g_col, g_row, b_gates, norm_w, batch, seq):
    m = a_main.shape[0]
    ln = CHUNK_A
    nc = seq // ln
    w = NH_A * DK_A
    blk = lambda j: pl.BlockSpec((ln, w), lambda b, c, j=j: (b * nc + c, j))
    bc = jnp.zeros((1, LANES), F32).at[0, :2 * NH_A].set(b_gates)
    br = b_gates.reshape(2 * NH_A, 1)
    const = lambda b, c: (0, 0)
    return pl.pallas_call(
        _mlstm_kernel, grid=(batch, nc),
        in_specs=[blk(0), blk(1), blk(2), blk(3),
                  pl.BlockSpec((ln, LANES), lambda b, c: (b * nc + c, 0)),
                  pl.BlockSpec((SUBLANES, ln), lambda b, c: (0, b * nc + c)),
                  pl.BlockSpec((1, LANES), const), pl.BlockSpec((SUBLANES, 1), const),
                  pl.BlockSpec((1, w), const)],
        out_specs=[pl.BlockSpec((ln, w), lambda b, c: (b * nc + c, 0)),
                   pl.BlockSpec((1, NH_A, DK_A, DV_A), lambda b, c: (b, 0, 0, 0)),
                   pl.BlockSpec((1, NH_A, DK_A), lambda b, c: (b, 0, 0)),
                   pl.BlockSpec((1, NH_A, LANES), lambda b, c: (b, 0, 0))],
        out_shape=[jax.ShapeDtypeStruct((m, w), F32),
                   jax.ShapeDtypeStruct((batch, NH_A, DK_A, DV_A), F32),
                   jax.ShapeDtypeStruct((batch, NH_A, DK_A), F32),
                   jax.ShapeDtypeStruct((batch, NH_A, LANES), F32)],
        scratch_shapes=[pltpu.VMEM((NH_A, DK_A, DV_A), F32), pltpu.VMEM((SUBLANES, LANES), F32),
                        pltpu.VMEM((SUBLANES, LANES), F32)],
        compiler_params=_params(("arbitrary", "arbitrary")), name="mlstm_prompt",
    )(a_main, a_main, a_main, a_main, g_col, g_row, bc, br, norm_w.reshape(1, w))


def _mlstm_step_kernel(a_ref, qc_ref, kc_ref, g_ref, b_ref, m_ref, c_ref, n_ref, nw_ref,
                       h_ref, cout_ref, nout_ref, mout_ref):
    g = g_ref[0] + b_ref[...]
    lf_all = _log_sigmoid(g)
    w = NH_A * DK_A
    scale = DK_A ** -0.5
    for h in range(NH_A):
        sl = slice(h * DK_A, (h + 1) * DK_A)
        q_row = a_ref[0, :, sl]
        k_row = a_ref[0, :, w + h * DK_A:w + (h + 1) * DK_A] * scale
        v_row = a_ref[0, :, 2 * w + h * DV_A:2 * w + (h + 1) * DV_A]
        o_row = a_ref[0, :, 3 * w + h * DV_A:3 * w + (h + 1) * DV_A]
        q_col = qc_ref[0, sl, :]
        k_col = kc_ref[0, sl, :] * scale
        c = c_ref[0, h]
        n_row = n_ref[0, h:h + 1, :]
        m = m_ref[0, :, h:h + 1]
        ic = g[:, h:h + 1]
        lf = lf_all[:, NH_A + h:NH_A + h + 1]
        inter = lf + m
        mt = jnp.maximum(inter, ic)
        wm = jnp.sum(q_row * k_row, axis=1, keepdims=True) * jnp.exp(ic - mt)
        a_inter = jnp.exp(inter - mt)
        num = wm * v_row + a_inter * jnp.sum(q_col * c, axis=0, keepdims=True)
        den = wm + a_inter * jnp.sum(q_row * n_row, axis=1, keepdims=True)
        hh = num / jnp.maximum(jnp.abs(den), jnp.exp(-mt))
        m_new = jnp.maximum(inter, ic)
        wgt = jnp.exp(ic - m_new)
        decay = jnp.exp(inter - m_new)
        cout_ref[0, h] = decay * c + (wgt * k_col) * v_row
        nout_ref[0, h:h + 1, :] = decay * n_row + wgt * k_row
        mout_ref[0, h:h + 1, :] = jnp.broadcast_to(m_new, (1, LANES))
        ha = _sigmoid(o_row) * hh
        h_ref[0, :, sl] = _rms(ha, nw_ref[:, sl])


def _mlstm_step(a_main, g_col, b_gates, norm_w, c0, n0, m0):
    ns = a_main.shape[0]
    w = NH_A * DK_A
    a3 = a_main.reshape(ns, 1, 4 * w)
    q_col = a_main[:, :w].reshape(ns, w, 1)
    k_col = a_main[:, w:2 * w].reshape(ns, w, 1)
    bc = jnp.zeros((1, LANES), F32).at[0, :2 * NH_A].set(b_gates)
    s3 = lambda s: (s, 0, 0)
    s4 = lambda s: (s, 0, 0, 0)
    const = lambda s: (0, 0)
    return pl.pallas_call(
        _mlstm_step_kernel, grid=(ns,),
        in_specs=[pl.BlockSpec((1, 1, 4 * w), s3), pl.BlockSpec((1, w, 1), s3), pl.BlockSpec((1, w, 1), s3),
                  pl.BlockSpec((1, 1, LANES), s3), pl.BlockSpec((1, LANES), const),
                  pl.BlockSpec((1, 1, NH_A), s3), pl.BlockSpec((1, NH_A, DK_A, DV_A), s4),
                  pl.BlockSpec((1, NH_A, DK_A), s3), pl.BlockSpec((1, w), const)],
        out_specs=[pl.BlockSpec((1, 1, w), s3), pl.BlockSpec((1, NH_A, DK_A, DV_A), s4),
                   pl.BlockSpec((1, NH_A, DK_A), s3), pl.BlockSpec((1, NH_A, LANES), s3)],
        out_shape=[jax.ShapeDtypeStruct((ns, 1, w), F32), jax.ShapeDtypeStruct((ns, NH_A, DK_A, DV_A), F32),
                   jax.ShapeDtypeStruct((ns, NH_A, DK_A), F32), jax.ShapeDtypeStruct((ns, NH_A, LANES), F32)],
        compiler_params=_params(("arbitrary",)), name="mlstm_step",
    )(a3, q_col, k_col, g_col.reshape(ns, 1, LANES), bc, m0.reshape(ns, 1, NH_A), c0, n0, norm_w.reshape(1, w))


def _top_blocks(bs, topk):
    blk_id = lax.broadcasted_iota(jnp.int32, bs.shape, 0).astype(F32)
    sel = jnp.zeros(bs.shape, F32)
    for _ in range(topk):
        mx = jnp.max(bs, axis=0, keepdims=True)
        idx = jnp.min(jnp.where(bs == mx, blk_id, float(bs.shape[0])), axis=0, keepdims=True)
        pick = (blk_id == idx) & (mx > -jnp.inf)
        sel = jnp.where(pick, 1.0, sel)
        bs = jnp.where(pick, -jnp.inf, bs)
    return sel


def _moba_kernel(n_full, q_ref, k_ref, v_ref, o_ref, kmean_scr, kb_scr, vt_scr, sel_scr, m_scr, l_scr, acc_scr):
    qi = pl.program_id(2)
    blk = MOBA_BLOCK
    grp = NH_B // KVH_B

    @pl.when(qi == 0)
    def _():
        kmean_scr[...] = jnp.zeros_like(kmean_scr)
        for n in range(n_full):
            kblk = k_ref[n * blk:(n + 1) * blk, :]
            kmean_scr[n:n + 1, :] = jnp.mean(kblk, axis=0, keepdims=True)
            kb_scr[n] = kblk.astype(BF16)
            vt_scr[n] = jnp.transpose(v_ref[n * blk:(n + 1) * blk, :]).astype(BF16)

    q = jnp.concatenate([q_ref[:, g * HD_B:(g + 1) * HD_B] for g in range(grp)], axis=0)
    rows = grp * blk
    bs = _dot(kmean_scr[...], q, NT, HIGHEST)
    bs = jnp.where(lax.broadcasted_iota(jnp.int32, bs.shape, 0) < qi, bs, -jnp.inf)
    sel_scr[...] = _top_blocks(bs, min(MOBA_TOPK, n_full))
    qs = (q * (HD_B ** -0.5)).astype(BF16)

    key = lax.broadcasted_iota(jnp.int32, (blk, rows), 0)
    qpos = lax.broadcasted_iota(jnp.int32, (blk, rows), 1) % blk
    causal = key <= qpos

    def attend(first, own_group, m_old, l_old, acc_old):
        scores = []
        for g in range(MOBA_GROUP):
            b = first + g
            ok = sel_scr[pl.ds(b, 1), :] > 0.0
            if own_group:
                ok = ((b < qi) & ok) | ((b == qi) & causal)
            scores.append(jnp.where(ok, _dot(kb_scr[b], qs, NT), -jnp.inf))
        m_new = m_old
        for s in scores:
            m_new = jnp.maximum(m_new, jnp.max(s, axis=0, keepdims=True))
        alpha = jnp.exp(m_old - m_new)
        l_new, acc_new = alpha * l_old, alpha * acc_old
        for g, s in enumerate(scores):
            p = jnp.exp(s - m_new)
            l_new = l_new + jnp.sum(p, axis=0, keepdims=True)
            acc_new = acc_new + _dot(vt_scr[first + g], p.astype(BF16))
        m_scr[...] = m_new
        l_scr[...] = l_new
        acc_scr[...] = acc_new

    own_first = pl.multiple_of((qi // MOBA_GROUP) * MOBA_GROUP, MOBA_GROUP)
    attend(own_first, True, jnp.full((1, rows), -jnp.inf, F32), jnp.zeros((1, rows), F32),
           jnp.zeros((HD_B, rows), F32))

    def body(i, carry):
        attend(pl.multiple_of(i * MOBA_GROUP, MOBA_GROUP), False, m_scr[...], l_scr[...], acc_scr[...])
        return carry

    lax.fori_loop(0, qi // MOBA_GROUP, body, 0)
    out = acc_scr[...] / l_scr[...]
    for g in range(grp):
        o_ref[:, g * HD_B:(g + 1) * HD_B] = jnp.transpose(out[:, g * blk:(g + 1) * blk])


MOBA_GROUP = 4


def _moba_prompt(qb, kb, vb, batch, seq):
    m = qb.shape[0]
    blk = MOBA_BLOCK
    nq = seq // blk
    assert seq % (blk * MOBA_GROUP) == 0
    grp = NH_B // KVH_B
    return pl.pallas_call(
        functools.partial(_moba_kernel, seq // blk), grid=(batch, KVH_B, nq),
        in_specs=[pl.BlockSpec((blk, grp * HD_B), lambda b, j, i: (b * nq + i, j)),
                  pl.BlockSpec((seq, HD_B), lambda b, j, i: (b, j)),
                  pl.BlockSpec((seq, HD_B), lambda b, j, i: (b, j))],
        out_specs=pl.BlockSpec((blk, grp * HD_B), lambda b, j, i: (b * nq + i, j)),
        out_shape=jax.ShapeDtypeStruct((m, NH_B * HD_B), F32),
        scratch_shapes=[pltpu.VMEM((-(-nq // SUBLANES) * SUBLANES, HD_B), F32),
                        pltpu.VMEM((nq, blk, HD_B), BF16), pltpu.VMEM((nq, HD_B, blk), BF16),
                        pltpu.VMEM((-(-nq // SUBLANES) * SUBLANES, grp * blk), F32),
                        pltpu.VMEM((1, grp * blk), F32), pltpu.VMEM((1, grp * blk), F32),
                        pltpu.VMEM((HD_B, grp * blk), F32)],
        compiler_params=_params(("arbitrary", "arbitrary", "arbitrary")), name="moba_prompt",
    )(qb, kb, vb)


MOBA_PAGES_PER_STEP = 16


def _moba_select_kernel(n_steps, pt_ref, q_ref, *refs):
    pages, (kmean_ref, sel_ref) = refs[:MOBA_PAGES_PER_STEP], refs[MOBA_PAGES_PER_STEP:]
    j = pl.program_id(1)
    page = pages[0].shape[1] // KVH_B

    def page_sum(p):
        heads = [jnp.sum(p[0, pl.ds(kv, page, stride=KVH_B), :], axis=0, keepdims=True) for kv in range(KVH_B)]
        return jnp.concatenate(heads, axis=1)

    sums = [page_sum(p) for p in pages]
    per_blk = [(sums[2 * i] + sums[2 * i + 1]) * (1.0 / MOBA_BLOCK) for i in range(MOBA_PAGES_PER_STEP // 2)]
    kmean_ref[0, pl.ds(pl.multiple_of(j * SUBLANES, SUBLANES), SUBLANES), :] = jnp.concatenate(per_blk, axis=0)

    @pl.when(j == n_steps - 1)
    def _():
        grp = NH_B // KVH_B
        nb = kmean_ref.shape[1]
        rowi = lax.broadcasted_iota(jnp.int32, (nb, 1), 0).astype(F32)
        orow = lax.broadcasted_iota(jnp.int32, (SUBLANES, LANES), 0)
        olane = lax.broadcasted_iota(jnp.int32, (SUBLANES, LANES), 1)
        out = jnp.zeros((SUBLANES, LANES), F32)
        for h in range(NH_B):
            kv = h // grp
            km = kmean_ref[0, :, kv * HD_B:(kv + 1) * HD_B]
            qh = q_ref[0, :, h * HD_B:(h + 1) * HD_B]
            bs = jnp.sum(km * qh, axis=1, keepdims=True)
            for r in range(MOBA_TOPK):
                mx = jnp.max(bs, axis=0, keepdims=True)
                idx = jnp.min(jnp.where(bs == mx, rowi, float(nb)), axis=0, keepdims=True)
                out = jnp.where((orow == h) & (olane == r), idx, out)
                bs = jnp.where(rowi == idx, -jnp.inf, bs)
        sel_ref[0] = out.astype(jnp.int32)


def _moba_select(qb, k_cache, page_table):
    ns, n_pages = page_table.shape
    page = k_cache.shape[1]
    assert 2 * page == MOBA_BLOCK and n_pages % MOBA_PAGES_PER_STEP == 0
    n_steps = n_pages // MOBA_PAGES_PER_STEP
    kw = KVH_B * HD_B
    kc = k_cache.reshape(k_cache.shape[0], page * KVH_B, HD_B)
    page_spec = lambda i: pl.BlockSpec(
        (1, page * KVH_B, HD_B), lambda s, j, pt, i=i: (pt[s * n_pages + j * MOBA_PAGES_PER_STEP + i], 0, 0))
    nb = n_pages * page // MOBA_BLOCK
    grid_spec = pltpu.PrefetchScalarGridSpec(
        num_scalar_prefetch=1, grid=(ns, n_steps),
        in_specs=[pl.BlockSpec((1, 1, NH_B * HD_B), lambda s, j, pt: (s, 0, 0))]
        + [page_spec(i) for i in range(MOBA_PAGES_PER_STEP)],
        out_specs=[pl.BlockSpec((1, nb, kw), lambda s, j, pt: (s, 0, 0)),
                   pl.BlockSpec((1, SUBLANES, LANES), lambda s, j, pt: (s, 0, 0))])
    return pl.pallas_call(
        functools.partial(_moba_select_kernel, n_steps), grid_spec=grid_spec,
        out_shape=[jax.ShapeDtypeStruct((ns, nb, kw), F32), jax.ShapeDtypeStruct((ns, SUBLANES, LANES), jnp.int32)],
        compiler_params=_params(("arbitrary", "arbitrary")), name="moba_select",
    )(page_table.reshape(-1), qb.reshape(ns, 1, NH_B * HD_B), *([kc] * MOBA_PAGES_PER_STEP))


def _moba_step_kernel(sel_ref, pt_ref, q_ref, kn_ref, vn_ref, *refs):
    pages, o_ref, (m_scr, l_scr, acc_scr) = refs[:4 * NH_B], refs[4 * NH_B], refs[4 * NH_B + 1:]
    r = pl.program_id(1)
    scale = HD_B ** -0.5
    grp = NH_B // KVH_B
    page = pages[0].shape[1] // KVH_B
    for h in range(NH_B):
        kv = h // grp
        row = slice(h, h + 1)
        q = q_ref[0, :, h * HD_B:(h + 1) * HD_B]
        ka_ref, kb_ref, va_ref, vb_ref = pages[4 * h:4 * h + 4]

        @pl.when(r == 0)
        def _():
            s0 = jnp.sum(q * kn_ref[0, :, kv * HD_B:(kv + 1) * HD_B], axis=1, keepdims=True) * scale
            m_scr[row, :] = jnp.broadcast_to(s0, (1, LANES))
            l_scr[row, :] = jnp.ones((1, LANES), F32)
            acc_scr[row, :] = vn_ref[0, :, kv * HD_B:(kv + 1) * HD_B]

        rows = pl.ds(kv, page, stride=KVH_B)
        kk = jnp.concatenate([ka_ref[0, rows, :], kb_ref[0, rows, :]], axis=0)
        vv = jnp.concatenate([va_ref[0, rows, :], vb_ref[0, rows, :]], axis=0)
        s = jnp.sum(kk * q, axis=1, keepdims=True) * scale
        m_old = m_scr[row, 0:1]
        m_new = jnp.maximum(m_old, jnp.max(s, axis=0, keepdims=True))
        alpha = jnp.exp(m_old - m_new)
        p = jnp.exp(s - m_new)
        m_scr[row, :] = jnp.broadcast_to(m_new, (1, LANES))
        l_scr[row, :] = alpha * l_scr[row, :] + jnp.sum(p, axis=0, keepdims=True)
        acc_scr[row, :] = alpha * acc_scr[row, :] + jnp.sum(p * vv, axis=0, keepdims=True)

        @pl.when(r == pl.num_programs(1) - 1)
        def _():
            o_ref[0, :, h * HD_B:(h + 1) * HD_B] = acc_scr[row, :] / l_scr[row, :]


def _moba_step(qb, k_new, v_new, k_cache, v_cache, page_table, sel):
    ns, n_pages = page_table.shape
    page = k_cache.shape[1]
    kw = KVH_B * HD_B
    kc = k_cache.reshape(k_cache.shape[0], page * KVH_B, HD_B)
    vc = v_cache.reshape(v_cache.shape[0], page * KVH_B, HD_B)

    def page_spec(h, half):
        def imap(s, r, sel_r, pt_r):
            blk = sel_r[(s * NH_B + h) * MOBA_TOPK + r]
            return (pt_r[s * n_pages + 2 * blk + half], 0, 0)
        return pl.BlockSpec((1, page * KVH_B, HD_B), imap)

    seq_row = lambda w: pl.BlockSpec((1, 1, w), lambda s, r, a, b: (s, 0, 0))
    page_specs, page_args = [], []
    for h in range(NH_B):
        page_specs += [page_spec(h, 0), page_spec(h, 1), page_spec(h, 0), page_spec(h, 1)]
        page_args += [kc, kc, vc, vc]
    grid_spec = pltpu.PrefetchScalarGridSpec(
        num_scalar_prefetch=2, grid=(ns, MOBA_TOPK),
        in_specs=[seq_row(NH_B * HD_B), seq_row(kw), seq_row(kw)] + page_specs,
        out_specs=seq_row(NH_B * HD_B),
        scratch_shapes=[pltpu.VMEM((SUBLANES, LANES), F32), pltpu.VMEM((SUBLANES, LANES), F32),
                        pltpu.VMEM((SUBLANES, HD_B), F32)])
    out = pl.pallas_call(
        _moba_step_kernel, grid_spec=grid_spec,
        out_shape=jax.ShapeDtypeStruct((ns, 1, NH_B * HD_B), F32),
        compiler_params=_params(("arbitrary", "arbitrary")), name="moba_step",
    )(sel.reshape(-1), page_table.reshape(-1), qb.reshape(ns, 1, NH_B * HD_B),
      k_new.reshape(ns, 1, kw), v_new.reshape(ns, 1, kw), *page_args)
    return out.reshape(ns, NH_B * HD_B)


def _outproj_kernel(a_ref, b_ref, wa_ref, wb_ref, x_ref, o_ref):
    o_ref[...] = x_ref[...] + _bdot(a_ref[...], wa_ref[...]) + _bdot(b_ref[...], wb_ref[...])


def _outproj(a, b, wa, wb, x, tm, name):
    m, d = x.shape
    row = lambda i: (i, 0)
    const = lambda i: (0, 0)
    return pl.pallas_call(
        _outproj_kernel, grid=(m // tm,),
        in_specs=[pl.BlockSpec((tm, a.shape[1]), row), pl.BlockSpec((tm, b.shape[1]), row),
                  pl.BlockSpec(wa.shape, const), pl.BlockSpec(wb.shape, const), pl.BlockSpec((tm, d), row)],
        out_specs=pl.BlockSpec((tm, d), row), out_shape=jax.ShapeDtypeStruct((m, d), F32),
        compiler_params=_params(("arbitrary",)), name=name,
    )(a, b, wa, wb, x)


def _ffn_kernel(final, x_ref, nw_ref, wup_ref, cw_ref, cb_ref, wdn_ref, nf_ref, o_ref, cs_ref,
                carry_scr, extg_scr, extv_scr):
    tm = x_ref.shape[0]

    @pl.when(pl.program_id(1) == 0)
    def _():
        carry_scr[...] = jnp.zeros_like(carry_scr)

    x = x_ref[...]
    xb = _rms(x, nw_ref[...]).astype(BF16)
    acc = x
    pad = SUBLANES

    def conv(ext_scr, col0):
        cols = slice(col0, col0 + FFN_CW)
        u = _dot(xb, wup_ref[:, cols])
        ext_scr[0:pad, :] = carry_scr[:, cols]
        ext_scr[pad:pad + tm, :] = u
        carry_scr[:, cols] = ext_scr[tm:tm + pad, :]
        y = cb_ref[:, cols] + cw_ref[FFN_CONV - 1:FFN_CONV, cols] * u
        for j in range(FFN_CONV - 1):
            off = pad - (FFN_CONV - 1) + j
            y = y + cw_ref[j:j + 1, cols] * ext_scr[off:off + tm, :]
        return y

    for c in range(D_FF // FFN_CW):
        gate = conv(extg_scr, c * FFN_CW)
        val = conv(extv_scr, D_FF + c * FFN_CW)
        act = (_silu(gate) * val).astype(BF16)
        acc = acc + _dot(act, wdn_ref[c * FFN_CW:(c + 1) * FFN_CW, :])
    if final:
        acc = _rms(acc, nf_ref[...])
    o_ref[...] = acc
    cs_ref[0] = carry_scr[...]


def _ffn_prompt(x, nw, w_up, conv_w, conv_b, w_down, nf, final, batch, seq, tm, name):
    m, d = x.shape
    nt = seq // tm
    row = lambda b, i: (b * nt + i, 0)
    const = lambda b, i: (0, 0)
    out, cs = pl.pallas_call(
        functools.partial(_ffn_kernel, final), grid=(batch, nt),
        in_specs=[pl.BlockSpec((tm, d), row), pl.BlockSpec((1, d), const), pl.BlockSpec(w_up.shape, const),
                  pl.BlockSpec(conv_w.shape, const), pl.BlockSpec((1, 2 * D_FF), const),
                  pl.BlockSpec(w_down.shape, const), pl.BlockSpec((1, d), const)],
        out_specs=[pl.BlockSpec((tm, d), row), pl.BlockSpec((1, SUBLANES, 2 * D_FF), lambda b, i: (b, 0, 0))],
        out_shape=[jax.ShapeDtypeStruct((m, d), F32), jax.ShapeDtypeStruct((batch, SUBLANES, 2 * D_FF), F32)],
        scratch_shapes=[pltpu.VMEM((SUBLANES, 2 * D_FF), F32), pltpu.VMEM((tm + SUBLANES, FFN_CW), F32),
                        pltpu.VMEM((tm + SUBLANES, FFN_CW), F32)],
        compiler_params=_params(("arbitrary", "arbitrary")), name=name,
    )(x, nw.reshape(1, d), w_up, conv_w, conv_b.reshape(1, -1), w_down, nf.reshape(1, d))
    return out, cs[:, SUBLANES - (FFN_CONV - 1):, :]


def _ffn_step_kernel(final, x_ref, nw_ref, wg_ref, wv_ref, g0_ref, g1_ref, v0_ref, v1_ref,
                     cwg_ref, cwv_ref, cbg_ref, cbv_ref, wdn_ref, nf_ref, o_ref, ug_ref, uv_ref, acc_scr):
    c = pl.program_id(0)

    @pl.when(c == 0)
    def _():
        acc_scr[...] = x_ref[...]

    xb = _rms(x_ref[...], nw_ref[...]).astype(BF16)
    ug = _dot(xb, wg_ref[...])
    uv = _dot(xb, wv_ref[...])
    ug_ref[...] = ug
    uv_ref[...] = uv
    gate = cbg_ref[...] + cwg_ref[0:1, :] * g0_ref[...] + cwg_ref[1:2, :] * g1_ref[...] + cwg_ref[2:3, :] * ug
    val = cbv_ref[...] + cwv_ref[0:1, :] * v0_ref[...] + cwv_ref[1:2, :] * v1_ref[...] + cwv_ref[2:3, :] * uv
    acc_scr[...] += _dot((_silu(gate) * val).astype(BF16), wdn_ref[...])

    @pl.when(c == pl.num_programs(0) - 1)
    def _():
        acc = acc_scr[...]
        o_ref[...] = _rms(acc, nf_ref[...]) if final else acc


def _ffn_step(x, nw, w_up, conv_w, conv_b, w_down, nf, final, buf, name):
    ns, d = x.shape
    cw = FFN_CW
    nch = D_FF // cw
    ntot = 2 * D_FF // cw
    bufw = buf.reshape(ns, 2 * 2 * D_FF)
    cb = conv_b.reshape(1, -1)
    const = lambda c: (0, 0)
    gcol = lambda c: (0, c)
    vcol = lambda c: (0, nch + c)
    out, u_g, u_v = pl.pallas_call(
        functools.partial(_ffn_step_kernel, final), grid=(nch,),
        in_specs=[pl.BlockSpec((ns, d), const), pl.BlockSpec((1, d), const),
                  pl.BlockSpec((d, cw), gcol), pl.BlockSpec((d, cw), vcol),
                  pl.BlockSpec((ns, cw), gcol), pl.BlockSpec((ns, cw), lambda c: (0, ntot + c)),
                  pl.BlockSpec((ns, cw), vcol), pl.BlockSpec((ns, cw), lambda c: (0, ntot + nch + c)),
                  pl.BlockSpec((FFN_CONV, cw), gcol), pl.BlockSpec((FFN_CONV, cw), vcol),
                  pl.BlockSpec((1, cw), gcol), pl.BlockSpec((1, cw), vcol),
                  pl.BlockSpec((cw, d), lambda c: (c, 0)), pl.BlockSpec((1, d), const)],
        out_specs=[pl.BlockSpec((ns, d), const), pl.BlockSpec((ns, cw), gcol), pl.BlockSpec((ns, cw), gcol)],
        out_shape=[jax.ShapeDtypeStruct((ns, d), F32), jax.ShapeDtypeStruct((ns, D_FF), F32),
                   jax.ShapeDtypeStruct((ns, D_FF), F32)],
        scratch_shapes=[pltpu.VMEM((ns, d), F32)],
        compiler_params=_params(("arbitrary",)), name=name,
    )(x, nw.reshape(1, d), w_up, w_up, bufw, bufw, bufw, bufw, conv_w, conv_w, cb, cb, w_down, nf.reshape(1, d))
    u = jnp.concatenate([u_g, u_v], axis=1)
    return out, jnp.stack([buf[:, 1], u], axis=1)


def _ssd_kernel(z_ref, xbc_ref, dc_ref, dr_ref, cw_ref, cb_ref, dbc_ref, dbr_ref, alc_ref, alr_ref,
                dsk_ref, nw_ref, y_ref, hout_ref, cs_ref, h_scr, carry_scr, ext_scr, y_scr):
    ci = pl.program_id(1)
    ln = CHUNK_C
    pad = SUBLANES

    @pl.when(ci == 0)
    def _():
        h_scr[...] = jnp.zeros_like(h_scr)
        carry_scr[...] = jnp.zeros_like(carry_scr)

    raw = xbc_ref[...]
    ext_scr[0:pad, :] = carry_scr[...]
    ext_scr[pad:pad + ln, :] = raw
    carry_scr[...] = ext_scr[ln:ln + pad, :]
    y = cb_ref[...] + cw_ref[CONV_C - 1:CONV_C, :] * raw
    for j in range(CONV_C - 1):
        off = pad - (CONV_C - 1) + j
        y = y + cw_ref[j:j + 1, :] * ext_scr[off:off + ln, :]
    xbc = _silu(y)
    bw = NG_C * DS_C
    xs, bm, cm = xbc[:, :DI_C], xbc[:, DI_C:DI_C + bw], xbc[:, DI_C + bw:]

    row = lax.broadcasted_iota(jnp.int32, (ln, ln), 0)
    col = lax.broadcasted_iota(jnp.int32, (ln, ln), 1)
    tril = row >= col
    lane_lo = lax.broadcasted_iota(jnp.int32, (ln, LANES), 1) < HD_C
    row_lo = lax.broadcasted_iota(jnp.int32, (2 * HD_C, DS_C), 0) < HD_C
    dt_c = _softplus(dc_ref[...] + dbc_ref[...])
    dt_r = _softplus(dr_ref[...] + dbr_ref[...])
    ac_c = dt_c * (-jnp.exp(alc_ref[...]))
    ac_r = dt_r * (-jnp.exp(alr_ref[...]))
    acum_c = _dot(tril.astype(F32), ac_c, NN, HIGHEST)
    acum_r = _dot(ac_r, (row <= col).astype(F32), NN, HIGHEST)
    hpg = NH_C // NG_C
    for g in range(NG_C):
        bg = bm[:, g * DS_C:(g + 1) * DS_C]
        cg = cm[:, g * DS_C:(g + 1) * DS_C]
        cb_mat = _bdot(cg, bg, NT)
        for pi in range(hpg // 2):
            pair = g * (hpg // 2) + pi
            h0, h1 = 2 * pair, 2 * pair + 1
            xp = xs[:, pair * LANES:(pair + 1) * LANES]
            hp = h_scr[pair]
            a0, a1 = acum_c[:, h0:h0 + 1], acum_c[:, h1:h1 + 1]

            def mmat(h, a_col):
                decay = jnp.exp(jnp.where(tril, a_col - acum_r[h:h + 1, :], -jnp.inf))
                return cb_mat * decay * dt_r[h:h + 1, :]

            yy = _bdot(mmat(h0, a0), jnp.where(lane_lo, xp, 0.0)) + _bdot(mmat(h1, a1), jnp.where(lane_lo, 0.0, xp))
            yy = yy + jnp.where(lane_lo, jnp.exp(a0), jnp.exp(a1)) * _bdot(cg, hp, NT)
            e0, e1 = a0[ln - 1:ln, :], a1[ln - 1:ln, :]
            wend = jnp.where(lane_lo, jnp.exp(e0 - a0) * dt_c[:, h0:h0 + 1], jnp.exp(e1 - a1) * dt_c[:, h1:h1 + 1])
            h_scr[pair] = jnp.where(row_lo, jnp.exp(e0), jnp.exp(e1)) * hp + _bdot(xp * wend, bg, TN)
            y_scr[:, pair * LANES:(pair + 1) * LANES] = yy + dsk_ref[:, pair * LANES:(pair + 1) * LANES] * xp
    gated = y_scr[...] * _silu(z_ref[...])
    gw = DI_C // NG_C
    for g in range(NG_C):
        sl = slice(g * gw, (g + 1) * gw)
        y_ref[:, sl] = _rms(gated[:, sl], nw_ref[:, sl])
    cs_ref[0] = carry_scr[...]

    @pl.when(ci == pl.num_programs(1) - 1)
    def _():
        hout_ref[0] = h_scr[...]


def _head_lanes(v, width):
    return jnp.repeat(v.astype(F32), width).reshape(1, -1)


def _ssd_prompt(z, xbc, d_col, d_row, conv_w, conv_b, dt_bias, a_log, d_skip, norm_w, batch, seq):
    m = z.shape[0]
    ln = CHUNK_C
    nc = seq // ln
    npair = NH_C // 2
    lane8 = lambda v: jnp.zeros((1, LANES), F32).at[0, :NH_C].set(v)
    row = lambda b, c: (b * nc + c, 0)
    const = lambda b, c: (0, 0)
    y, h1, cs = pl.pallas_call(
        _ssd_kernel, grid=(batch, nc),
        in_specs=[pl.BlockSpec((ln, DI_C), row), pl.BlockSpec((ln, CONV_DIM_C), row),
                  pl.BlockSpec((ln, LANES), row), pl.BlockSpec((SUBLANES, ln), lambda b, c: (0, b * nc + c)),
                  pl.BlockSpec((CONV_C, CONV_DIM_C), const), pl.BlockSpec((1, CONV_DIM_C), const),
                  pl.BlockSpec((1, LANES), const), pl.BlockSpec((SUBLANES, 1), const),
                  pl.BlockSpec((1, LANES), const), pl.BlockSpec((SUBLANES, 1), const),
                  pl.BlockSpec((1, DI_C), const), pl.BlockSpec((1, DI_C), const)],
        out_specs=[pl.BlockSpec((ln, DI_C), row),
                   pl.BlockSpec((1, npair, 2 * HD_C, DS_C), lambda b, c: (b, 0, 0, 0)),
                   pl.BlockSpec((1, SUBLANES, CONV_DIM_C), lambda b, c: (b, 0, 0))],
        out_shape=[jax.ShapeDtypeStruct((m, DI_C), F32),
                   jax.ShapeDtypeStruct((batch, npair, 2 * HD_C, DS_C), F32),
                   jax.ShapeDtypeStruct((batch, SUBLANES, CONV_DIM_C), F32)],
        scratch_shapes=[pltpu.VMEM((npair, 2 * HD_C, DS_C), F32), pltpu.VMEM((SUBLANES, CONV_DIM_C), F32),
                        pltpu.VMEM((ln + SUBLANES, CONV_DIM_C), F32), pltpu.VMEM((ln, DI_C), F32)],
        compiler_params=_params(("arbitrary", "arbitrary")), name="ssd_prompt",
    )(z, xbc, d_col, d_row, conv_w, conv_b.reshape(1, -1), lane8(dt_bias), dt_bias.reshape(NH_C, 1),
      lane8(a_log), a_log.reshape(NH_C, 1), _head_lanes(d_skip, HD_C), norm_w.reshape(1, DI_C))
    return y, h1.reshape(batch, NH_C, HD_C, DS_C), cs[:, SUBLANES - (CONV_C - 1):, :]


def _lane_bcast_col(row_vec):
    return jnp.transpose(jnp.broadcast_to(row_vec, (LANES, LANES)))


def _ssd_step_kernel(z_ref, x_ref, cs_ref, d_ref, cw_ref, cb_ref, db_ref, al_ref, dsk_ref, nw_ref, h_ref,
                     y_ref, hout_ref):
    y = cb_ref[...] + cw_ref[CONV_C - 1:CONV_C, :] * x_ref[0]
    for j in range(CONV_C - 1):
        y = y + cw_ref[j:j + 1, :] * cs_ref[0, j:j + 1, :]
    xbc = _silu(y)
    bw = NG_C * DS_C
    dt = _softplus(d_ref[0] + db_ref[...])
    ea = jnp.exp(dt * (-jnp.exp(al_ref[...])))
    row_lo = lax.broadcasted_iota(jnp.int32, (LANES, 1), 0) < HD_C
    lane_lo = lax.broadcasted_iota(jnp.int32, (1, LANES), 1) < HD_C
    hpg = NH_C // NG_C
    parts = []
    for pair in range(NH_C // 2):
        g = (2 * pair) // hpg
        h0, h1 = 2 * pair, 2 * pair + 1
        xp = xbc[:, pair * LANES:(pair + 1) * LANES]
        bg = xbc[:, DI_C + g * DS_C:DI_C + (g + 1) * DS_C]
        cg = xbc[:, DI_C + bw + g * DS_C:DI_C + bw + (g + 1) * DS_C]
        hp = h_ref[0, pair]
        x_col = _lane_bcast_col(xp)
        ea_col = jnp.where(row_lo, ea[:, h0:h0 + 1], ea[:, h1:h1 + 1])
        dt_col = jnp.where(row_lo, dt[:, h0:h0 + 1], dt[:, h1:h1 + 1])
        hout_ref[0, pair] = ea_col * hp + (dt_col * x_col) * bg
        y_inter = jnp.sum(hp * cg, axis=1, keepdims=True)
        y_col = ea_col * y_inter
        y_row = jnp.transpose(jnp.broadcast_to(y_col, (LANES, LANES)))[0:1, :]
        dt_row = jnp.where(lane_lo, dt[:, h0:h0 + 1], dt[:, h1:h1 + 1])
        cb = jnp.sum(cg * bg, axis=1, keepdims=True)
        parts.append(y_row + (cb * dt_row) * xp + dsk_ref[:, pair * LANES:(pair + 1) * LANES] * xp)
    gated = jnp.concatenate(parts, axis=1) * _silu(z_ref[0])
    gw = DI_C // NG_C
    for g in range(NG_C):
        sl = slice(g * gw, (g + 1) * gw)
        y_ref[0, :, sl] = _rms(gated[:, sl], nw_ref[:, sl])


def _ssd_step(z, xbc, d_col, conv_state, conv_w, conv_b, dt_bias, a_log, d_skip, norm_w, h0):
    ns = z.shape[0]
    npair = NH_C // 2
    lane8 = lambda v: jnp.zeros((1, LANES), F32).at[0, :NH_C].set(v)
    s3 = lambda s: (s, 0, 0)
    s4 = lambda s: (s, 0, 0, 0)
    const = lambda s: (0, 0)
    y, h1 = pl.pallas_call(
        _ssd_step_kernel, grid=(ns,),
        in_specs=[pl.BlockSpec((1, 1, DI_C), s3), pl.BlockSpec((1, 1, CONV_DIM_C), s3),
                  pl.BlockSpec((1, CONV_C - 1, CONV_DIM_C), s3), pl.BlockSpec((1, 1, LANES), s3),
                  pl.BlockSpec((CONV_C, CONV_DIM_C), const), pl.BlockSpec((1, CONV_DIM_C), const),
                  pl.BlockSpec((1, LANES), const), pl.BlockSpec((1, LANES), const),
                  pl.BlockSpec((1, DI_C), const), pl.BlockSpec((1, DI_C), const),
                  pl.BlockSpec((1, npair, 2 * HD_C, DS_C), s4)],
        out_specs=[pl.BlockSpec((1, 1, DI_C), s3), pl.BlockSpec((1, npair, 2 * HD_C, DS_C), s4)],
        out_shape=[jax.ShapeDtypeStruct((ns, 1, DI_C), F32), jax.ShapeDtypeStruct((ns, npair, 2 * HD_C, DS_C), F32)],
        compiler_params=_params(("arbitrary",)), name="ssd_step",
    )(z.reshape(ns, 1, DI_C), xbc.reshape(ns, 1, CONV_DIM_C), conv_state, d_col.reshape(ns, 1, LANES),
      conv_w, conv_b.reshape(1, -1), lane8(dt_bias), lane8(a_log), _head_lanes(d_skip, HD_C),
      norm_w.reshape(1, DI_C), h0.reshape(ns, npair, 2 * HD_C, DS_C))
    return y.reshape(ns, DI_C), h1.reshape(ns, NH_C, HD_C, DS_C)


def _head_mask(width):
    lane = lax.broadcasted_iota(jnp.int32, (1, width), 1)
    return [(lane >= h * HD_D) & (lane < (h + 1) * HD_D) for h in range(width // HD_D)]


def _swa_kernel(q_ref, kp_ref, kc_ref, vp_ref, vc_ref, acc_ref, m_ref, l_ref):
    t = pl.program_id(2)
    nk = SWA_KEYS
    q = q_ref[0] * (HD_D ** -0.5)
    kk = jnp.concatenate([kp_ref[0], kc_ref[0]], axis=0).astype(BF16)
    vv = jnp.concatenate([vp_ref[0], vc_ref[0]], axis=0).astype(BF16)
    qi = lax.broadcasted_iota(jnp.int32, (nk, 2 * nk), 0)
    kj = lax.broadcasted_iota(jnp.int32, (nk, 2 * nk), 1)
    dist = nk + qi - kj
    ok = (dist >= 0) & (dist <= nk) & ((t > 0) | (kj >= nk))
    acc = jnp.zeros((nk, SWA_W), F32)
    mm = jnp.zeros((nk, SWA_W), F32)
    ll = jnp.zeros((nk, SWA_W), F32)
    for hm in _head_mask(SWA_W):
        s = _dot(jnp.where(hm, q, 0.0).astype(BF16), kk, NT)
        s = jnp.where(ok, s, -jnp.inf)
        mx = jnp.max(s, axis=1, keepdims=True)
        p = jnp.exp(s - mx)
        acc = jnp.where(hm, _dot(p.astype(BF16), vv), acc)
        mm = jnp.where(hm, mx, mm)
        ll = jnp.where(hm, jnp.sum(p, axis=1, keepdims=True), ll)
    acc_ref[0] = acc
    m_ref[0] = mm
    l_ref[0] = ll


def _swa_prompt(qd, kd, vd, g, dil, batch, seq):
    ngrp = qd.shape[1] // SWA_W
    sub = seq // dil
    nt = sub // SWA_KEYS
    view = lambda a: a.reshape(batch, sub, dil * ngrp * SWA_W)
    cur = pl.BlockSpec((1, SWA_KEYS, SWA_W), lambda b, r, t: (b, t, r * ngrp + g))
    prev = pl.BlockSpec((1, SWA_KEYS, SWA_W), lambda b, r, t: (b, jnp.maximum(t - 1, 0), r * ngrp + g))
    ospec = pl.BlockSpec((1, SWA_KEYS, SWA_W), lambda b, r, t: (b, t, r))
    oshape = jax.ShapeDtypeStruct((batch, sub, dil * SWA_W), F32)
    outs = pl.pallas_call(
        _swa_kernel, grid=(batch, dil, nt),
        in_specs=[cur, prev, cur, prev, cur], out_specs=[ospec] * 3, out_shape=[oshape] * 3,
        compiler_params=_params(("arbitrary", "arbitrary", "arbitrary")), name=f"swa_prompt_{g}",
    )(view(qd), view(kd), view(kd), view(vd), view(vd))
    return [o.reshape(batch * seq, SWA_W) for o in outs]


def _swa_step_kernel(q_ref, kn_ref, vn_ref, c0_ref, c1_ref, c2_ref, acc_ref, m_ref, l_ref):
    r = lax.broadcasted_iota(jnp.int32, (SWA_W, SWA_W), 0) // HD_D
    c = lax.broadcasted_iota(jnp.int32, (SWA_W, SWA_W), 1) // HD_D
    ones_blk = (r == c).astype(F32)
    for g, c_ref in enumerate((c0_ref, c1_ref, c2_ref)):
        sl = slice(g * SWA_W, (g + 1) * SWA_W)
        q = q_ref[0, :, sl] * (HD_D ** -0.5)
        kc, vc = c_ref[0, :, :SWA_W], c_ref[0, :, SWA_W:]
        kn, vn = kn_ref[0, :, sl], vn_ref[0, :, sl]
        s = _dot(kc * q, ones_blk, NN, HIGHEST)
        sn = _dot(jnp.broadcast_to(kn * q, (SUBLANES, SWA_W)), ones_blk, NN, HIGHEST)[0:1, :]
        mx = jnp.maximum(jnp.max(s, axis=0, keepdims=True), sn)
        p = jnp.exp(s - mx)
        pn = jnp.exp(sn - mx)
        acc_ref[g, 0] = jnp.sum(p * vc, axis=0, keepdims=True) + pn * vn
        m_ref[g, 0] = mx
        l_ref[g, 0] = jnp.sum(p, axis=0, keepdims=True) + pn


def _swa_gather_kernel(*refs):
    n = len(refs) // 2
    for c_ref, o_ref in zip(refs[:n], refs[n:]):
        o_ref[...] = c_ref[...]


def _swa_gather(caches):
    ns = caches[0].shape[0]
    views, in_specs = [], []
    for (win, dil), cache in zip(SWA_GROUPS, caches):
        assert cache.shape[1] == win and win // dil == SWA_KEYS
        views.append(cache.reshape(ns, SWA_KEYS, dil, 2, HPG_D, HD_D))
        in_specs.append(pl.BlockSpec((1, SWA_KEYS, None, 2, HPG_D, HD_D), lambda s: (s, 0, 0, 0, 0, 0)))
    ospec = pl.BlockSpec((1, SWA_KEYS, 2, HPG_D, HD_D), lambda s: (s, 0, 0, 0, 0))
    oshape = jax.ShapeDtypeStruct((ns, SWA_KEYS, 2, HPG_D, HD_D), F32)
    outs = pl.pallas_call(
        _swa_gather_kernel, grid=(ns,), in_specs=in_specs, out_specs=[ospec] * len(views),
        out_shape=[oshape] * len(views), compiler_params=_params(("arbitrary",)), name="swa_gather",
    )(*views)
    return [o.reshape(ns, SWA_KEYS, 2 * SWA_W) for o in outs]


def _swa_step(qd, kd, vd, caches):
    ns = qd.shape[0]
    w3 = qd.shape[1]
    views = _swa_gather(caches)
    s3 = lambda s: (s, 0, 0)
    row = pl.BlockSpec((1, 1, w3), s3)
    ospec = pl.BlockSpec((len(SWA_GROUPS), 1, 1, SWA_W), lambda s: (0, s, 0, 0))
    oshape = jax.ShapeDtypeStruct((len(SWA_GROUPS), ns, 1, SWA_W), F32)
    outs = pl.pallas_call(
        _swa_step_kernel, grid=(ns,),
        in_specs=[row, row, row] + [pl.BlockSpec((1, SWA_KEYS, 2 * SWA_W), s3)] * 3,
        out_specs=[ospec] * 3, out_shape=[oshape] * 3,
        compiler_params=_params(("arbitrary",)), name="swa_step",
    )(qd.reshape(ns, 1, w3), kd.reshape(ns, 1, w3), vd.reshape(ns, 1, w3), *views)
    return [[o[g].reshape(ns, SWA_W) for o in outs] for g in range(len(SWA_GROUPS))]


def _swa_merge_kernel(*refs):
    ins, o_ref = refs[:-1], refs[-1]
    ng = len(ins) // 3
    mx = ins[1][...]
    for g in range(1, ng):
        mx = jnp.maximum(mx, ins[3 * g + 1][...])
    num = jnp.zeros_like(mx)
    den = jnp.zeros_like(mx)
    for g in range(ng):
        wgt = jnp.exp(ins[3 * g + 1][...] - mx)
        num = num + wgt * ins[3 * g][...]
        den = den + wgt * ins[3 * g + 2][...]
    o_ref[...] = num / den


def _swa_merge(parts, tm):
    flat = [a for grp in parts for a in grp]
    m = flat[0].shape[0]
    spec = pl.BlockSpec((tm, SWA_W), lambda i: (i, 0))
    return pl.pallas_call(
        _swa_merge_kernel, grid=(m // tm,), in_specs=[spec] * len(flat), out_specs=spec,
        out_shape=jax.ShapeDtypeStruct((m, SWA_W), F32),
        compiler_params=_params(("arbitrary",)), name="swa_merge",
    )(*flat)


def _split_cols(w, sizes):
    out, start = [], 0
    for s in sizes:
        out.append(w[:, start:start + s])
        start += s
    return out


def _trunk(x3, pos0, state, wts, tm):
    batch, seq, d = x3.shape
    m = batch * seq
    x = x3.reshape(m, d)
    pos = pos0 + jnp.arange(seq, dtype=jnp.int32)
    prompt = state is None
    if not prompt:
        pos = jnp.broadcast_to(pos, (m,))
    tabs_b, half_b = _rope_tables(pos, HD_B)
    tabs_d, half_d = _rope_tables(pos, HD_D)
    wa = NH_A * DK_A

    a_main, qb, kb, vb, g_col, g_row = _inproj(
        x, wts["norm_mix"][0], tabs_b, wts["l0_w"], (0, half_b, half_b, 0), wts["l0_small"], tm, "inproj_l0")
    if prompt:
        ha, c1, n1, m1 = _mlstm_prompt(a_main, g_col, g_row, wts["b_gates"], wts["norm_mlstm"], batch, seq)
        hb = _moba_prompt(qb, kb, vb, batch, seq)
    else:
        c0, n0, m0, k_cache, v_cache, page_table = state["even"]
        ha, c1, n1, m1 = _mlstm_step(a_main, g_col, wts["b_gates"], wts["norm_mlstm"], c0, n0, m0)
        ha = ha.reshape(m, wa)
        _, sel = _moba_select(qb, k_cache, page_table)
        hb = _moba_step(qb, kb, vb, k_cache, v_cache, page_table, sel[:, :NH_B, :MOBA_TOPK])
    m1 = m1[:, :, 0]
    x = _outproj(ha, hb, wts["l0_out_a"], wts["l0_out_b"], x, tm, "outproj_l0")
    ffn = wts["ffn"]
    if prompt:
        x, fbuf0 = _ffn_prompt(x, wts["norm_ffn"][0], *ffn[0], wts["norm_final"], False, batch, seq,
                               min(tm, 256), "ffn_l0")
    else:
        x, fbuf0 = _ffn_step(x, wts["norm_ffn"][0], *ffn[0], wts["norm_final"], False, state["ffn"][0], "ffn_step_l0")

    z, xbc, qd, kd, vd, d_col, d_row = _inproj(
        x, wts["norm_mix"][1], tabs_d, wts["l1_w"], (0, 0, half_d, half_d, 0), wts["l1_small"], tm, "inproj_l1")
    ssd_w = wts["ssd"]
    if prompt:
        yc, h1, conv1 = _ssd_prompt(z, xbc, d_col, d_row, *ssd_w, batch, seq)
        parts = [_swa_prompt(qd, kd, vd, g, dil, batch, seq) for g, (_, dil) in enumerate(SWA_GROUPS)]
    else:
        h0, conv0, caches = state["odd"]
        yc, h1 = _ssd_step(z, xbc, d_col, conv0, *ssd_w, h0)
        conv1 = jnp.concatenate([conv0[:, 1:], xbc[:, None, :]], axis=1)
        parts = _swa_step(qd, kd, vd, caches)
    od = _swa_merge(parts, tm)
    x = _outproj(yc, od, wts["l1_out_a"], wts["l1_out_b"], x, tm, "outproj_l1")
    if prompt:
        x, fbuf1 = _ffn_prompt(x, wts["norm_ffn"][1], *ffn[1], wts["norm_final"], True, batch, seq,
                               min(tm, 256), "ffn_l1")
    else:
        x, fbuf1 = _ffn_step(x, wts["norm_ffn"][1], *ffn[1], wts["norm_final"], True, state["ffn"][1], "ffn_step_l1")

    kr = kb.reshape(batch, seq, KVH_B, HD_B)
    vr = vb.reshape(batch, seq, KVH_B, HD_B)
    rows = []
    for g, (win, _) in enumerate(SWA_GROUPS):
        keep = min(win, seq) if prompt else seq
        kg = kd.reshape(batch, seq, -1)[:, seq - keep:, g * SWA_W:(g + 1) * SWA_W].reshape(batch, keep, HPG_D, HD_D)
        vg = vd.reshape(batch, seq, -1)[:, seq - keep:, g * SWA_W:(g + 1) * SWA_W].reshape(batch, keep, HPG_D, HD_D)
        rows.append(jnp.stack([kg, vg], axis=2))
    return (x.reshape(batch, seq, d), (c1, n1, m1, kr, vr), (h1, conv1, rows[0], rows[1], rows[2]),
            jnp.stack([fbuf0, fbuf1]))


def kernel(x_prompt, x_sample, state_l0_mlstm_c, state_l0_mlstm_n, state_l0_mlstm_m, cache_l0_moba_k, cache_l0_moba_v, state_l1_ssd_h, state_l1_ssd_conv, cache_l1_swa_kv0, cache_l1_swa_kv1, cache_l1_swa_kv2, state_ffn_conv, page_table, norm_mix, norm_ffn, norm_final, w_in_l0, b_gates_l0, norm_mlstm_l0, w_out_l0, w_in_l1, conv_w_l1, conv_b_l1, dt_bias_l1, a_log_l1, d_skip_l1, norm_ssd_l1, w_out_l1, ffn_up, ffn_conv_w, ffn_conv_b, ffn_down):
    wa = NH_A * DK_A
    qa, ka, va, oa, ia, fa, qb, kb, vb = _split_cols(
        w_in_l0, (wa, wa, wa, wa, NH_A, NH_A, NH_B * HD_B, KVH_B * HD_B, KVH_B * HD_B))
    wd = 3 * SWA_W
    z, xbc, dtr, qd, kd, vd = _split_cols(w_in_l1, (DI_C, CONV_DIM_C, NH_C, wd, wd, wd))
    bf = lambda a: a.astype(BF16)
    wts = {
        "norm_mix": norm_mix, "norm_ffn": norm_ffn, "norm_final": norm_final,
        "l0_w": [bf(jnp.concatenate([qa, ka, va, oa], axis=1)), bf(qb), bf(kb), bf(vb)],
        "l0_small": jnp.concatenate([ia, fa], axis=1),
        "b_gates": b_gates_l0, "norm_mlstm": norm_mlstm_l0,
        "l0_out_a": bf(w_out_l0[:wa]), "l0_out_b": bf(w_out_l0[wa:]),
        "l1_w": [bf(z), bf(xbc), bf(qd), bf(kd), bf(vd)],
        "l1_small": dtr,
        "ssd": (conv_w_l1, conv_b_l1, dt_bias_l1, a_log_l1, d_skip_l1, norm_ssd_l1),
        "l1_out_a": bf(w_out_l1[:DI_C]), "l1_out_b": bf(w_out_l1[DI_C:]),
        "ffn": [(bf(ffn_up[l]), ffn_conv_w[l], ffn_conv_b[l], bf(ffn_down[l])) for l in range(ffn_up.shape[0])],
    }
    n_seq, n_pages = page_table.shape
    past_len = n_pages * cache_l0_moba_k.shape[1]
    y_p, ev_p, od_p, ffn_p = _trunk(x_prompt, 0, None, wts, 512)
    state = {
        "even": (state_l0_mlstm_c, state_l0_mlstm_n, state_l0_mlstm_m, cache_l0_moba_k, cache_l0_moba_v, page_table),
        "odd": (state_l1_ssd_h, state_l1_ssd_conv, (cache_l1_swa_kv0, cache_l1_swa_kv1, cache_l1_swa_kv2)),
        "ffn": state_ffn_conv,
    }
    y_s, ev_s, od_s, ffn_s = _trunk(x_sample, past_len, state, wts, x_sample.shape[0] * x_sample.shape[1])
    c_p, n_p, m_p, k_p, v_p = ev_p
    c_s, n_s, m_s, k_s, v_s = ev_s
    h_p, conv_p, sw0_p, sw1_p, sw2_p = od_p
    h_s, conv_s, sw0_s, sw1_s, sw2_s = od_s
    return (y_p, y_s, c_p, c_s, n_p, n_s, m_p, m_s, k_p, k_s, v_p, v_s, h_p, h_s, conv_p, conv_s,
            sw0_p, sw0_s, sw1_p, sw1_s, sw2_p, sw2_s, ffn_p, ffn_s)
```

```python
import functools
import math

import jax
import jax.numpy as jnp
from jax import lax
from jax.experimental import pallas as pl
from jax.experimental.pallas import tpu as pltpu

F32 = jnp.float32
BF16 = jnp.bfloat16
HIGHEST = lax.Precision.HIGHEST

EPS = 1e-6
ROPE_THETA = 500000.0
ROPE_FRACTION = 4
LANES = 128
SUBLANES = 8
VMEM_LIMIT = 56 * 1024 * 1024

NH_A, DK_A, DV_A, CHUNK_A = 4, 128, 128, 128
NH_B, KVH_B, HD_B, MOBA_BLOCK, MOBA_TOPK = 4, 2, 128, 256, 3
NH_C, HD_C, NG_C, DS_C, CONV_C, CHUNK_C = 8, 64, 2, 128, 4, 128
DI_C = NH_C * HD_C
CONV_DIM_C = DI_C + 2 * NG_C * DS_C
SWA_GROUPS = ((128, 1), (512, 4), (2048, 16))
HPG_D, HD_D = 4, 64
SWA_W = HPG_D * HD_D
SWA_KEYS = 128
D_FF, FFN_CONV = 2816, 3
FFN_CW = 256

NN = (((1,), (0,)), ((), ()))
NT = (((1,), (1,)), ((), ()))
TN = (((0,), (0,)), ((), ()))


def _dot(a, b, dims=NN, precision=None):
    return lax.dot_general(a, b, dims, precision=precision, preferred_element_type=F32)


def _bdot(a, b, dims=NN):
    return _dot(a.astype(BF16), b.astype(BF16), dims)


def _params(sem, vmem=VMEM_LIMIT):
    return pltpu.CompilerParams(dimension_semantics=sem, vmem_limit_bytes=vmem)


def _rms(x, w):
    return x * lax.rsqrt(jnp.mean(x * x, axis=-1, keepdims=True) + EPS) * w


def _sigmoid(x):
    return 1.0 / (1.0 + jnp.exp(-x))


def _silu(x):
    return x * _sigmoid(x)


def _log_sigmoid(x):
    return jnp.minimum(x, 0.0) - jnp.log1p(jnp.exp(-jnp.abs(x)))


def _softplus(x):
    return jnp.maximum(x, 0.0) + jnp.log1p(jnp.exp(-jnp.abs(x)))


def _rope_lanes(y, cos, sa, sb, half):
    parts = []
    for g in range(y.shape[1] // LANES):
        yg = y[:, g * LANES:(g + 1) * LANES]
        parts.append(yg * cos + pltpu.roll(yg, LANES - half, axis=1) * sa + pltpu.roll(yg, half, axis=1) * sb)
    return parts


def _inproj_kernel(rope_half, x_ref, nw_ref, cos_ref, sa_ref, sb_ref, *refs):
    n = len(rope_half)
    w_refs, ws_ref, wst_ref = refs[:n], refs[n], refs[n + 1]
    out_refs, sc_ref, sr_ref = refs[n + 2:2 * n + 2], refs[2 * n + 2], refs[2 * n + 3]
    xn = _rms(x_ref[...], nw_ref[...])
    xb = xn.astype(BF16)
    for w_ref, o_ref, half in zip(w_refs, out_refs, rope_half):
        y = _dot(xb, w_ref[...])
        if half:
            parts = _rope_lanes(y, cos_ref[...], sa_ref[...], sb_ref[...], half)
            for g, p in enumerate(parts):
                o_ref[:, g * LANES:(g + 1) * LANES] = p
        else:
            o_ref[...] = y
    sc_ref[...] = _dot(xn, ws_ref[...], NN, HIGHEST)
    sr_ref[...] = _dot(wst_ref[...], xn, NT, HIGHEST)


def _inproj(x, nw, tabs, weights, rope_half, w_small, tm, name):
    m, d = x.shape
    cos, sa, sb = tabs
    trows = cos.shape[0]
    tb = trows // tm if trows >= tm else 1
    ws = jnp.zeros((d, LANES), F32).at[:, :w_small.shape[1]].set(w_small)
    wst = jnp.transpose(w_small)
    row = lambda i: (i, 0)
    const = lambda i: (0, 0)
    tab_spec = pl.BlockSpec((tm, LANES), lambda i: (i % tb, 0))
    in_specs = [pl.BlockSpec((tm, d), row), pl.BlockSpec((1, d), const), tab_spec, tab_spec, tab_spec]
    in_specs += [pl.BlockSpec(w.shape, const) for w in weights]
    in_specs += [pl.BlockSpec(ws.shape, const), pl.BlockSpec(wst.shape, const)]
    out_shape = [jax.ShapeDtypeStruct((m, w.shape[1]), F32) for w in weights]
    out_shape += [jax.ShapeDtypeStruct((m, LANES), F32), jax.ShapeDtypeStruct((SUBLANES, m), F32)]
    out_specs = [pl.BlockSpec((tm, w.shape[1]), row) for w in weights]
    out_specs += [pl.BlockSpec((tm, LANES), row), pl.BlockSpec((SUBLANES, tm), lambda i: (0, i))]
    return pl.pallas_call(
        functools.partial(_inproj_kernel, tuple(rope_half)),
        grid=(m // tm,), in_specs=in_specs, out_specs=out_specs, out_shape=out_shape,
        compiler_params=_params(("arbitrary",)), name=name,
    )(x, nw.reshape(1, d), cos, sa, sb, *weights, ws, wst)


def _rope_tables(pos, head_dim):
    rd = head_dim // ROPE_FRACTION
    half = rd // 2
    inv = ROPE_THETA ** (-jnp.arange(half, dtype=F32) / half)
    ang = pos.astype(F32)[:, None] * inv[None, :]
    cos, sin = jnp.cos(ang), jnp.sin(ang)
    ones = jnp.ones((pos.shape[0], head_dim - rd), F32)
    zeros = jnp.zeros((pos.shape[0], head_dim - rd), F32)
    zh = jnp.zeros_like(sin)
    reps = LANES // head_dim
    cos_t = jnp.tile(jnp.concatenate([cos, cos, ones], axis=1), (1, reps))
    sa_t = jnp.tile(jnp.concatenate([-sin, zh, zeros], axis=1), (1, reps))
    sb_t = jnp.tile(jnp.concatenate([zh, sin, zeros], axis=1), (1, reps))
    return (cos_t, sa_t, sb_t), half


def _mlstm_kernel(q_ref, k_ref, v_ref, o_ref, gc_ref, gr_ref, bc_ref, br_ref, nw_ref,
                  h_ref, cout_ref, nout_ref, mout_ref, c_scr, n_scr, m_scr):
    ci = pl.program_id(1)
    ln = CHUNK_A

    @pl.when(ci == 0)
    def _():
        c_scr[...] = jnp.zeros_like(c_scr)
        n_scr[...] = jnp.zeros_like(n_scr)
        m_scr[...] = jnp.zeros_like(m_scr)

    row = lax.broadcasted_iota(jnp.int32, (ln, ln), 0)
    col = lax.broadcasted_iota(jnp.int32, (ln, ln), 1)
    tril = row >= col
    gc = gc_ref[...] + bc_ref[...]
    gr = gr_ref[...] + br_ref[...]
    fcum_c = _dot(tril.astype(F32), _log_sigmoid(gc), NN, HIGHEST)
    fcum_r = _dot(_log_sigmoid(gr), (row <= col).astype(F32), NN, HIGHEST)
    for h in range(NH_A):
        sl = slice(h * DK_A, (h + 1) * DK_A)
        ic_r, ic_c = gr[h:h + 1, :], gc[:, h:h + 1]
        fc_c, fc_r = fcum_c[:, NH_A + h:NH_A + h + 1], fcum_r[NH_A + h:NH_A + h + 1, :]
        m = m_scr[h:h + 1, 0:1]
        c = c_scr[h]
        nrow = n_scr[h:h + 1, :]
        qh = q_ref[:, sl]
        kh = k_ref[:, sl] * (DK_A ** -0.5)
        vh = v_ref[:, sl]
        dmat = jnp.where(tril, fc_c - fc_r + ic_r, -jnp.inf)
        inter = fc_c + m
        mt = jnp.maximum(inter, jnp.max(dmat, axis=1, keepdims=True))
        wmat = _bdot(qh, kh, NT) * jnp.exp(dmat - mt)
        a_inter = jnp.exp(inter - mt)
        num = _bdot(wmat, vh) + a_inter * _bdot(qh, c)
        den = jnp.sum(wmat, axis=1, keepdims=True) + a_inter * jnp.sum(qh * nrow, axis=1, keepdims=True)
        hh = num / jnp.maximum(jnp.abs(den), jnp.exp(-mt))
        f_end = fc_c[ln - 1:ln, :]
        m_new = jnp.maximum(f_end + m, jnp.max(f_end - fc_r + ic_r, axis=1, keepdims=True))
        kw = kh * jnp.exp(f_end - fc_c + ic_c - m_new)
        decay = jnp.exp(f_end + m - m_new)
        c_scr[h] = decay * c + _bdot(kw, vh, TN)
        n_scr[h:h + 1, :] = decay * nrow + jnp.sum(kw, axis=0, keepdims=True)
        m_scr[h:h + 1, :] = jnp.broadcast_to(m_new, (1, LANES))
        ha = _sigmoid(o_ref[:, sl]) * hh
        h_ref[:, sl] = _rms(ha, nw_ref[:, sl])

    @pl.when(ci == pl.num_programs(1) - 1)
    def _():
        cout_ref[0] = c_scr[...]
        nout_ref[0] = n_scr[0:NH_A, :]
        mout_ref[0] = m_scr[0:NH_A, :]


def _mlstm_prompt(a_main, g_col, g_row, b_gates, norm_w, batch, seq):
    m = a_main.shape[0]
    ln = CHUNK_A
    nc = seq // ln
    w = NH_A * DK_A
    blk = lambda j: pl.BlockSpec((ln, w), lambda b, c, j=j: (b * nc + c, j))
    bc = jnp.zeros((1, LANES), F32).at[0, :2 * NH_A].set(b_gates)
    br = b_gates.reshape(2 * NH_A, 1)
    const = lambda b, c: (0, 0)
    return pl.pallas_call(
        _mlstm_kernel, grid=(batch, nc),
        in_specs=[blk(0), blk(1), blk(2), blk(3),
                  pl.BlockSpec((ln, LANES), lambda b, c: (b * nc + c, 0)),
                  pl.BlockSpec((SUBLANES, ln), lambda b, c: (0, b * nc + c)),
                  pl.BlockSpec((1, LANES), const), pl.BlockSpec((SUBLANES, 1), const),
                  pl.BlockSpec((1, w), const)],
        out_specs=[pl.BlockSpec((ln, w), lambda b, c: (b * nc + c, 0)),
                   pl.BlockSpec((1, NH_A, DK_A, DV_A), lambda b, c: (b, 0, 0, 0)),
                   pl.BlockSpec((1, NH_A, DK_A), lambda b, c: (b, 0, 0)),
                   pl.BlockSpec((1, NH_A, LANES), lambda b, c: (b, 0, 0))],
        out_shape=[jax.ShapeDtypeStruct((m, w), F32),
                   jax.ShapeDtypeStruct((batch, NH_A, DK_A, DV_A), F32),
                   jax.ShapeDtypeStruct((batch, NH_A, DK_A), F32),
                   jax.ShapeDtypeStruct((batch, NH_A, LANES), F32)],
        scratch_shapes=[pltpu.VMEM((NH_A, DK_A, DV_A), F32), pltpu.VMEM((SUBLANES, LANES), F32),
                        pltpu.VMEM((SUBLANES, LANES), F32)],
        compiler_params=_params(("arbitrary", "arbitrary")), name="mlstm_prompt",
    )(a_main, a_main, a_main, a_main, g_col, g_row, bc, br, norm_w.reshape(1, w))


def _mlstm_step_kernel(a_ref, qc_ref, kc_ref, g_ref, b_ref, m_ref, c_ref, n_ref, nw_ref,
                       h_ref, cout_ref, nout_ref, mout_ref):
    g = g_ref[0] + b_ref[...]
    lf_all = _log_sigmoid(g)
    w = NH_A * DK_A
    scale = DK_A ** -0.5
    for h in range(NH_A):
        sl = slice(h * DK_A, (h + 1) * DK_A)
        q_row = a_ref[0, :, sl]
        k_row = a_ref[0, :, w + h * DK_A:w + (h + 1) * DK_A] * scale
        v_row = a_ref[0, :, 2 * w + h * DV_A:2 * w + (h + 1) * DV_A]
        o_row = a_ref[0, :, 3 * w + h * DV_A:3 * w + (h + 1) * DV_A]
        q_col = qc_ref[0, sl, :]
        k_col = kc_ref[0, sl, :] * scale
        c = c_ref[0, h]
        n_row = n_ref[0, h:h + 1, :]
        m = m_ref[0, :, h:h + 1]
        ic = g[:, h:h + 1]
        lf = lf_all[:, NH_A + h:NH_A + h + 1]
        inter = lf + m
        mt = jnp.maximum(inter, ic)
        wm = jnp.sum(q_row * k_row, axis=1, keepdims=True) * jnp.exp(ic - mt)
        a_inter = jnp.exp(inter - mt)
        num = wm * v_row + a_inter * jnp.sum(q_col * c, axis=0, keepdims=True)
        den = wm + a_inter * jnp.sum(q_row * n_row, axis=1, keepdims=True)
        hh = num / jnp.maximum(jnp.abs(den), jnp.exp(-mt))
        m_new = jnp.maximum(inter, ic)
        wgt = jnp.exp(ic - m_new)
        decay = jnp.exp(inter - m_new)
        cout_ref[0, h] = decay * c + (wgt * k_col) * v_row
        nout_ref[0, h:h + 1, :] = decay * n_row + wgt * k_row
        mout_ref[0, h:h + 1, :] = jnp.broadcast_to(m_new, (1, LANES))
        ha = _sigmoid(o_row) * hh
        h_ref[0, :, sl] = _rms(ha, nw_ref[:, sl])


def _mlstm_step(a_main, g_col, b_gates, norm_w, c0, n0, m0):
    ns = a_main.shape[0]
    w = NH_A * DK_A
    a3 = a_main.reshape(ns, 1, 4 * w)
    q_col = a_main[:, :w].reshape(ns, w, 1)
    k_col = a_main[:, w:2 * w].reshape(ns, w, 1)
    bc = jnp.zeros((1, LANES), F32).at[0, :2 * NH_A].set(b_gates)
    s3 = lambda s: (s, 0, 0)
    s4 = lambda s: (s, 0, 0, 0)
    const = lambda s: (0, 0)
    return pl.pallas_call(
        _mlstm_step_kernel, grid=(ns,),
        in_specs=[pl.BlockSpec((1, 1, 4 * w), s3), pl.BlockSpec((1, w, 1), s3), pl.BlockSpec((1, w, 1), s3),
                  pl.BlockSpec((1, 1, LANES), s3), pl.BlockSpec((1, LANES), const),
                  pl.BlockSpec((1, 1, NH_A), s3), pl.BlockSpec((1, NH_A, DK_A, DV_A), s4),
                  pl.BlockSpec((1, NH_A, DK_A), s3), pl.BlockSpec((1, w), const)],
        out_specs=[pl.BlockSpec((1, 1, w), s3), pl.BlockSpec((1, NH_A, DK_A, DV_A), s4),
                   pl.BlockSpec((1, NH_A, DK_A), s3), pl.BlockSpec((1, NH_A, LANES), s3)],
        out_shape=[jax.ShapeDtypeStruct((ns, 1, w), F32), jax.ShapeDtypeStruct((ns, NH_A, DK_A, DV_A), F32),
                   jax.ShapeDtypeStruct((ns, NH_A, DK_A), F32), jax.ShapeDtypeStruct((ns, NH_A, LANES), F32)],
        compiler_params=_params(("arbitrary",)), name="mlstm_step",
    )(a3, q_col, k_col, g_col.reshape(ns, 1, LANES), bc, m0.reshape(ns, 1, NH_A), c0, n0, norm_w.reshape(1, w))


def _top_blocks(bs, topk):
    blk_id = lax.broadcasted_iota(jnp.int32, bs.shape, 0).astype(F32)
    sel = jnp.zeros(bs.shape, F32)
    for _ in range(topk):
        mx = jnp.max(bs, axis=0, keepdims=True)
        idx = jnp.min(jnp.where(bs == mx, blk_id, float(bs.shape[0])), axis=0, keepdims=True)
        pick = (blk_id == idx) & (mx > -jnp.inf)
        sel = jnp.where(pick, 1.0, sel)
        bs = jnp.where(pick, -jnp.inf, bs)
    return sel


def _moba_kernel(n_full, q_ref, k_ref, v_ref, o_ref, kmean_scr, kb_scr, vt_scr, sel_scr, m_scr, l_scr, acc_scr):
    qi = pl.program_id(2)
    blk = MOBA_BLOCK
    grp = NH_B // KVH_B

    @pl.when(qi == 0)
    def _():
        kmean_scr[...] = jnp.zeros_like(kmean_scr)
        for n in range(n_full):
            kblk = k_ref[n * blk:(n + 1) * blk, :]
            kmean_scr[n:n + 1, :] = jnp.mean(kblk, axis=0, keepdims=True)
            kb_scr[n] = kblk.astype(BF16)
            vt_scr[n] = jnp.transpose(v_ref[n * blk:(n + 1) * blk, :]).astype(BF16)

    q = jnp.concatenate([q_ref[:, g * HD_B:(g + 1) * HD_B] for g in range(grp)], axis=0)
    rows = grp * blk
    bs = _dot(kmean_scr[...], q, NT, HIGHEST)
    bs = jnp.where(lax.broadcasted_iota(jnp.int32, bs.shape, 0) < qi, bs, -jnp.inf)
    sel_scr[...] = _top_blocks(bs, min(MOBA_TOPK, n_full))
    qs = (q * (HD_B ** -0.5)).astype(BF16)

    key = lax.broadcasted_iota(jnp.int32, (blk, rows), 0)
    qpos = lax.broadcasted_iota(jnp.int32, (blk, rows), 1) % blk
    causal = key <= qpos

    def attend(first, own_group, m_old, l_old, acc_old):
        scores = []
        for g in range(MOBA_GROUP):
            b = first + g
            ok = sel_scr[pl.ds(b, 1), :] > 0.0
            if own_group:
                ok = ((b < qi) & ok) | ((b == qi) & causal)
            scores.append(jnp.where(ok, _dot(kb_scr[b], qs, NT), -jnp.inf))
        m_new = m_old
        for s in scores:
            m_new = jnp.maximum(m_new, jnp.max(s, axis=0, keepdims=True))
        alpha = jnp.exp(m_old - m_new)
        l_new, acc_new = alpha * l_old, alpha * acc_old
        for g, s in enumerate(scores):
            p = jnp.exp(s - m_new)
            l_new = l_new + jnp.sum(p, axis=0, keepdims=True)
            acc_new = acc_new + _dot(vt_scr[first + g], p.astype(BF16))
        m_scr[...] = m_new
        l_scr[...] = l_new
        acc_scr[...] = acc_new

    own_first = pl.multiple_of((qi // MOBA_GROUP) * MOBA_GROUP, MOBA_GROUP)
    attend(own_first, True, jnp.full((1, rows), -jnp.inf, F32), jnp.zeros((1, rows), F32),
           jnp.zeros((HD_B, rows), F32))

    def body(i, carry):
        attend(pl.multiple_of(i * MOBA_GROUP, MOBA_GROUP), False, m_scr[...], l_scr[...], acc_scr[...])
        return carry

    lax.fori_loop(0, qi // MOBA_GROUP, body, 0)
    out = acc_scr[...] / l_scr[...]
    for g in range(grp):
        o_ref[:, g * HD_B:(g + 1) * HD_B] = jnp.transpose(out[:, g * blk:(g + 1) * blk])


MOBA_GROUP = 4


def _moba_prompt(qb, kb, vb, batch, seq):
    m = qb.shape[0]
    blk = MOBA_BLOCK
    nq = seq // blk
    assert seq % (blk * MOBA_GROUP) == 0
    grp = NH_B // KVH_B
    return pl.pallas_call(
        functools.partial(_moba_kernel, seq // blk), grid=(batch, KVH_B, nq),
        in_specs=[pl.BlockSpec((blk, grp * HD_B), lambda b, j, i: (b * nq + i, j)),
                  pl.BlockSpec((seq, HD_B), lambda b, j, i: (b, j)),
                  pl.BlockSpec((seq, HD_B), lambda b, j, i: (b, j))],
        out_specs=pl.BlockSpec((blk, grp * HD_B), lambda b, j, i: (b * nq + i, j)),
        out_shape=jax.ShapeDtypeStruct((m, NH_B * HD_B), F32),
        scratch_shapes=[pltpu.VMEM((-(-nq // SUBLANES) * SUBLANES, HD_B), F32),
                        pltpu.VMEM((nq, blk, HD_B), BF16), pltpu.VMEM((nq, HD_B, blk), BF16),
                        pltpu.VMEM((-(-nq // SUBLANES) * SUBLANES, grp * blk), F32),
                        pltpu.VMEM((1, grp * blk), F32), pltpu.VMEM((1, grp * blk), F32),
                        pltpu.VMEM((HD_B, grp * blk), F32)],
        compiler_params=_params(("arbitrary", "arbitrary", "arbitrary")), name="moba_prompt",
    )(qb, kb, vb)


MOBA_PAGES_PER_STEP = 32


def _moba_select_kernel(n_steps, pt_ref, q_ref, *refs):
    pages, (kmean_ref, sel_ref) = refs[:-2], refs[-2:]
    j = pl.program_id(1)
    page = pages[0].shape[1] // KVH_B

    def page_sum(p):
        heads = [jnp.sum(p[0, pl.ds(kv, page, stride=KVH_B), :], axis=0, keepdims=True) for kv in range(KVH_B)]
        return jnp.concatenate(heads, axis=1)

    sums = [page_sum(p) for p in pages]
    nblk = len(pages) // 2
    per_blk = [(sums[2 * i] + sums[2 * i + 1]) * (1.0 / MOBA_BLOCK) for i in range(nblk)]
    kmean_ref[0, pl.ds(pl.multiple_of(j * nblk, SUBLANES), nblk), :] = jnp.concatenate(per_blk, axis=0)

    @pl.when(j == n_steps - 1)
    def _():
        grp = NH_B // KVH_B
        nb = kmean_ref.shape[1]
        rowi = lax.broadcasted_iota(jnp.int32, (nb, 1), 0).astype(F32)
        orow = lax.broadcasted_iota(jnp.int32, (SUBLANES, LANES), 0)
        olane = lax.broadcasted_iota(jnp.int32, (SUBLANES, LANES), 1)
        out = jnp.zeros((SUBLANES, LANES), F32)
        for h in range(NH_B):
            kv = h // grp
            km = kmean_ref[0, :, kv * HD_B:(kv + 1) * HD_B]
            qh = q_ref[0, :, h * HD_B:(h + 1) * HD_B]
            bs = jnp.sum(km * qh, axis=1, keepdims=True)
            for r in range(MOBA_TOPK):
                mx = jnp.max(bs, axis=0, keepdims=True)
                idx = jnp.min(jnp.where(bs == mx, rowi, float(nb)), axis=0, keepdims=True)
                out = jnp.where((orow == h) & (olane == r), idx, out)
                bs = jnp.where(rowi == idx, -jnp.inf, bs)
        sel_ref[0] = out.astype(jnp.int32)


def _moba_select(qb, k_cache, page_table):
    ns, n_pages = page_table.shape
    page = k_cache.shape[1]
    pps = min(MOBA_PAGES_PER_STEP, n_pages)
    assert 2 * page == MOBA_BLOCK and n_pages % pps == 0 and pps % (2 * SUBLANES) == 0
    n_steps = n_pages // pps
    kw = KVH_B * HD_B
    kc = k_cache.reshape(k_cache.shape[0], page * KVH_B, HD_B)
    page_spec = lambda i: pl.BlockSpec(
        (1, page * KVH_B, HD_B), lambda s, j, pt, i=i: (pt[s * n_pages + j * pps + i], 0, 0))
    nb = n_pages * page // MOBA_BLOCK
    grid_spec = pltpu.PrefetchScalarGridSpec(
        num_scalar_prefetch=1, grid=(ns, n_steps),
        in_specs=[pl.BlockSpec((1, 1, NH_B * HD_B), lambda s, j, pt: (s, 0, 0))]
        + [page_spec(i) for i in range(pps)],
        out_specs=[pl.BlockSpec((1, nb, kw), lambda s, j, pt: (s, 0, 0)),
                   pl.BlockSpec((1, SUBLANES, LANES), lambda s, j, pt: (s, 0, 0))])
    return pl.pallas_call(
        functools.partial(_moba_select_kernel, n_steps), grid_spec=grid_spec,
        out_shape=[jax.ShapeDtypeStruct((ns, nb, kw), F32), jax.ShapeDtypeStruct((ns, SUBLANES, LANES), jnp.int32)],
        compiler_params=_params(("arbitrary", "arbitrary")), name="moba_select",
    )(page_table.reshape(-1), qb.reshape(ns, 1, NH_B * HD_B), *([kc] * pps))


def _moba_step_kernel(sel_ref, pt_ref, q_ref, kn_ref, vn_ref, *refs):
    pages, o_ref, (m_scr, l_scr, acc_scr) = refs[:4 * NH_B], refs[4 * NH_B], refs[4 * NH_B + 1:]
    r = pl.program_id(1)
    scale = HD_B ** -0.5
    grp = NH_B // KVH_B
    page = pages[0].shape[1] // KVH_B
    for h in range(NH_B):
        kv = h // grp
        row = slice(h, h + 1)
        q = q_ref[0, :, h * HD_B:(h + 1) * HD_B]
        ka_ref, kb_ref, va_ref, vb_ref = pages[4 * h:4 * h + 4]

        @pl.when(r == 0)
        def _():
            s0 = jnp.sum(q * kn_ref[0, :, kv * HD_B:(kv + 1) * HD_B], axis=1, keepdims=True) * scale
            m_scr[row, :] = jnp.broadcast_to(s0, (1, LANES))
            l_scr[row, :] = jnp.ones((1, LANES), F32)
            acc_scr[row, :] = vn_ref[0, :, kv * HD_B:(kv + 1) * HD_B]

        rows = pl.ds(kv, page, stride=KVH_B)
        kk = jnp.concatenate([ka_ref[0, rows, :], kb_ref[0, rows, :]], axis=0)
        vv = jnp.concatenate([va_ref[0, rows, :], vb_ref[0, rows, :]], axis=0)
        s = jnp.sum(kk * q, axis=1, keepdims=True) * scale
        m_old = m_scr[row, 0:1]
        m_new = jnp.maximum(m_old, jnp.max(s, axis=0, keepdims=True))
        alpha = jnp.exp(m_old - m_new)
        p = jnp.exp(s - m_new)
        m_scr[row, :] = jnp.broadcast_to(m_new, (1, LANES))
        l_scr[row, :] = alpha * l_scr[row, :] + jnp.sum(p, axis=0, keepdims=True)
        acc_scr[row, :] = alpha * acc_scr[row, :] + jnp.sum(p * vv, axis=0, keepdims=True)

        @pl.when(r == pl.num_programs(1) - 1)
        def _():
            o_ref[0, :, h * HD_B:(h + 1) * HD_B] = acc_scr[row, :] / l_scr[row, :]


def _moba_step(qb, k_new, v_new, k_cache, v_cache, page_table, sel):
    ns, n_pages = page_table.shape
    page = k_cache.shape[1]
    kw = KVH_B * HD_B
    kc = k_cache.reshape(k_cache.shape[0], page * KVH_B, HD_B)
    vc = v_cache.reshape(v_cache.shape[0], page * KVH_B, HD_B)

    def page_spec(h, half):
        def imap(s, r, sel_r, pt_r):
            blk = sel_r[(s * NH_B + h) * MOBA_TOPK + r]
            return (pt_r[s * n_pages + 2 * blk + half], 0, 0)
        return pl.BlockSpec((1, page * KVH_B, HD_B), imap)

    seq_row = lambda w: pl.BlockSpec((1, 1, w), lambda s, r, a, b: (s, 0, 0))
    page_specs, page_args = [], []
    for h in range(NH_B):
        page_specs += [page_spec(h, 0), page_spec(h, 1), page_spec(h, 0), page_spec(h, 1)]
        page_args += [kc, kc, vc, vc]
    grid_spec = pltpu.PrefetchScalarGridSpec(
        num_scalar_prefetch=2, grid=(ns, MOBA_TOPK),
        in_specs=[seq_row(NH_B * HD_B), seq_row(kw), seq_row(kw)] + page_specs,
        out_specs=seq_row(NH_B * HD_B),
        scratch_shapes=[pltpu.VMEM((SUBLANES, LANES), F32), pltpu.VMEM((SUBLANES, LANES), F32),
                        pltpu.VMEM((SUBLANES, HD_B), F32)])
    out = pl.pallas_call(
        _moba_step_kernel, grid_spec=grid_spec,
        out_shape=jax.ShapeDtypeStruct((ns, 1, NH_B * HD_B), F32),
        compiler_params=_params(("arbitrary", "arbitrary")), name="moba_step",
    )(sel.reshape(-1), page_table.reshape(-1), qb.reshape(ns, 1, NH_B * HD_B),
      k_new.reshape(ns, 1, kw), v_new.reshape(ns, 1, kw), *page_args)
    return out.reshape(ns, NH_B * HD_B)


def _outproj_kernel(a_ref, b_ref, wa_ref, wb_ref, x_ref, o_ref):
    o_ref[...] = x_ref[...] + _bdot(a_ref[...], wa_ref[...]) + _bdot(b_ref[...], wb_ref[...])


def _outproj(a, b, wa, wb, x, tm, name):
    m, d = x.shape
    row = lambda i: (i, 0)
    const = lambda i: (0, 0)
    return pl.pallas_call(
        _outproj_kernel, grid=(m // tm,),
        in_specs=[pl.BlockSpec((tm, a.shape[1]), row), pl.BlockSpec((tm, b.shape[1]), row),
                  pl.BlockSpec(wa.shape, const), pl.BlockSpec(wb.shape, const), pl.BlockSpec((tm, d), row)],
        out_specs=pl.BlockSpec((tm, d), row), out_shape=jax.ShapeDtypeStruct((m, d), F32),
        compiler_params=_params(("arbitrary",)), name=name,
    )(a, b, wa, wb, x)


def _ffn_kernel(final, x_ref, nw_ref, wup_ref, cw_ref, cb_ref, wdn_ref, nf_ref, o_ref, cs_ref,
                carry_scr, extg_scr, extv_scr):
    tm = x_ref.shape[0]

    @pl.when(pl.program_id(1) == 0)
    def _():
        carry_scr[...] = jnp.zeros_like(carry_scr)

    x = x_ref[...]
    xb = _rms(x, nw_ref[...]).astype(BF16)
    acc = x
    pad = SUBLANES

    def conv(ext_scr, col0):
        cols = slice(col0, col0 + FFN_CW)
        u = _dot(xb, wup_ref[:, cols])
        ext_scr[0:pad, :] = carry_scr[:, cols]
        ext_scr[pad:pad + tm, :] = u
        carry_scr[:, cols] = ext_scr[tm:tm + pad, :]
        y = cb_ref[:, cols] + cw_ref[FFN_CONV - 1:FFN_CONV, cols] * u
        for j in range(FFN_CONV - 1):
            off = pad - (FFN_CONV - 1) + j
            y = y + cw_ref[j:j + 1, cols] * ext_scr[off:off + tm, :]
        return y

    for c in range(D_FF // FFN_CW):
        gate = conv(extg_scr, c * FFN_CW)
        val = conv(extv_scr, D_FF + c * FFN_CW)
        act = (_silu(gate) * val).astype(BF16)
        acc = acc + _dot(act, wdn_ref[c * FFN_CW:(c + 1) * FFN_CW, :])
    if final:
        acc = _rms(acc, nf_ref[...])
    o_ref[...] = acc
    cs_ref[0] = carry_scr[...]


def _ffn_prompt(x, nw, w_up, conv_w, conv_b, w_down, nf, final, batch, seq, tm, name):
    m, d = x.shape
    nt = seq // tm
    row = lambda b, i: (b * nt + i, 0)
    const = lambda b, i: (0, 0)
    out, cs = pl.pallas_call(
        functools.partial(_ffn_kernel, final), grid=(batch, nt),
        in_specs=[pl.BlockSpec((tm, d), row), pl.BlockSpec((1, d), const), pl.BlockSpec(w_up.shape, const),
                  pl.BlockSpec(conv_w.shape, const), pl.BlockSpec((1, 2 * D_FF), const),
                  pl.BlockSpec(w_down.shape, const), pl.BlockSpec((1, d), const)],
        out_specs=[pl.BlockSpec((tm, d), row), pl.BlockSpec((1, SUBLANES, 2 * D_FF), lambda b, i: (b, 0, 0))],
        out_shape=[jax.ShapeDtypeStruct((m, d), F32), jax.ShapeDtypeStruct((batch, SUBLANES, 2 * D_FF), F32)],
        scratch_shapes=[pltpu.VMEM((SUBLANES, 2 * D_FF), F32), pltpu.VMEM((tm + SUBLANES, FFN_CW), F32),
                        pltpu.VMEM((tm + SUBLANES, FFN_CW), F32)],
        compiler_params=_params(("arbitrary", "arbitrary")), name=name,
    )(x, nw.reshape(1, d), w_up, conv_w, conv_b.reshape(1, -1), w_down, nf.reshape(1, d))
    return out, cs[:, SUBLANES - (FFN_CONV - 1):, :]


def _ffn_step_kernel(final, x_ref, nw_ref, wg_ref, wv_ref, g0_ref, g1_ref, v0_ref, v1_ref,
                     cwg_ref, cwv_ref, cbg_ref, cbv_ref, wdn_ref, nf_ref, o_ref, ug_ref, uv_ref, acc_scr):
    c = pl.program_id(0)

    @pl.when(c == 0)
    def _():
        acc_scr[...] = x_ref[...]

    xb = _rms(x_ref[...], nw_ref[...]).astype(BF16)
    ug = _dot(xb, wg_ref[...])
    uv = _dot(xb, wv_ref[...])
    ug_ref[...] = ug
    uv_ref[...] = uv
    gate = cbg_ref[...] + cwg_ref[0:1, :] * g0_ref[...] + cwg_ref[1:2, :] * g1_ref[...] + cwg_ref[2:3, :] * ug
    val = cbv_ref[...] + cwv_ref[0:1, :] * v0_ref[...] + cwv_ref[1:2, :] * v1_ref[...] + cwv_ref[2:3, :] * uv
    acc_scr[...] += _dot((_silu(gate) * val).astype(BF16), wdn_ref[...])

    @pl.when(c == pl.num_programs(0) - 1)
    def _():
        acc = acc_scr[...]
        o_ref[...] = _rms(acc, nf_ref[...]) if final else acc


def _ffn_step(x, nw, w_up, conv_w, conv_b, w_down, nf, final, buf, name):
    ns, d = x.shape
    cw = FFN_CW
    nch = D_FF // cw
    ntot = 2 * D_FF // cw
    bufw = buf.reshape(ns, 2 * 2 * D_FF)
    cb = conv_b.reshape(1, -1)
    const = lambda c: (0, 0)
    gcol = lambda c: (0, c)
    vcol = lambda c: (0, nch + c)
    out, u_g, u_v = pl.pallas_call(
        functools.partial(_ffn_step_kernel, final), grid=(nch,),
        in_specs=[pl.BlockSpec((ns, d), const), pl.BlockSpec((1, d), const),
                  pl.BlockSpec((d, cw), gcol), pl.BlockSpec((d, cw), vcol),
                  pl.BlockSpec((ns, cw), gcol), pl.BlockSpec((ns, cw), lambda c: (0, ntot + c)),
                  pl.BlockSpec((ns, cw), vcol), pl.BlockSpec((ns, cw), lambda c: (0, ntot + nch + c)),
                  pl.BlockSpec((FFN_CONV, cw), gcol), pl.BlockSpec((FFN_CONV, cw), vcol),
                  pl.BlockSpec((1, cw), gcol), pl.BlockSpec((1, cw), vcol),
                  pl.BlockSpec((cw, d), lambda c: (c, 0)), pl.BlockSpec((1, d), const)],
        out_specs=[pl.BlockSpec((ns, d), const), pl.BlockSpec((ns, cw), gcol), pl.BlockSpec((ns, cw), gcol)],
        out_shape=[jax.ShapeDtypeStruct((ns, d), F32), jax.ShapeDtypeStruct((ns, D_FF), F32),
                   jax.ShapeDtypeStruct((ns, D_FF), F32)],
        scratch_shapes=[pltpu.VMEM((ns, d), F32)],
        compiler_params=_params(("arbitrary",)), name=name,
    )(x, nw.reshape(1, d), w_up, w_up, bufw, bufw, bufw, bufw, conv_w, conv_w, cb, cb, w_down, nf.reshape(1, d))
    u = jnp.concatenate([u_g, u_v], axis=1)
    return out, jnp.stack([buf[:, 1], u], axis=1)


def _ssd_kernel(z_ref, xbc_ref, dc_ref, dr_ref, cw_ref, cb_ref, dbc_ref, dbr_ref, alc_ref, alr_ref,
                dsk_ref, nw_ref, y_ref, hout_ref, cs_ref, h_scr, carry_scr, ext_scr, y_scr):
    ci = pl.program_id(1)
    ln = CHUNK_C
    pad = SUBLANES

    @pl.when(ci == 0)
    def _():
        h_scr[...] = jnp.zeros_like(h_scr)
        carry_scr[...] = jnp.zeros_like(carry_scr)

    raw = xbc_ref[...]
    ext_scr[0:pad, :] = carry_scr[...]
    ext_scr[pad:pad + ln, :] = raw
    carry_scr[...] = ext_scr[ln:ln + pad, :]
    y = cb_ref[...] + cw_ref[CONV_C - 1:CONV_C, :] * raw
    for j in range(CONV_C - 1):
        off = pad - (CONV_C - 1) + j
        y = y + cw_ref[j:j + 1, :] * ext_scr[off:off + ln, :]
    xbc = _silu(y)
    bw = NG_C * DS_C
    xs, bm, cm = xbc[:, :DI_C], xbc[:, DI_C:DI_C + bw], xbc[:, DI_C + bw:]

    row = lax.broadcasted_iota(jnp.int32, (ln, ln), 0)
    col = lax.broadcasted_iota(jnp.int32, (ln, ln), 1)
    tril = row >= col
    lane_lo = lax.broadcasted_iota(jnp.int32, (ln, LANES), 1) < HD_C
    row_lo = lax.broadcasted_iota(jnp.int32, (2 * HD_C, DS_C), 0) < HD_C
    dt_c = _softplus(dc_ref[...] + dbc_ref[...])
    dt_r = _softplus(dr_ref[...] + dbr_ref[...])
    ac_c = dt_c * (-jnp.exp(alc_ref[...]))
    ac_r = dt_r * (-jnp.exp(alr_ref[...]))
    acum_c = _dot(tril.astype(F32), ac_c, NN, HIGHEST)
    acum_r = _dot(ac_r, (row <= col).astype(F32), NN, HIGHEST)
    hpg = NH_C // NG_C
    for g in range(NG_C):
        bg = bm[:, g * DS_C:(g + 1) * DS_C]
        cg = cm[:, g * DS_C:(g + 1) * DS_C]
        cb_mat = _bdot(cg, bg, NT)
        for pi in range(hpg // 2):
            pair = g * (hpg // 2) + pi
            h0, h1 = 2 * pair, 2 * pair + 1
            xp = xs[:, pair * LANES:(pair + 1) * LANES]
            hp = h_scr[pair]
            a0, a1 = acum_c[:, h0:h0 + 1], acum_c[:, h1:h1 + 1]

            def mmat(h, a_col):
                decay = jnp.exp(jnp.where(tril, a_col - acum_r[h:h + 1, :], -jnp.inf))
                return cb_mat * decay * dt_r[h:h + 1, :]

            yy = _bdot(mmat(h0, a0), jnp.where(lane_lo, xp, 0.0)) + _bdot(mmat(h1, a1), jnp.where(lane_lo, 0.0, xp))
            yy = yy + jnp.where(lane_lo, jnp.exp(a0), jnp.exp(a1)) * _bdot(cg, hp, NT)
            e0, e1 = a0[ln - 1:ln, :], a1[ln - 1:ln, :]
            wend = jnp.where(lane_lo, jnp.exp(e0 - a0) * dt_c[:, h0:h0 + 1], jnp.exp(e1 - a1) * dt_c[:, h1:h1 + 1])
            h_scr[pair] = jnp.where(row_lo, jnp.exp(e0), jnp.exp(e1)) * hp + _bdot(xp * wend, bg, TN)
            y_scr[:, pair * LANES:(pair + 1) * LANES] = yy + dsk_ref[:, pair * LANES:(pair + 1) * LANES] * xp
    gated = y_scr[...] * _silu(z_ref[...])
    gw = DI_C // NG_C
    for g in range(NG_C):
        sl = slice(g * gw, (g + 1) * gw)
        y_ref[:, sl] = _rms(gated[:, sl], nw_ref[:, sl])
    cs_ref[0] = carry_scr[...]

    @pl.when(ci == pl.num_programs(1) - 1)
    def _():
        hout_ref[0] = h_scr[...]


def _head_lanes(v, width):
    return jnp.repeat(v.astype(F32), width).reshape(1, -1)


def _ssd_prompt(z, xbc, d_col, d_row, conv_w, conv_b, dt_bias, a_log, d_skip, norm_w, batch, seq):
    m = z.shape[0]
    ln = CHUNK_C
    nc = seq // ln
    npair = NH_C // 2
    lane8 = lambda v: jnp.zeros((1, LANES), F32).at[0, :NH_C].set(v)
    row = lambda b, c: (b * nc + c, 0)
    const = lambda b, c: (0, 0)
    y, h1, cs = pl.pallas_call(
        _ssd_kernel, grid=(batch, nc),
        in_specs=[pl.BlockSpec((ln, DI_C), row), pl.BlockSpec((ln, CONV_DIM_C), row),
                  pl.BlockSpec((ln, LANES), row), pl.BlockSpec((SUBLANES, ln), lambda b, c: (0, b * nc + c)),
                  pl.BlockSpec((CONV_C, CONV_DIM_C), const), pl.BlockSpec((1, CONV_DIM_C), const),
                  pl.BlockSpec((1, LANES), const), pl.BlockSpec((SUBLANES, 1), const),
                  pl.BlockSpec((1, LANES), const), pl.BlockSpec((SUBLANES, 1), const),
                  pl.BlockSpec((1, DI_C), const), pl.BlockSpec((1, DI_C), const)],
        out_specs=[pl.BlockSpec((ln, DI_C), row),
                   pl.BlockSpec((1, npair, 2 * HD_C, DS_C), lambda b, c: (b, 0, 0, 0)),
                   pl.BlockSpec((1, SUBLANES, CONV_DIM_C), lambda b, c: (b, 0, 0))],
        out_shape=[jax.ShapeDtypeStruct((m, DI_C), F32),
                   jax.ShapeDtypeStruct((batch, npair, 2 * HD_C, DS_C), F32),
                   jax.ShapeDtypeStruct((batch, SUBLANES, CONV_DIM_C), F32)],
        scratch_shapes=[pltpu.VMEM((npair, 2 * HD_C, DS_C), F32), pltpu.VMEM((SUBLANES, CONV_DIM_C), F32),
                        pltpu.VMEM((ln + SUBLANES, CONV_DIM_C), F32), pltpu.VMEM((ln, DI_C), F32)],
        compiler_params=_params(("arbitrary", "arbitrary")), name="ssd_prompt",
    )(z, xbc, d_col, d_row, conv_w, conv_b.reshape(1, -1), lane8(dt_bias), dt_bias.reshape(NH_C, 1),
      lane8(a_log), a_log.reshape(NH_C, 1), _head_lanes(d_skip, HD_C), norm_w.reshape(1, DI_C))
    return y, h1.reshape(batch, NH_C, HD_C, DS_C), cs[:, SUBLANES - (CONV_C - 1):, :]


def _lane_bcast_col(row_vec):
    return jnp.transpose(jnp.broadcast_to(row_vec, (LANES, LANES)))


def _ssd_step_kernel(z_ref, x_ref, cs_ref, d_ref, cw_ref, cb_ref, db_ref, al_ref, dsk_ref, nw_ref, h_ref,
                     y_ref, hout_ref):
    y = cb_ref[...] + cw_ref[CONV_C - 1:CONV_C, :] * x_ref[0]
    for j in range(CONV_C - 1):
        y = y + cw_ref[j:j + 1, :] * cs_ref[0, j:j + 1, :]
    xbc = _silu(y)
    bw = NG_C * DS_C
    dt = _softplus(d_ref[0] + db_ref[...])
    ea = jnp.exp(dt * (-jnp.exp(al_ref[...])))
    row_lo = lax.broadcasted_iota(jnp.int32, (LANES, 1), 0) < HD_C
    lane_lo = lax.broadcasted_iota(jnp.int32, (1, LANES), 1) < HD_C
    hpg = NH_C // NG_C
    parts = []
    for pair in range(NH_C // 2):
        g = (2 * pair) // hpg
        h0, h1 = 2 * pair, 2 * pair + 1
        xp = xbc[:, pair * LANES:(pair + 1) * LANES]
        bg = xbc[:, DI_C + g * DS_C:DI_C + (g + 1) * DS_C]
        cg = xbc[:, DI_C + bw + g * DS_C:DI_C + bw + (g + 1) * DS_C]
        hp = h_ref[0, pair]
        x_col = _lane_bcast_col(xp)
        ea_col = jnp.where(row_lo, ea[:, h0:h0 + 1], ea[:, h1:h1 + 1])
        dt_col = jnp.where(row_lo, dt[:, h0:h0 + 1], dt[:, h1:h1 + 1])
        hout_ref[0, pair] = ea_col * hp + (dt_col * x_col) * bg
        y_inter = jnp.sum(hp * cg, axis=1, keepdims=True)
        y_col = ea_col * y_inter
        y_row = jnp.transpose(jnp.broadcast_to(y_col, (LANES, LANES)))[0:1, :]
        dt_row = jnp.where(lane_lo, dt[:, h0:h0 + 1], dt[:, h1:h1 + 1])
        cb = jnp.sum(cg * bg, axis=1, keepdims=True)
        parts.append(y_row + (cb * dt_row) * xp + dsk_ref[:, pair * LANES:(pair + 1) * LANES] * xp)
    gated = jnp.concatenate(parts, axis=1) * _silu(z_ref[0])
    gw = DI_C // NG_C
    for g in range(NG_C):
        sl = slice(g * gw, (g + 1) * gw)
        y_ref[0, :, sl] = _rms(gated[:, sl], nw_ref[:, sl])


def _ssd_step(z, xbc, d_col, conv_state, conv_w, conv_b, dt_bias, a_log, d_skip, norm_w, h0):
    ns = z.shape[0]
    npair = NH_C // 2
    lane8 = lambda v: jnp.zeros((1, LANES), F32).at[0, :NH_C].set(v)
    s3 = lambda s: (s, 0, 0)
    s4 = lambda s: (s, 0, 0, 0)
    const = lambda s: (0, 0)
    y, h1 = pl.pallas_call(
        _ssd_step_kernel, grid=(ns,),
        in_specs=[pl.BlockSpec((1, 1, DI_C), s3), pl.BlockSpec((1, 1, CONV_DIM_C), s3),
                  pl.BlockSpec((1, CONV_C - 1, CONV_DIM_C), s3), pl.BlockSpec((1, 1, LANES), s3),
                  pl.BlockSpec((CONV_C, CONV_DIM_C), const), pl.BlockSpec((1, CONV_DIM_C), const),
                  pl.BlockSpec((1, LANES), const), pl.BlockSpec((1, LANES), const),
                  pl.BlockSpec((1, DI_C), const), pl.BlockSpec((1, DI_C), const),
                  pl.BlockSpec((1, npair, 2 * HD_C, DS_C), s4)],
        out_specs=[pl.BlockSpec((1, 1, DI_C), s3), pl.BlockSpec((1, npair, 2 * HD_C, DS_C), s4)],
        out_shape=[jax.ShapeDtypeStruct((ns, 1, DI_C), F32), jax.ShapeDtypeStruct((ns, npair, 2 * HD_C, DS_C), F32)],
        compiler_params=_params(("arbitrary",)), name="ssd_step",
    )(z.reshape(ns, 1, DI_C), xbc.reshape(ns, 1, CONV_DIM_C), conv_state, d_col.reshape(ns, 1, LANES),
      conv_w, conv_b.reshape(1, -1), lane8(dt_bias), lane8(a_log), _head_lanes(d_skip, HD_C),
      norm_w.reshape(1, DI_C), h0.reshape(ns, npair, 2 * HD_C, DS_C))
    return y.reshape(ns, DI_C), h1.reshape(ns, NH_C, HD_C, DS_C)


def _head_mask(width):
    lane = lax.broadcasted_iota(jnp.int32, (1, width), 1)
    return [(lane >= h * HD_D) & (lane < (h + 1) * HD_D) for h in range(width // HD_D)]


SWA_TILE = 2048


def _swa_kernel(dil, q_ref, kt_ref, kc_ref, vt_ref, vc_ref, acc_ref, m_ref, l_ref, kbuf, vbuf):
    t = pl.program_id(1)
    nk = SWA_KEYS
    tail = nk * dil
    tile = q_ref.shape[0]
    kbuf[0:tail, :] = kt_ref[...]
    kbuf[tail:tail + tile, :] = kc_ref[...]
    vbuf[0:tail, :] = vt_ref[...]
    vbuf[tail:tail + tile, :] = vc_ref[...]
    qi = lax.broadcasted_iota(jnp.int32, (nk, 2 * nk), 0)
    kj = lax.broadcasted_iota(jnp.int32, (nk, 2 * nk), 1)
    dist = nk + qi - kj
    band = (dist >= 0) & (dist <= nk)
    heads = _head_mask(LANES)

    def sub_tile(j, carry):
        first = pl.multiple_of(j * tail, tail)
        ok = band & ((t > 0) | (j > 0) | (kj >= nk))
        q_win = q_ref.at[pl.ds(first, tail), :]
        k_win = kbuf.at[pl.ds(first, 2 * tail), :]
        v_win = vbuf.at[pl.ds(first, 2 * tail), :]
        outs = [o.at[pl.ds(first, tail), :] for o in (acc_ref, m_ref, l_ref)]
        for r in range(dil):
            q = q_win[pl.ds(r, nk, stride=dil), :] * (HD_D ** -0.5)
            kk = k_win[pl.ds(r, 2 * nk, stride=dil), :].astype(BF16)
            vv = v_win[pl.ds(r, 2 * nk, stride=dil), :].astype(BF16)
            acc = jnp.zeros((nk, LANES), F32)
            mm = jnp.zeros((nk, LANES), F32)
            ll = jnp.zeros((nk, LANES), F32)
            for hm in heads:
                s = _dot(jnp.where(hm, q, 0.0).astype(BF16), kk, NT)
                s = jnp.where(ok, s, -jnp.inf)
                mx = jnp.max(s, axis=1, keepdims=True)
                p = jnp.exp(s - mx)
                acc = jnp.where(hm, _dot(p.astype(BF16), vv), acc)
                mm = jnp.where(hm, mx, mm)
                ll = jnp.where(hm, jnp.sum(p, axis=1, keepdims=True), ll)
            for o, val in zip(outs, (acc, mm, ll)):
                o[pl.ds(r, nk, stride=dil), :] = val
        return carry

    lax.fori_loop(0, tile // tail, sub_tile, 0)


def _swa_prompt(qd, kd, vd, g, dil, batch, seq):
    m = qd.shape[0]
    tile = min(SWA_TILE, seq)
    tail = SWA_KEYS * dil
    assert seq % tile == 0 and tile % tail == 0
    nt = seq // tile
    per = tile // tail
    halves = SWA_W // LANES
    cur = pl.BlockSpec((tile, LANES), lambda b, t, f: (b * nt + t, g * halves + f))
    prev = pl.BlockSpec((tail, LANES), lambda b, t, f: (jnp.maximum((b * nt + t) * per - 1, 0), g * halves + f))
    ospec = pl.BlockSpec((tile, LANES), lambda b, t, f: (b * nt + t, f))
    oshape = jax.ShapeDtypeStruct((m, SWA_W), F32)
    return pl.pallas_call(
        functools.partial(_swa_kernel, dil), grid=(batch, nt, halves),
        in_specs=[cur, prev, cur, prev, cur], out_specs=[ospec] * 3, out_shape=[oshape] * 3,
        scratch_shapes=[pltpu.VMEM((tail + tile, LANES), F32), pltpu.VMEM((tail + tile, LANES), F32)],
        compiler_params=_params(("arbitrary", "arbitrary", "arbitrary")), name=f"swa_prompt_{g}",
    )(qd, kd, kd, vd, vd)


def _swa_step_kernel(q_ref, kn_ref, vn_ref, c0_ref, c1_ref, c2_ref, o_ref):
    scale = HD_D ** -0.5
    ngrp = len(SWA_GROUPS)
    stats = []
    for g, (c_ref, (win, dil)) in enumerate(zip((c0_ref, c1_ref, c2_ref), SWA_GROUPS)):
        on_stride = lax.broadcasted_iota(jnp.int32, (1, win), 1) % dil == 0
        per_head = []
        for h in range(HPG_D):
            sl = slice((g * HPG_D + h) * HD_D, (g * HPG_D + h + 1) * HD_D)
            q = q_ref[0, sl, :] * scale
            s = jnp.sum(c_ref[0, 0, h] * q, axis=0, keepdims=True)
            s = jnp.where(on_stride, s, -jnp.inf)
            sn = jnp.sum(kn_ref[0, sl, :] * q, axis=0, keepdims=True)
            mx = jnp.maximum(jnp.max(s, axis=1, keepdims=True), sn)
            p = jnp.exp(s - mx)
            pn = jnp.exp(sn - mx)
            den = jnp.sum(p, axis=1, keepdims=True) + pn
            acc = jnp.sum(c_ref[0, 1, h] * p, axis=1, keepdims=True) + pn * vn_ref[0, sl, :]
            per_head.append((acc, mx, den))
        stats.append(per_head)
    for h in range(HPG_D):
        mx = functools.reduce(jnp.maximum, [stats[g][h][1] for g in range(ngrp)])
        wgt = [jnp.exp(stats[g][h][1] - mx) for g in range(ngrp)]
        num = functools.reduce(jnp.add, [wgt[g] * stats[g][h][0] for g in range(ngrp)])
        den = functools.reduce(jnp.add, [wgt[g] * stats[g][h][2] for g in range(ngrp)])
        o_ref[0, h * HD_D:(h + 1) * HD_D, :] = num / den


def _swa_step(qd, kd, vd, caches):
    ns = qd.shape[0]
    w3 = qd.shape[1]
    views, cspecs = [], []
    for (win, dil), cache in zip(SWA_GROUPS, caches):
        assert cache.shape[1] == win and win // dil == SWA_KEYS
        views.append(jnp.transpose(cache, (0, 2, 3, 4, 1)))
        cspecs.append(pl.BlockSpec((1, 2, HPG_D, HD_D, win), lambda s: (s, 0, 0, 0, 0)))
    s3 = lambda s: (s, 0, 0)
    col = pl.BlockSpec((1, w3, 1), s3)
    out = pl.pallas_call(
        _swa_step_kernel, grid=(ns,), in_specs=[col, col, col] + cspecs,
        out_specs=pl.BlockSpec((1, SWA_W, 1), s3), out_shape=jax.ShapeDtypeStruct((ns, SWA_W, 1), F32),
        compiler_params=_params(("arbitrary",)), name="swa_step",
    )(qd.reshape(ns, w3, 1), kd.reshape(ns, w3, 1), vd.reshape(ns, w3, 1), *views)
    return out.reshape(ns, SWA_W)


def _swa_merge_kernel(*refs):
    ins, o_ref = refs[:-1], refs[-1]
    ng = len(ins) // 3
    mx = ins[1][...]
    for g in range(1, ng):
        mx = jnp.maximum(mx, ins[3 * g + 1][...])
    num = jnp.zeros_like(mx)
    den = jnp.zeros_like(mx)
    for g in range(ng):
        wgt = jnp.exp(ins[3 * g + 1][...] - mx)
        num = num + wgt * ins[3 * g][...]
        den = den + wgt * ins[3 * g + 2][...]
    o_ref[...] = num / den


def _swa_merge(parts, tm):
    flat = [a for grp in parts for a in grp]
    m = flat[0].shape[0]
    spec = pl.BlockSpec((tm, SWA_W), lambda i: (i, 0))
    return pl.pallas_call(
        _swa_merge_kernel, grid=(m // tm,), in_specs=[spec] * len(flat), out_specs=spec,
        out_shape=jax.ShapeDtypeStruct((m, SWA_W), F32),
        compiler_params=_params(("arbitrary",)), name="swa_merge",
    )(*flat)


def _split_cols(w, sizes):
    out, start = [], 0
    for s in sizes:
        out.append(w[:, start:start + s])
        start += s
    return out


def _trunk(x3, pos0, state, wts, tm):
    batch, seq, d = x3.shape
    m = batch * seq
    x = x3.reshape(m, d)
    pos = pos0 + jnp.arange(seq, dtype=jnp.int32)
    prompt = state is None
    if not prompt:
        pos = jnp.broadcast_to(pos, (m,))
    tabs_b, half_b = _rope_tables(pos, HD_B)
    tabs_d, half_d = _rope_tables(pos, HD_D)
    wa = NH_A * DK_A

    a_main, qb, kb, vb, g_col, g_row = _inproj(
        x, wts["norm_mix"][0], tabs_b, wts["l0_w"], (0, half_b, half_b, 0), wts["l0_small"], tm, "inproj_l0")
    if prompt:
        ha, c1, n1, m1 = _mlstm_prompt(a_main, g_col, g_row, wts["b_gates"], wts["norm_mlstm"], batch, seq)
        hb = _moba_prompt(qb, kb, vb, batch, seq)
    else:
        c0, n0, m0, k_cache, v_cache, page_table = state["even"]
        ha, c1, n1, m1 = _mlstm_step(a_main, g_col, wts["b_gates"], wts["norm_mlstm"], c0, n0, m0)
        ha = ha.reshape(m, wa)
        _, sel = _moba_select(qb, k_cache, page_table)
        hb = _moba_step(qb, kb, vb, k_cache, v_cache, page_table, sel[:, :NH_B, :MOBA_TOPK])
    m1 = m1[:, :, 0]
    x = _outproj(ha, hb, wts["l0_out_a"], wts["l0_out_b"], x, tm, "outproj_l0")
    ffn = wts["ffn"]
    if prompt:
        x, fbuf0 = _ffn_prompt(x, wts["norm_ffn"][0], *ffn[0], wts["norm_final"], False, batch, seq,
                               min(tm, 256), "ffn_l0")
    else:
        x, fbuf0 = _ffn_step(x, wts["norm_ffn"][0], *ffn[0], wts["norm_final"], False, state["ffn"][0], "ffn_step_l0")

    z, xbc, qd, kd, vd, d_col, d_row = _inproj(
        x, wts["norm_mix"][1], tabs_d, wts["l1_w"], (0, 0, half_d, half_d, 0), wts["l1_small"], tm, "inproj_l1")
    ssd_w = wts["ssd"]
    if prompt:
        yc, h1, conv1 = _ssd_prompt(z, xbc, d_col, d_row, *ssd_w, batch, seq)
        parts = [_swa_prompt(qd, kd, vd, g, dil, batch, seq) for g, (_, dil) in enumerate(SWA_GROUPS)]
        od = _swa_merge(parts, tm)
    else:
        h0, conv0, caches = state["odd"]
        yc, h1 = _ssd_step(z, xbc, d_col, conv0, *ssd_w, h0)
        conv1 = jnp.concatenate([conv0[:, 1:], xbc[:, None, :]], axis=1)
        od = _swa_step(qd, kd, vd, caches)
    x = _outproj(yc, od, wts["l1_out_a"], wts["l1_out_b"], x, tm, "outproj_l1")
    if prompt:
        x, fbuf1 = _ffn_prompt(x, wts["norm_ffn"][1], *ffn[1], wts["norm_final"], True, batch, seq,
                               min(tm, 256), "ffn_l1")
    else:
        x, fbuf1 = _ffn_step(x, wts["norm_ffn"][1], *ffn[1], wts["norm_final"], True, state["ffn"][1], "ffn_step_l1")

    kr = kb.reshape(batch, seq, KVH_B, HD_B)
    vr = vb.reshape(batch, seq, KVH_B, HD_B)
    rows = []
    for g, (win, _) in enumerate(SWA_GROUPS):
        keep = min(win, seq) if prompt else seq
        kg = kd.reshape(batch, seq, -1)[:, seq - keep:, g * SWA_W:(g + 1) * SWA_W].reshape(batch, keep, HPG_D, HD_D)
        vg = vd.reshape(batch, seq, -1)[:, seq - keep:, g * SWA_W:(g + 1) * SWA_W].reshape(batch, keep, HPG_D, HD_D)
        rows.append(jnp.stack([kg, vg], axis=2))
    return (x.reshape(batch, seq, d), (c1, n1, m1, kr, vr), (h1, conv1, rows[0], rows[1], rows[2]),
            jnp.stack([fbuf0, fbuf1]))


def kernel(x_prompt, x_sample, state_l0_mlstm_c, state_l0_mlstm_n, state_l0_mlstm_m, cache_l0_moba_k, cache_l0_moba_v, state_l1_ssd_h, state_l1_ssd_conv, cache_l1_swa_kv0, cache_l1_swa_kv1, cache_l1_swa_kv2, state_ffn_conv, page_table, norm_mix, norm_ffn, norm_final, w_in_l0, b_gates_l0, norm_mlstm_l0, w_out_l0, w_in_l1, conv_w_l1, conv_b_l1, dt_bias_l1, a_log_l1, d_skip_l1, norm_ssd_l1, w_out_l1, ffn_up, ffn_conv_w, ffn_conv_b, ffn_down):
    wa = NH_A * DK_A
    qa, ka, va, oa, ia, fa, qb, kb, vb = _split_cols(
        w_in_l0, (wa, wa, wa, wa, NH_A, NH_A, NH_B * HD_B, KVH_B * HD_B, KVH_B * HD_B))
    wd = 3 * SWA_W
    z, xbc, dtr, qd, kd, vd = _split_cols(w_in_l1, (DI_C, CONV_DIM_C, NH_C, wd, wd, wd))
    bf = lambda a: a.astype(BF16)
    wts = {
        "norm_mix": norm_mix, "norm_ffn": norm_ffn, "norm_final": norm_final,
        "l0_w": [bf(jnp.concatenate([qa, ka, va, oa], axis=1)), bf(qb), bf(kb), bf(vb)],
        "l0_small": jnp.concatenate([ia, fa], axis=1),
        "b_gates": b_gates_l0, "norm_mlstm": norm_mlstm_l0,
        "l0_out_a": bf(w_out_l0[:wa]), "l0_out_b": bf(w_out_l0[wa:]),
        "l1_w": [bf(z), bf(xbc), bf(qd), bf(kd), bf(vd)],
        "l1_small": dtr,
        "ssd": (conv_w_l1, conv_b_l1, dt_bias_l1, a_log_l1, d_skip_l1, norm_ssd_l1),
        "l1_out_a": bf(w_out_l1[:DI_C]), "l1_out_b": bf(w_out_l1[DI_C:]),
        "ffn": [(bf(ffn_up[l]), ffn_conv_w[l], ffn_conv_b[l], bf(ffn_down[l])) for l in range(ffn_up.shape[0])],
    }
    n_seq, n_pages = page_table.shape
    past_len = n_pages * cache_l0_moba_k.shape[1]
    y_p, ev_p, od_p, ffn_p = _trunk(x_prompt, 0, None, wts, 512)
    state = {
        "even": (state_l0_mlstm_c, state_l0_mlstm_n, state_l0_mlstm_m, cache_l0_moba_k, cache_l0_moba_v, page_table),
        "odd": (state_l1_ssd_h, state_l1_ssd_conv, (cache_l1_swa_kv0, cache_l1_swa_kv1, cache_l1_swa_kv2)),
        "ffn": state_ffn_conv,
    }
    y_s, ev_s, od_s, ffn_s = _trunk(x_sample, past_len, state, wts, x_sample.shape[0] * x_sample.shape[1])
    c_p, n_p, m_p, k_p, v_p = ev_p
    c_s, n_s, m_s, k_s, v_s = ev_s
    h_p, conv_p, sw0_p, sw1_p, sw2_p = od_p
    h_s, conv_s, sw0_s, sw1_s, sw2_s = od_s
    return (y_p, y_s, c_p, c_s, n_p, n_s, m_p, m_s, k_p, k_s, v_p, v_s, h_p, h_s, conv_p, conv_s,
            sw0_p, sw0_s, sw1_p, sw1_s, sw2_p, sw2_s, ffn_p, ffn_s)
```

```python
import functools
import math

import jax
import jax.numpy as jnp
from jax import lax
from jax.experimental import pallas as pl
from jax.experimental.pallas import tpu as pltpu

F32 = jnp.float32
BF16 = jnp.bfloat16
HIGHEST = lax.Precision.HIGHEST

EPS = 1e-6
ROPE_THETA = 500000.0
ROPE_FRACTION = 4
LANES = 128
SUBLANES = 8
VMEM_LIMIT = 56 * 1024 * 1024

NH_A, DK_A, DV_A, CHUNK_A = 4, 128, 128, 128
NH_B, KVH_B, HD_B, MOBA_BLOCK, MOBA_TOPK = 4, 2, 128, 256, 3
NH_C, HD_C, NG_C, DS_C, CONV_C, CHUNK_C = 8, 64, 2, 128, 4, 128
DI_C = NH_C * HD_C
CONV_DIM_C = DI_C + 2 * NG_C * DS_C
SWA_GROUPS = ((128, 1), (512, 4), (2048, 16))
HPG_D, HD_D = 4, 64
SWA_W = HPG_D * HD_D
SWA_KEYS = 128
D_FF, FFN_CONV = 2816, 3
FFN_CW = 256

NN = (((1,), (0,)), ((), ()))
NT = (((1,), (1,)), ((), ()))
TN = (((0,), (0,)), ((), ()))


def _dot(a, b, dims=NN, precision=None):
    return lax.dot_general(a, b, dims, precision=precision, preferred_element_type=F32)


def _bdot(a, b, dims=NN):
    return _dot(a.astype(BF16), b.astype(BF16), dims)


def _params(sem, vmem=VMEM_LIMIT):
    return pltpu.CompilerParams(dimension_semantics=sem, vmem_limit_bytes=vmem)


def _rms(x, w):
    return x * lax.rsqrt(jnp.mean(x * x, axis=-1, keepdims=True) + EPS) * w


def _sigmoid(x):
    return 1.0 / (1.0 + jnp.exp(-x))


def _silu(x):
    return x * _sigmoid(x)


def _log_sigmoid(x):
    return jnp.minimum(x, 0.0) - jnp.log1p(jnp.exp(-jnp.abs(x)))


def _softplus(x):
    return jnp.maximum(x, 0.0) + jnp.log1p(jnp.exp(-jnp.abs(x)))


def _rope_lanes(y, cos, sa, sb, half):
    parts = []
    for g in range(y.shape[1] // LANES):
        yg = y[:, g * LANES:(g + 1) * LANES]
        parts.append(yg * cos + pltpu.roll(yg, LANES - half, axis=1) * sa + pltpu.roll(yg, half, axis=1) * sb)
    return parts


def _inproj_kernel(rope_half, x_ref, nw_ref, cos_ref, sa_ref, sb_ref, *refs):
    n = len(rope_half)
    w_refs, out_refs = refs[:n], refs[n:]
    xb = _rms(x_ref[...], nw_ref[...]).astype(BF16)
    for w_ref, o_ref, half in zip(w_refs, out_refs, rope_half):
        y = _dot(xb, w_ref[...])
        if half:
            parts = _rope_lanes(y, cos_ref[...], sa_ref[...], sb_ref[...], half)
            for g, p in enumerate(parts):
                o_ref[:, g * LANES:(g + 1) * LANES] = p
        else:
            o_ref[...] = y


def _inproj(x, nw, tabs, weights, rope_half, tm, name):
    m, d = x.shape
    cos, sa, sb = tabs
    trows = cos.shape[0]
    tb = trows // tm if trows >= tm else 1
    row = lambda i: (i, 0)
    const = lambda i: (0, 0)
    tab_spec = pl.BlockSpec((tm, LANES), lambda i: (i % tb, 0))
    in_specs = [pl.BlockSpec((tm, d), row), pl.BlockSpec((1, d), const), tab_spec, tab_spec, tab_spec]
    in_specs += [pl.BlockSpec(w.shape, const) for w in weights]
    out_shape = [jax.ShapeDtypeStruct((m, w.shape[1]), F32) for w in weights]
    out_specs = [pl.BlockSpec((tm, w.shape[1]), row) for w in weights]
    return pl.pallas_call(
        functools.partial(_inproj_kernel, tuple(rope_half)),
        grid=(m // tm,), in_specs=in_specs, out_specs=out_specs, out_shape=out_shape,
        compiler_params=_params(("arbitrary",)), name=name,
    )(x, nw.reshape(1, d), cos, sa, sb, *weights)


def _pad_lanes(w):
    return jnp.zeros((w.shape[0], LANES), w.dtype).at[:, :w.shape[1]].set(w)


def _rope_tables(pos, head_dim):
    rd = head_dim // ROPE_FRACTION
    half = rd // 2
    inv = ROPE_THETA ** (-jnp.arange(half, dtype=F32) / half)
    ang = pos.astype(F32)[:, None] * inv[None, :]
    cos, sin = jnp.cos(ang), jnp.sin(ang)
    ones = jnp.ones((pos.shape[0], head_dim - rd), F32)
    zeros = jnp.zeros((pos.shape[0], head_dim - rd), F32)
    zh = jnp.zeros_like(sin)
    reps = LANES // head_dim
    cos_t = jnp.tile(jnp.concatenate([cos, cos, ones], axis=1), (1, reps))
    sa_t = jnp.tile(jnp.concatenate([-sin, zh, zeros], axis=1), (1, reps))
    sb_t = jnp.tile(jnp.concatenate([zh, sin, zeros], axis=1), (1, reps))
    return (cos_t, sa_t, sb_t), half


def _mlstm_kernel(q_ref, k_ref, v_ref, o_ref, gc_ref, bc_ref, nw_ref,
                  h_ref, cout_ref, nout_ref, mout_ref, c_scr, n_scr, m_scr):
    ci = pl.program_id(1)
    ln = CHUNK_A

    @pl.when(ci == 0)
    def _():
        c_scr[...] = jnp.zeros_like(c_scr)
        n_scr[...] = jnp.zeros_like(n_scr)
        m_scr[...] = jnp.zeros_like(m_scr)

    row = lax.broadcasted_iota(jnp.int32, (ln, ln), 0)
    col = lax.broadcasted_iota(jnp.int32, (ln, ln), 1)
    tril = row >= col
    gc = gc_ref[...] + bc_ref[...]
    gr = jnp.transpose(gc)[0:2 * NH_A, :]
    fcum_c = _dot(tril.astype(F32), _log_sigmoid(gc), NN, HIGHEST)
    fcum_r = _dot(_log_sigmoid(gr), (row <= col).astype(F32), NN, HIGHEST)
    for h in range(NH_A):
        sl = slice(h * DK_A, (h + 1) * DK_A)
        ic_r, ic_c = gr[h:h + 1, :], gc[:, h:h + 1]
        fc_c, fc_r = fcum_c[:, NH_A + h:NH_A + h + 1], fcum_r[NH_A + h:NH_A + h + 1, :]
        m = m_scr[h:h + 1, 0:1]
        c = c_scr[h]
        nrow = n_scr[h:h + 1, :]
        qh = q_ref[:, sl]
        kh = k_ref[:, sl] * (DK_A ** -0.5)
        vh = v_ref[:, sl]
        dmat = jnp.where(tril, fc_c - fc_r + ic_r, -jnp.inf)
        inter = fc_c + m
        mt = jnp.maximum(inter, jnp.max(dmat, axis=1, keepdims=True))
        wmat = _bdot(qh, kh, NT) * jnp.exp(dmat - mt)
        a_inter = jnp.exp(inter - mt)
        num = _bdot(wmat, vh) + a_inter * _bdot(qh, c)
        den = jnp.sum(wmat, axis=1, keepdims=True) + a_inter * jnp.sum(qh * nrow, axis=1, keepdims=True)
        hh = num / jnp.maximum(jnp.abs(den), jnp.exp(-mt))
        f_end = fc_c[ln - 1:ln, :]
        m_new = jnp.maximum(f_end + m, jnp.max(f_end - fc_r + ic_r, axis=1, keepdims=True))
        kw = kh * jnp.exp(f_end - fc_c + ic_c - m_new)
        decay = jnp.exp(f_end + m - m_new)
        c_scr[h] = decay * c + _bdot(kw, vh, TN)
        n_scr[h:h + 1, :] = decay * nrow + jnp.sum(kw, axis=0, keepdims=True)
        m_scr[h:h + 1, :] = jnp.broadcast_to(m_new, (1, LANES))
        ha = _sigmoid(o_ref[:, sl]) * hh
        h_ref[:, sl] = _rms(ha, nw_ref[:, sl])

    @pl.when(ci == pl.num_programs(1) - 1)
    def _():
        cout_ref[0] = c_scr[...]
        nout_ref[0] = n_scr[0:NH_A, :]
        mout_ref[0] = m_scr[0:NH_A, :]


def _mlstm_prompt(a_main, g_col, b_gates, norm_w, batch, seq):
    m = a_main.shape[0]
    ln = CHUNK_A
    nc = seq // ln
    w = NH_A * DK_A
    blk = lambda j: pl.BlockSpec((ln, w), lambda b, c, j=j: (b * nc + c, j))
    bc = jnp.zeros((1, LANES), F32).at[0, :2 * NH_A].set(b_gates)
    const = lambda b, c: (0, 0)
    return pl.pallas_call(
        _mlstm_kernel, grid=(batch, nc),
        in_specs=[blk(0), blk(1), blk(2), blk(3),
                  pl.BlockSpec((ln, LANES), lambda b, c: (b * nc + c, 0)),
                  pl.BlockSpec((1, LANES), const), pl.BlockSpec((1, w), const)],
        out_specs=[pl.BlockSpec((ln, w), lambda b, c: (b * nc + c, 0)),
                   pl.BlockSpec((1, NH_A, DK_A, DV_A), lambda b, c: (b, 0, 0, 0)),
                   pl.BlockSpec((1, NH_A, DK_A), lambda b, c: (b, 0, 0)),
                   pl.BlockSpec((1, NH_A, LANES), lambda b, c: (b, 0, 0))],
        out_shape=[jax.ShapeDtypeStruct((m, w), F32),
                   jax.ShapeDtypeStruct((batch, NH_A, DK_A, DV_A), F32),
                   jax.ShapeDtypeStruct((batch, NH_A, DK_A), F32),
                   jax.ShapeDtypeStruct((batch, NH_A, LANES), F32)],
        scratch_shapes=[pltpu.VMEM((NH_A, DK_A, DV_A), F32), pltpu.VMEM((SUBLANES, LANES), F32),
                        pltpu.VMEM((SUBLANES, LANES), F32)],
        compiler_params=_params(("arbitrary", "arbitrary")), name="mlstm_prompt",
    )(a_main, a_main, a_main, a_main, g_col, bc, norm_w.reshape(1, w))


def _mlstm_step_kernel(a_ref, qc_ref, kc_ref, g_ref, b_ref, m_ref, c_ref, n_ref, nw_ref,
                       h_ref, cout_ref, nout_ref, mout_ref):
    g = g_ref[0] + b_ref[...]
    lf_all = _log_sigmoid(g)
    w = NH_A * DK_A
    scale = DK_A ** -0.5
    for h in range(NH_A):
        sl = slice(h * DK_A, (h + 1) * DK_A)
        q_row = a_ref[0, :, sl]
        k_row = a_ref[0, :, w + h * DK_A:w + (h + 1) * DK_A] * scale
        v_row = a_ref[0, :, 2 * w + h * DV_A:2 * w + (h + 1) * DV_A]
        o_row = a_ref[0, :, 3 * w + h * DV_A:3 * w + (h + 1) * DV_A]
        q_col = qc_ref[0, sl, :]
        k_col = kc_ref[0, sl, :] * scale
        c = c_ref[0, h]
        n_row = n_ref[0, h:h + 1, :]
        m = m_ref[0, :, h:h + 1]
        ic = g[:, h:h + 1]
        lf = lf_all[:, NH_A + h:NH_A + h + 1]
        inter = lf + m
        mt = jnp.maximum(inter, ic)
        wm = jnp.sum(q_row * k_row, axis=1, keepdims=True) * jnp.exp(ic - mt)
        a_inter = jnp.exp(inter - mt)
        num = wm * v_row + a_inter * jnp.sum(q_col * c, axis=0, keepdims=True)
        den = wm + a_inter * jnp.sum(q_row * n_row, axis=1, keepdims=True)
        hh = num / jnp.maximum(jnp.abs(den), jnp.exp(-mt))
        m_new = jnp.maximum(inter, ic)
        wgt = jnp.exp(ic - m_new)
        decay = jnp.exp(inter - m_new)
        cout_ref[0, h] = decay * c + (wgt * k_col) * v_row
        nout_ref[0, h:h + 1, :] = decay * n_row + wgt * k_row
        mout_ref[0, h:h + 1, :] = jnp.broadcast_to(m_new, (1, LANES))
        ha = _sigmoid(o_row) * hh
        h_ref[0, :, sl] = _rms(ha, nw_ref[:, sl])


def _mlstm_step(a_main, g_col, b_gates, norm_w, c0, n0, m0):
    ns = a_main.shape[0]
    w = NH_A * DK_A
    a3 = a_main.reshape(ns, 1, 4 * w)
    q_col = a_main[:, :w].reshape(ns, w, 1)
    k_col = a_main[:, w:2 * w].reshape(ns, w, 1)
    bc = jnp.zeros((1, LANES), F32).at[0, :2 * NH_A].set(b_gates)
    s3 = lambda s: (s, 0, 0)
    s4 = lambda s: (s, 0, 0, 0)
    const = lambda s: (0, 0)
    return pl.pallas_call(
        _mlstm_step_kernel, grid=(ns,),
        in_specs=[pl.BlockSpec((1, 1, 4 * w), s3), pl.BlockSpec((1, w, 1), s3), pl.BlockSpec((1, w, 1), s3),
                  pl.BlockSpec((1, 1, LANES), s3), pl.BlockSpec((1, LANES), const),
                  pl.BlockSpec((1, 1, NH_A), s3), pl.BlockSpec((1, NH_A, DK_A, DV_A), s4),
                  pl.BlockSpec((1, NH_A, DK_A), s3), pl.BlockSpec((1, w), const)],
        out_specs=[pl.BlockSpec((1, 1, w), s3), pl.BlockSpec((1, NH_A, DK_A, DV_A), s4),
                   pl.BlockSpec((1, NH_A, DK_A), s3), pl.BlockSpec((1, NH_A, LANES), s3)],
        out_shape=[jax.ShapeDtypeStruct((ns, 1, w), F32), jax.ShapeDtypeStruct((ns, NH_A, DK_A, DV_A), F32),
                   jax.ShapeDtypeStruct((ns, NH_A, DK_A), F32), jax.ShapeDtypeStruct((ns, NH_A, LANES), F32)],
        compiler_params=_params(("arbitrary",)), name="mlstm_step",
    )(a3, q_col, k_col, g_col.reshape(ns, 1, LANES), bc, m0.reshape(ns, 1, NH_A), c0, n0, norm_w.reshape(1, w))


def _top_blocks(bs, topk):
    blk_id = lax.broadcasted_iota(jnp.int32, bs.shape, 0).astype(F32)
    sel = jnp.zeros(bs.shape, F32)
    for _ in range(topk):
        mx = jnp.max(bs, axis=0, keepdims=True)
        idx = jnp.min(jnp.where(bs == mx, blk_id, float(bs.shape[0])), axis=0, keepdims=True)
        pick = (blk_id == idx) & (mx > -jnp.inf)
        sel = jnp.where(pick, 1.0, sel)
        bs = jnp.where(pick, -jnp.inf, bs)
    return sel


def _moba_kernel(n_full, q_ref, k_ref, v_ref, o_ref, kmean_scr, kb_scr, vt_scr, sel_scr, m_scr, l_scr, acc_scr):
    qi = pl.program_id(2)
    blk = MOBA_BLOCK
    grp = NH_B // KVH_B

    @pl.when(qi == 0)
    def _():
        kmean_scr[...] = jnp.zeros_like(kmean_scr)
        for n in range(n_full):
            kblk = k_ref[n * blk:(n + 1) * blk, :]
            kmean_scr[n:n + 1, :] = jnp.mean(kblk, axis=0, keepdims=True)
            kb_scr[n] = kblk.astype(BF16)
            vt_scr[n] = jnp.transpose(v_ref[n * blk:(n + 1) * blk, :]).astype(BF16)

    q = jnp.concatenate([q_ref[:, g * HD_B:(g + 1) * HD_B] for g in range(grp)], axis=0)
    rows = grp * blk
    bs = _dot(kmean_scr[...], q, NT, HIGHEST)
    bs = jnp.where(lax.broadcasted_iota(jnp.int32, bs.shape, 0) < qi, bs, -jnp.inf)
    sel_scr[...] = _top_blocks(bs, min(MOBA_TOPK, n_full))
    qs = (q * (HD_B ** -0.5 * math.log2(math.e))).astype(BF16)

    key = lax.broadcasted_iota(jnp.int32, (blk, rows), 0)
    qpos = lax.broadcasted_iota(jnp.int32, (blk, rows), 1) % blk
    causal = key <= qpos

    def attend(first, own_group, m_old, l_old, acc_old):
        scores = []
        for g in range(MOBA_GROUP):
            b = first + g
            ok = sel_scr[pl.ds(b, 1), :] > 0.0
            if own_group:
                ok = ((b < qi) & ok) | ((b == qi) & causal)
            scores.append(jnp.where(ok, _dot(kb_scr[b], qs, NT), -jnp.inf))
        m_new = m_old
        for s in scores:
            m_new = jnp.maximum(m_new, jnp.max(s, axis=0, keepdims=True))
        alpha = jnp.exp2(m_old - m_new)
        l_new, acc_new = alpha * l_old, alpha * acc_old
        for g, s in enumerate(scores):
            p = jnp.exp2(s - m_new)
            l_new = l_new + jnp.sum(p, axis=0, keepdims=True)
            acc_new = acc_new + _dot(vt_scr[first + g], p.astype(BF16))
        m_scr[...] = m_new
        l_scr[...] = l_new
        acc_scr[...] = acc_new

    own_first = pl.multiple_of((qi // MOBA_GROUP) * MOBA_GROUP, MOBA_GROUP)
    attend(own_first, True, jnp.full((1, rows), -jnp.inf, F32), jnp.zeros((1, rows), F32),
           jnp.zeros((HD_B, rows), F32))

    def body(i, carry):
        attend(pl.multiple_of(i * MOBA_GROUP, MOBA_GROUP), False, m_scr[...], l_scr[...], acc_scr[...])
        return carry

    lax.fori_loop(0, qi // MOBA_GROUP, body, 0)
    out = acc_scr[...] / l_scr[...]
    for g in range(grp):
        o_ref[:, g * HD_B:(g + 1) * HD_B] = jnp.transpose(out[:, g * blk:(g + 1) * blk])


MOBA_GROUP = 4


def _moba_prompt(qb, kb, vb, batch, seq):
    m = qb.shape[0]
    blk = MOBA_BLOCK
    nq = seq // blk
    assert seq % (blk * MOBA_GROUP) == 0
    grp = NH_B // KVH_B
    return pl.pallas_call(
        functools.partial(_moba_kernel, seq // blk), grid=(batch, KVH_B, nq),
        in_specs=[pl.BlockSpec((blk, grp * HD_B), lambda b, j, i: (b * nq + i, j)),
                  pl.BlockSpec((seq, HD_B), lambda b, j, i: (b, j)),
                  pl.BlockSpec((seq, HD_B), lambda b, j, i: (b, j))],
        out_specs=pl.BlockSpec((blk, grp * HD_B), lambda b, j, i: (b * nq + i, j)),
        out_shape=jax.ShapeDtypeStruct((m, NH_B * HD_B), F32),
        scratch_shapes=[pltpu.VMEM((-(-nq // SUBLANES) * SUBLANES, HD_B), F32),
                        pltpu.VMEM((nq, blk, HD_B), BF16), pltpu.VMEM((nq, HD_B, blk), BF16),
                        pltpu.VMEM((-(-nq // SUBLANES) * SUBLANES, grp * blk), F32),
                        pltpu.VMEM((1, grp * blk), F32), pltpu.VMEM((1, grp * blk), F32),
                        pltpu.VMEM((HD_B, grp * blk), F32)],
        compiler_params=_params(("arbitrary", "arbitrary", "arbitrary")), name="moba_prompt",
    )(qb, kb, vb)


MOBA_PAGES_PER_STEP = 32


def _moba_select_kernel(n_steps, pt_ref, q_ref, *refs):
    pages, (kmean_ref, sel_ref) = refs[:-2], refs[-2:]
    j = pl.program_id(1)
    page = pages[0].shape[1] // KVH_B

    def page_sum(p):
        heads = [jnp.sum(p[0, pl.ds(kv, page, stride=KVH_B), :], axis=0, keepdims=True) for kv in range(KVH_B)]
        return jnp.concatenate(heads, axis=1)

    sums = [page_sum(p) for p in pages]
    nblk = len(pages) // 2
    per_blk = [(sums[2 * i] + sums[2 * i + 1]) * (1.0 / MOBA_BLOCK) for i in range(nblk)]
    kmean_ref[0, pl.ds(pl.multiple_of(j * nblk, SUBLANES), nblk), :] = jnp.concatenate(per_blk, axis=0)

    @pl.when(j == n_steps - 1)
    def _():
        grp = NH_B // KVH_B
        nb = kmean_ref.shape[1]
        rowi = lax.broadcasted_iota(jnp.int32, (nb, 1), 0).astype(F32)
        orow = lax.broadcasted_iota(jnp.int32, (SUBLANES, LANES), 0)
        olane = lax.broadcasted_iota(jnp.int32, (SUBLANES, LANES), 1)
        out = jnp.zeros((SUBLANES, LANES), F32)
        for h in range(NH_B):
            kv = h // grp
            km = kmean_ref[0, :, kv * HD_B:(kv + 1) * HD_B]
            qh = q_ref[0, :, h * HD_B:(h + 1) * HD_B]
            bs = jnp.sum(km * qh, axis=1, keepdims=True)
            for r in range(MOBA_TOPK):
                mx = jnp.max(bs, axis=0, keepdims=True)
                idx = jnp.min(jnp.where(bs == mx, rowi, float(nb)), axis=0, keepdims=True)
                out = jnp.where((orow == h) & (olane == r), idx, out)
                bs = jnp.where(rowi == idx, -jnp.inf, bs)
        sel_ref[0] = out.astype(jnp.int32)


def _moba_select(qb, k_cache, page_table):
    ns, n_pages = page_table.shape
    page = k_cache.shape[1]
    pps = min(MOBA_PAGES_PER_STEP, n_pages)
    assert 2 * page == MOBA_BLOCK and n_pages % pps == 0 and pps % (2 * SUBLANES) == 0
    n_steps = n_pages // pps
    kw = KVH_B * HD_B
    kc = k_cache.reshape(k_cache.shape[0], page * KVH_B, HD_B)
    page_spec = lambda i: pl.BlockSpec(
        (1, page * KVH_B, HD_B), lambda s, j, pt, i=i: (pt[s * n_pages + j * pps + i], 0, 0))
    nb = n_pages * page // MOBA_BLOCK
    grid_spec = pltpu.PrefetchScalarGridSpec(
        num_scalar_prefetch=1, grid=(ns, n_steps),
        in_specs=[pl.BlockSpec((1, 1, NH_B * HD_B), lambda s, j, pt: (s, 0, 0))]
        + [page_spec(i) for i in range(pps)],
        out_specs=[pl.BlockSpec((1, nb, kw), lambda s, j, pt: (s, 0, 0)),
                   pl.BlockSpec((1, SUBLANES, LANES), lambda s, j, pt: (s, 0, 0))])
    return pl.pallas_call(
        functools.partial(_moba_select_kernel, n_steps), grid_spec=grid_spec,
        out_shape=[jax.ShapeDtypeStruct((ns, nb, kw), F32), jax.ShapeDtypeStruct((ns, SUBLANES, LANES), jnp.int32)],
        compiler_params=_params(("arbitrary", "arbitrary")), name="moba_select",
    )(page_table.reshape(-1), qb.reshape(ns, 1, NH_B * HD_B), *([kc] * pps))


def _moba_step_kernel(sel_ref, pt_ref, q_ref, kn_ref, vn_ref, *refs):
    pages, o_ref, (m_scr, l_scr, acc_scr) = refs[:4 * NH_B], refs[4 * NH_B], refs[4 * NH_B + 1:]
    r = pl.program_id(1)
    scale = HD_B ** -0.5
    grp = NH_B // KVH_B
    page = pages[0].shape[1] // KVH_B
    for h in range(NH_B):
        kv = h // grp
        row = slice(h, h + 1)
        q = q_ref[0, :, h * HD_B:(h + 1) * HD_B]
        ka_ref, kb_ref, va_ref, vb_ref = pages[4 * h:4 * h + 4]

        @pl.when(r == 0)
        def _():
            s0 = jnp.sum(q * kn_ref[0, :, kv * HD_B:(kv + 1) * HD_B], axis=1, keepdims=True) * scale
            m_scr[row, :] = jnp.broadcast_to(s0, (1, LANES))
            l_scr[row, :] = jnp.ones((1, LANES), F32)
            acc_scr[row, :] = vn_ref[0, :, kv * HD_B:(kv + 1) * HD_B]

        rows = pl.ds(kv, page, stride=KVH_B)
        kk = jnp.concatenate([ka_ref[0, rows, :], kb_ref[0, rows, :]], axis=0)
        vv = jnp.concatenate([va_ref[0, rows, :], vb_ref[0, rows, :]], axis=0)
        s = jnp.sum(kk * q, axis=1, keepdims=True) * scale
        m_old = m_scr[row, 0:1]
        m_new = jnp.maximum(m_old, jnp.max(s, axis=0, keepdims=True))
        alpha = jnp.exp(m_old - m_new)
        p = jnp.exp(s - m_new)
        m_scr[row, :] = jnp.broadcast_to(m_new, (1, LANES))
        l_scr[row, :] = alpha * l_scr[row, :] + jnp.sum(p, axis=0, keepdims=True)
        acc_scr[row, :] = alpha * acc_scr[row, :] + jnp.sum(p * vv, axis=0, keepdims=True)

        @pl.when(r == pl.num_programs(1) - 1)
        def _():
            o_ref[0, :, h * HD_B:(h + 1) * HD_B] = acc_scr[row, :] / l_scr[row, :]


def _moba_step(qb, k_new, v_new, k_cache, v_cache, page_table, sel):
    ns, n_pages = page_table.shape
    page = k_cache.shape[1]
    kw = KVH_B * HD_B
    kc = k_cache.reshape(k_cache.shape[0], page * KVH_B, HD_B)
    vc = v_cache.reshape(v_cache.shape[0], page * KVH_B, HD_B)

    def page_spec(h, half):
        def imap(s, r, sel_r, pt_r):
            blk = sel_r[(s * NH_B + h) * MOBA_TOPK + r]
            return (pt_r[s * n_pages + 2 * blk + half], 0, 0)
        return pl.BlockSpec((1, page * KVH_B, HD_B), imap)

    seq_row = lambda w: pl.BlockSpec((1, 1, w), lambda s, r, a, b: (s, 0, 0))
    page_specs, page_args = [], []
    for h in range(NH_B):
        page_specs += [page_spec(h, 0), page_spec(h, 1), page_spec(h, 0), page_spec(h, 1)]
        page_args += [kc, kc, vc, vc]
    grid_spec = pltpu.PrefetchScalarGridSpec(
        num_scalar_prefetch=2, grid=(ns, MOBA_TOPK),
        in_specs=[seq_row(NH_B * HD_B), seq_row(kw), seq_row(kw)] + page_specs,
        out_specs=seq_row(NH_B * HD_B),
        scratch_shapes=[pltpu.VMEM((SUBLANES, LANES), F32), pltpu.VMEM((SUBLANES, LANES), F32),
                        pltpu.VMEM((SUBLANES, HD_B), F32)])
    out = pl.pallas_call(
        _moba_step_kernel, grid_spec=grid_spec,
        out_shape=jax.ShapeDtypeStruct((ns, 1, NH_B * HD_B), F32),
        compiler_params=_params(("arbitrary", "arbitrary")), name="moba_step",
    )(sel.reshape(-1), page_table.reshape(-1), qb.reshape(ns, 1, NH_B * HD_B),
      k_new.reshape(ns, 1, kw), v_new.reshape(ns, 1, kw), *page_args)
    return out.reshape(ns, NH_B * HD_B)


def _outproj_kernel(a_ref, b_ref, wa_ref, wb_ref, x_ref, o_ref):
    o_ref[...] = x_ref[...] + _bdot(a_ref[...], wa_ref[...]) + _bdot(b_ref[...], wb_ref[...])


def _outproj(a, b, wa, wb, x, tm, name):
    m, d = x.shape
    row = lambda i: (i, 0)
    const = lambda i: (0, 0)
    return pl.pallas_call(
        _outproj_kernel, grid=(m // tm,),
        in_specs=[pl.BlockSpec((tm, a.shape[1]), row), pl.BlockSpec((tm, b.shape[1]), row),
                  pl.BlockSpec(wa.shape, const), pl.BlockSpec(wb.shape, const), pl.BlockSpec((tm, d), row)],
        out_specs=pl.BlockSpec((tm, d), row), out_shape=jax.ShapeDtypeStruct((m, d), F32),
        compiler_params=_params(("arbitrary",)), name=name,
    )(a, b, wa, wb, x)


def _ffn_kernel(final, x_ref, nw_ref, wup_ref, cw_ref, cb_ref, wdn_ref, nf_ref, o_ref, cs_ref, ext_scr):
    tm = x_ref.shape[0]
    pad = SUBLANES

    @pl.when(pl.program_id(1) == 0)
    def _():
        ext_scr[0:pad, :] = jnp.zeros((pad, ext_scr.shape[1]), F32)

    x = x_ref[...]
    xb = _rms(x, nw_ref[...]).astype(BF16)
    acc = x

    def up(col0):
        cols = slice(col0, col0 + FFN_CW)
        u = _dot(xb, wup_ref[:, cols])
        ext_scr[pad:pad + tm, cols] = u
        return u

    def conv(u, col0):
        cols = slice(col0, col0 + FFN_CW)
        y = cb_ref[:, cols] + cw_ref[FFN_CONV - 1:FFN_CONV, cols] * u
        for j in range(FFN_CONV - 1):
            off = pad - (FFN_CONV - 1) + j
            y = y + cw_ref[j:j + 1, cols] * ext_scr[off:off + tm, cols]
        ext_scr[0:pad, cols] = ext_scr[tm:tm + pad, cols]
        return y

    nch = D_FF // FFN_CW
    ups = (up(0), up(D_FF))
    for c in range(nch):
        nxt = (up((c + 1) * FFN_CW), up(D_FF + (c + 1) * FFN_CW)) if c + 1 < nch else None
        gate = conv(ups[0], c * FFN_CW)
        val = conv(ups[1], D_FF + c * FFN_CW)
        act = (_silu(gate) * val).astype(BF16)
        acc = acc + _dot(act, wdn_ref[c * FFN_CW:(c + 1) * FFN_CW, :])
        ups = nxt
    if final:
        acc = _rms(acc, nf_ref[...])
    o_ref[...] = acc
    cs_ref[0] = ext_scr[0:pad, :]


def _ffn_prompt(x, nw, w_up, conv_w, conv_b, w_down, nf, final, batch, seq, tm, name):
    m, d = x.shape
    nt = seq // tm
    row = lambda b, i: (b * nt + i, 0)
    const = lambda b, i: (0, 0)
    out, cs = pl.pallas_call(
        functools.partial(_ffn_kernel, final), grid=(batch, nt),
        in_specs=[pl.BlockSpec((tm, d), row), pl.BlockSpec((1, d), const), pl.BlockSpec(w_up.shape, const),
                  pl.BlockSpec(conv_w.shape, const), pl.BlockSpec((1, 2 * D_FF), const),
                  pl.BlockSpec(w_down.shape, const), pl.BlockSpec((1, d), const)],
        out_specs=[pl.BlockSpec((tm, d), row), pl.BlockSpec((1, SUBLANES, 2 * D_FF), lambda b, i: (b, 0, 0))],
        out_shape=[jax.ShapeDtypeStruct((m, d), F32), jax.ShapeDtypeStruct((batch, SUBLANES, 2 * D_FF), F32)],
        scratch_shapes=[pltpu.VMEM((tm + SUBLANES, 2 * D_FF), F32)],
        compiler_params=_params(("arbitrary", "arbitrary")), name=name,
    )(x, nw.reshape(1, d), w_up, conv_w, conv_b.reshape(1, -1), w_down, nf.reshape(1, d))
    return out, cs[:, SUBLANES - (FFN_CONV - 1):, :]


def _ffn_step_kernel(final, x_ref, nw_ref, wg_ref, wv_ref, g0_ref, g1_ref, v0_ref, v1_ref,
                     cwg_ref, cwv_ref, cbg_ref, cbv_ref, wdn_ref, nf_ref, o_ref, ug_ref, uv_ref, acc_scr):
    c = pl.program_id(0)

    @pl.when(c == 0)
    def _():
        acc_scr[...] = x_ref[...]

    xb = _rms(x_ref[...], nw_ref[...]).astype(BF16)
    ug = _dot(xb, wg_ref[...])
    uv = _dot(xb, wv_ref[...])
    ug_ref[...] = ug
    uv_ref[...] = uv
    gate = cbg_ref[...] + cwg_ref[0:1, :] * g0_ref[...] + cwg_ref[1:2, :] * g1_ref[...] + cwg_ref[2:3, :] * ug
    val = cbv_ref[...] + cwv_ref[0:1, :] * v0_ref[...] + cwv_ref[1:2, :] * v1_ref[...] + cwv_ref[2:3, :] * uv
    acc_scr[...] += _dot((_silu(gate) * val).astype(BF16), wdn_ref[...])

    @pl.when(c == pl.num_programs(0) - 1)
    def _():
        acc = acc_scr[...]
        o_ref[...] = _rms(acc, nf_ref[...]) if final else acc


def _ffn_step(x, nw, w_up, conv_w, conv_b, w_down, nf, final, buf, name):
    ns, d = x.shape
    cw = FFN_CW
    nch = D_FF // cw
    ntot = 2 * D_FF // cw
    bufw = buf.reshape(ns, 2 * 2 * D_FF)
    cb = conv_b.reshape(1, -1)
    const = lambda c: (0, 0)
    gcol = lambda c: (0, c)
    vcol = lambda c: (0, nch + c)
    out, u_g, u_v = pl.pallas_call(
        functools.partial(_ffn_step_kernel, final), grid=(nch,),
        in_specs=[pl.BlockSpec((ns, d), const), pl.BlockSpec((1, d), const),
                  pl.BlockSpec((d, cw), gcol), pl.BlockSpec((d, cw), vcol),
                  pl.BlockSpec((ns, cw), gcol), pl.BlockSpec((ns, cw), lambda c: (0, ntot + c)),
                  pl.BlockSpec((ns, cw), vcol), pl.BlockSpec((ns, cw), lambda c: (0, ntot + nch + c)),
                  pl.BlockSpec((FFN_CONV, cw), gcol), pl.BlockSpec((FFN_CONV, cw), vcol),
                  pl.BlockSpec((1, cw), gcol), pl.BlockSpec((1, cw), vcol),
                  pl.BlockSpec((cw, d), lambda c: (c, 0)), pl.BlockSpec((1, d), const)],
        out_specs=[pl.BlockSpec((ns, d), const), pl.BlockSpec((ns, cw), gcol), pl.BlockSpec((ns, cw), gcol)],
        out_shape=[jax.ShapeDtypeStruct((ns, d), F32), jax.ShapeDtypeStruct((ns, D_FF), F32),
                   jax.ShapeDtypeStruct((ns, D_FF), F32)],
        scratch_shapes=[pltpu.VMEM((ns, d), F32)],
        compiler_params=_params(("arbitrary",)), name=name,
    )(x, nw.reshape(1, d), w_up, w_up, bufw, bufw, bufw, bufw, conv_w, conv_w, cb, cb, w_down, nf.reshape(1, d))
    u = jnp.concatenate([u_g, u_v], axis=1)
    return out, jnp.stack([buf[:, 1], u], axis=1)


def _ssd_kernel(z_ref, xbc_ref, dc_ref, cw_ref, cb_ref, dbc_ref, alc_ref,
                dsk_ref, nw_ref, y_ref, hout_ref, cs_ref, h_scr, carry_scr, ext_scr, y_scr):
    ci = pl.program_id(1)
    ln = CHUNK_C
    pad = SUBLANES

    @pl.when(ci == 0)
    def _():
        h_scr[...] = jnp.zeros_like(h_scr)
        carry_scr[...] = jnp.zeros_like(carry_scr)

    raw = xbc_ref[...]
    ext_scr[0:pad, :] = carry_scr[...]
    ext_scr[pad:pad + ln, :] = raw
    carry_scr[...] = ext_scr[ln:ln + pad, :]
    y = cb_ref[...] + cw_ref[CONV_C - 1:CONV_C, :] * raw
    for j in range(CONV_C - 1):
        off = pad - (CONV_C - 1) + j
        y = y + cw_ref[j:j + 1, :] * ext_scr[off:off + ln, :]
    xbc = _silu(y)
    bw = NG_C * DS_C
    xs, bm, cm = xbc[:, :DI_C], xbc[:, DI_C:DI_C + bw], xbc[:, DI_C + bw:]

    row = lax.broadcasted_iota(jnp.int32, (ln, ln), 0)
    col = lax.broadcasted_iota(jnp.int32, (ln, ln), 1)
    tril = row >= col
    lane_lo = lax.broadcasted_iota(jnp.int32, (ln, LANES), 1) < HD_C
    row_lo = lax.broadcasted_iota(jnp.int32, (2 * HD_C, DS_C), 0) < HD_C
    dt_c = _softplus(dc_ref[...] + dbc_ref[...])
    ac_c = dt_c * (-jnp.exp(alc_ref[...]))
    dt_r = jnp.transpose(dt_c)[0:NH_C, :]
    ac_r = jnp.transpose(ac_c)[0:NH_C, :]
    acum_c = _dot(tril.astype(F32), ac_c, NN, HIGHEST)
    acum_r = _dot(ac_r, (row <= col).astype(F32), NN, HIGHEST)
    hpg = NH_C // NG_C
    for g in range(NG_C):
        bg = bm[:, g * DS_C:(g + 1) * DS_C]
        cg = cm[:, g * DS_C:(g + 1) * DS_C]
        cb_mat = _bdot(cg, bg, NT)
        for pi in range(hpg // 2):
            pair = g * (hpg // 2) + pi
            h0, h1 = 2 * pair, 2 * pair + 1
            xp = xs[:, pair * LANES:(pair + 1) * LANES]
            hp = h_scr[pair]
            a0, a1 = acum_c[:, h0:h0 + 1], acum_c[:, h1:h1 + 1]

            def mmat(h, a_col):
                decay = jnp.exp(jnp.where(tril, a_col - acum_r[h:h + 1, :], -jnp.inf))
                return cb_mat * decay * dt_r[h:h + 1, :]

            yy = _bdot(mmat(h0, a0), jnp.where(lane_lo, xp, 0.0)) + _bdot(mmat(h1, a1), jnp.where(lane_lo, 0.0, xp))
            yy = yy + jnp.where(lane_lo, jnp.exp(a0), jnp.exp(a1)) * _bdot(cg, hp, NT)
            e0, e1 = a0[ln - 1:ln, :], a1[ln - 1:ln, :]
            wend = jnp.where(lane_lo, jnp.exp(e0 - a0) * dt_c[:, h0:h0 + 1], jnp.exp(e1 - a1) * dt_c[:, h1:h1 + 1])
            h_scr[pair] = jnp.where(row_lo, jnp.exp(e0), jnp.exp(e1)) * hp + _bdot(xp * wend, bg, TN)
            y_scr[:, pair * LANES:(pair + 1) * LANES] = yy + dsk_ref[:, pair * LANES:(pair + 1) * LANES] * xp
    gated = y_scr[...] * _silu(z_ref[...])
    gw = DI_C // NG_C
    for g in range(NG_C):
        sl = slice(g * gw, (g + 1) * gw)
        y_ref[:, sl] = _rms(gated[:, sl], nw_ref[:, sl])
    cs_ref[0] = carry_scr[...]

    @pl.when(ci == pl.num_programs(1) - 1)
    def _():
        hout_ref[0] = h_scr[...]


def _head_lanes(v, width):
    return jnp.repeat(v.astype(F32), width).reshape(1, -1)


def _ssd_prompt(z, xbc, d_col, conv_w, conv_b, dt_bias, a_log, d_skip, norm_w, batch, seq):
    m = z.shape[0]
    ln = CHUNK_C
    nc = seq // ln
    npair = NH_C // 2
    lane8 = lambda v: jnp.zeros((1, LANES), F32).at[0, :NH_C].set(v)
    row = lambda b, c: (b * nc + c, 0)
    const = lambda b, c: (0, 0)
    y, h1, cs = pl.pallas_call(
        _ssd_kernel, grid=(batch, nc),
        in_specs=[pl.BlockSpec((ln, DI_C), row), pl.BlockSpec((ln, CONV_DIM_C), row),
                  pl.BlockSpec((ln, LANES), row),
                  pl.BlockSpec((CONV_C, CONV_DIM_C), const), pl.BlockSpec((1, CONV_DIM_C), const),
                  pl.BlockSpec((1, LANES), const), pl.BlockSpec((1, LANES), const),
                  pl.BlockSpec((1, DI_C), const), pl.BlockSpec((1, DI_C), const)],
        out_specs=[pl.BlockSpec((ln, DI_C), row),
                   pl.BlockSpec((1, npair, 2 * HD_C, DS_C), lambda b, c: (b, 0, 0, 0)),
                   pl.BlockSpec((1, SUBLANES, CONV_DIM_C), lambda b, c: (b, 0, 0))],
        out_shape=[jax.ShapeDtypeStruct((m, DI_C), F32),
                   jax.ShapeDtypeStruct((batch, npair, 2 * HD_C, DS_C), F32),
                   jax.ShapeDtypeStruct((batch, SUBLANES, CONV_DIM_C), F32)],
        scratch_shapes=[pltpu.VMEM((npair, 2 * HD_C, DS_C), F32), pltpu.VMEM((SUBLANES, CONV_DIM_C), F32),
                        pltpu.VMEM((ln + SUBLANES, CONV_DIM_C), F32), pltpu.VMEM((ln, DI_C), F32)],
        compiler_params=_params(("arbitrary", "arbitrary")), name="ssd_prompt",
    )(z, xbc, d_col, conv_w, conv_b.reshape(1, -1), lane8(dt_bias), lane8(a_log),
      _head_lanes(d_skip, HD_C), norm_w.reshape(1, DI_C))
    return y, h1.reshape(batch, NH_C, HD_C, DS_C), cs[:, SUBLANES - (CONV_C - 1):, :]


def _lane_bcast_col(row_vec):
    return jnp.transpose(jnp.broadcast_to(row_vec, (LANES, LANES)))


def _ssd_step_kernel(z_ref, x_ref, cs_ref, d_ref, cw_ref, cb_ref, db_ref, al_ref, dsk_ref, nw_ref, h_ref,
                     y_ref, hout_ref):
    y = cb_ref[...] + cw_ref[CONV_C - 1:CONV_C, :] * x_ref[0]
    for j in range(CONV_C - 1):
        y = y + cw_ref[j:j + 1, :] * cs_ref[0, j:j + 1, :]
    xbc = _silu(y)
    bw = NG_C * DS_C
    dt = _softplus(d_ref[0] + db_ref[...])
    ea = jnp.exp(dt * (-jnp.exp(al_ref[...])))
    row_lo = lax.broadcasted_iota(jnp.int32, (LANES, 1), 0) < HD_C
    lane_lo = lax.broadcasted_iota(jnp.int32, (1, LANES), 1) < HD_C
    hpg = NH_C // NG_C
    parts = []
    for pair in range(NH_C // 2):
        g = (2 * pair) // hpg
        h0, h1 = 2 * pair, 2 * pair + 1
        xp = xbc[:, pair * LANES:(pair + 1) * LANES]
        bg = xbc[:, DI_C + g * DS_C:DI_C + (g + 1) * DS_C]
        cg = xbc[:, DI_C + bw + g * DS_C:DI_C + bw + (g + 1) * DS_C]
        hp = h_ref[0, pair]
        x_col = _lane_bcast_col(xp)
        ea_col = jnp.where(row_lo, ea[:, h0:h0 + 1], ea[:, h1:h1 + 1])
        dt_col = jnp.where(row_lo, dt[:, h0:h0 + 1], dt[:, h1:h1 + 1])
        hout_ref[0, pair] = ea_col * hp + (dt_col * x_col) * bg
        y_inter = jnp.sum(hp * cg, axis=1, keepdims=True)
        y_col = ea_col * y_inter
        y_row = jnp.transpose(jnp.broadcast_to(y_col, (LANES, LANES)))[0:1, :]
        dt_row = jnp.where(lane_lo, dt[:, h0:h0 + 1], dt[:, h1:h1 + 1])
        cb = jnp.sum(cg * bg, axis=1, keepdims=True)
        parts.append(y_row + (cb * dt_row) * xp + dsk_ref[:, pair * LANES:(pair + 1) * LANES] * xp)
    gated = jnp.concatenate(parts, axis=1) * _silu(z_ref[0])
    gw = DI_C // NG_C
    for g in range(NG_C):
        sl = slice(g * gw, (g + 1) * gw)
        y_ref[0, :, sl] = _rms(gated[:, sl], nw_ref[:, sl])


def _ssd_step(z, xbc, d_col, conv_state, conv_w, conv_b, dt_bias, a_log, d_skip, norm_w, h0):
    ns = z.shape[0]
    npair = NH_C // 2
    lane8 = lambda v: jnp.zeros((1, LANES), F32).at[0, :NH_C].set(v)
    s3 = lambda s: (s, 0, 0)
    s4 = lambda s: (s, 0, 0, 0)
    const = lambda s: (0, 0)
    y, h1 = pl.pallas_call(
        _ssd_step_kernel, grid=(ns,),
        in_specs=[pl.BlockSpec((1, 1, DI_C), s3), pl.BlockSpec((1, 1, CONV_DIM_C), s3),
                  pl.BlockSpec((1, CONV_C - 1, CONV_DIM_C), s3), pl.BlockSpec((1, 1, LANES), s3),
                  pl.BlockSpec((CONV_C, CONV_DIM_C), const), pl.BlockSpec((1, CONV_DIM_C), const),
                  pl.BlockSpec((1, LANES), const), pl.BlockSpec((1, LANES), const),
                  pl.BlockSpec((1, DI_C), const), pl.BlockSpec((1, DI_C), const),
                  pl.BlockSpec((1, npair, 2 * HD_C, DS_C), s4)],
        out_specs=[pl.BlockSpec((1, 1, DI_C), s3), pl.BlockSpec((1, npair, 2 * HD_C, DS_C), s4)],
        out_shape=[jax.ShapeDtypeStruct((ns, 1, DI_C), F32), jax.ShapeDtypeStruct((ns, npair, 2 * HD_C, DS_C), F32)],
        compiler_params=_params(("arbitrary",)), name="ssd_step",
    )(z.reshape(ns, 1, DI_C), xbc.reshape(ns, 1, CONV_DIM_C), conv_state, d_col.reshape(ns, 1, LANES),
      conv_w, conv_b.reshape(1, -1), lane8(dt_bias), lane8(a_log), _head_lanes(d_skip, HD_C),
      norm_w.reshape(1, DI_C), h0.reshape(ns, npair, 2 * HD_C, DS_C))
    return y.reshape(ns, DI_C), h1.reshape(ns, NH_C, HD_C, DS_C)


def _head_mask(width):
    lane = lax.broadcasted_iota(jnp.int32, (1, width), 1)
    return [(lane >= h * HD_D) & (lane < (h + 1) * HD_D) for h in range(width // HD_D)]


SWA_TILE = 2048


def _swa_kernel(dil, q_ref, kt_ref, kc_ref, vt_ref, vc_ref, acc_ref, m_ref, l_ref, kbuf, vbuf):
    t = pl.program_id(1)
    nk = SWA_KEYS
    tail = nk * dil
    tile = q_ref.shape[0]
    kbuf[0:tail, :] = kt_ref[...]
    kbuf[tail:tail + tile, :] = kc_ref[...]
    vbuf[0:tail, :] = vt_ref[...]
    vbuf[tail:tail + tile, :] = vc_ref[...]
    qi = lax.broadcasted_iota(jnp.int32, (nk, 2 * nk), 0)
    kj = lax.broadcasted_iota(jnp.int32, (nk, 2 * nk), 1)
    dist = nk + qi - kj
    band = (dist >= 0) & (dist <= nk)
    heads = _head_mask(LANES)

    for j in range(tile // tail):
        first = j * tail
        ok = band if j > 0 else band & ((t > 0) | (kj >= nk))
        q_win = q_ref.at[pl.ds(first, tail), :]
        k_win = kbuf.at[pl.ds(first, 2 * tail), :]
        v_win = vbuf.at[pl.ds(first, 2 * tail), :]
        outs = [o.at[pl.ds(first, tail), :] for o in (acc_ref, m_ref, l_ref)]
        for r in range(dil):
            q = q_win[pl.ds(r, nk, stride=dil), :] * (HD_D ** -0.5)
            kk = k_win[pl.ds(r, 2 * nk, stride=dil), :].astype(BF16)
            vv = v_win[pl.ds(r, 2 * nk, stride=dil), :].astype(BF16)
            acc = jnp.zeros((nk, LANES), F32)
            mm = jnp.zeros((nk, LANES), F32)
            ll = jnp.zeros((nk, LANES), F32)
            for hm in heads:
                s = _dot(jnp.where(hm, q, 0.0).astype(BF16), kk, NT)
                s = jnp.where(ok, s, -jnp.inf)
                mx = jnp.max(s, axis=1, keepdims=True)
                p = jnp.exp(s - mx)
                acc = jnp.where(hm, _dot(p.astype(BF16), vv), acc)
                mm = jnp.where(hm, mx, mm)
                ll = jnp.where(hm, jnp.sum(p, axis=1, keepdims=True), ll)
            for o, val in zip(outs, (acc, mm, ll)):
                o[pl.ds(r, nk, stride=dil), :] = val


def _swa_prompt(qd, kd, vd, g, dil, batch, seq):
    m = qd.shape[0]
    tile = min(SWA_TILE, seq)
    tail = SWA_KEYS * dil
    assert seq % tile == 0 and tile % tail == 0
    nt = seq // tile
    per = tile // tail
    halves = SWA_W // LANES
    cur = pl.BlockSpec((tile, LANES), lambda b, t, f: (b * nt + t, g * halves + f))
    prev = pl.BlockSpec((tail, LANES), lambda b, t, f: (jnp.maximum((b * nt + t) * per - 1, 0), g * halves + f))
    ospec = pl.BlockSpec((tile, LANES), lambda b, t, f: (b * nt + t, f))
    oshape = jax.ShapeDtypeStruct((m, SWA_W), F32)
    return pl.pallas_call(
        functools.partial(_swa_kernel, dil), grid=(batch, nt, halves),
        in_specs=[cur, prev, cur, prev, cur], out_specs=[ospec] * 3, out_shape=[oshape] * 3,
        scratch_shapes=[pltpu.VMEM((tail + tile, LANES), F32), pltpu.VMEM((tail + tile, LANES), F32)],
        compiler_params=_params(("arbitrary", "arbitrary", "arbitrary")), name=f"swa_prompt_{g}",
    )(qd, kd, kd, vd, vd)


def _swa_step_kernel(q_ref, kn_ref, vn_ref, c0_ref, c1_ref, c2_ref, o_ref):
    scale = HD_D ** -0.5
    ngrp = len(SWA_GROUPS)
    stats = []
    for g, (c_ref, (win, dil)) in enumerate(zip((c0_ref, c1_ref, c2_ref), SWA_GROUPS)):
        on_stride = lax.broadcasted_iota(jnp.int32, (1, win), 1) % dil == 0
        per_head = []
        for h in range(HPG_D):
            sl = slice((g * HPG_D + h) * HD_D, (g * HPG_D + h + 1) * HD_D)
            q = q_ref[0, sl, :] * scale
            s = jnp.sum(c_ref[0, 0, h] * q, axis=0, keepdims=True)
            s = jnp.where(on_stride, s, -jnp.inf)
            sn = jnp.sum(kn_ref[0, sl, :] * q, axis=0, keepdims=True)
            mx = jnp.maximum(jnp.max(s, axis=1, keepdims=True), sn)
            p = jnp.exp(s - mx)
            pn = jnp.exp(sn - mx)
            den = jnp.sum(p, axis=1, keepdims=True) + pn
            acc = jnp.sum(c_ref[0, 1, h] * p, axis=1, keepdims=True) + pn * vn_ref[0, sl, :]
            per_head.append((acc, mx, den))
        stats.append(per_head)
    for h in range(HPG_D):
        mx = functools.reduce(jnp.maximum, [stats[g][h][1] for g in range(ngrp)])
        wgt = [jnp.exp(stats[g][h][1] - mx) for g in range(ngrp)]
        num = functools.reduce(jnp.add, [wgt[g] * stats[g][h][0] for g in range(ngrp)])
        den = functools.reduce(jnp.add, [wgt[g] * stats[g][h][2] for g in range(ngrp)])
        o_ref[0, h * HD_D:(h + 1) * HD_D, :] = num / den


def _swa_step(qd, kd, vd, caches):
    ns = qd.shape[0]
    w3 = qd.shape[1]
    views, cspecs = [], []
    for (win, dil), cache in zip(SWA_GROUPS, caches):
        assert cache.shape[1] == win and win // dil == SWA_KEYS
        views.append(jnp.transpose(cache, (0, 2, 3, 4, 1)))
        cspecs.append(pl.BlockSpec((1, 2, HPG_D, HD_D, win), lambda s: (s, 0, 0, 0, 0)))
    s3 = lambda s: (s, 0, 0)
    col = pl.BlockSpec((1, w3, 1), s3)
    out = pl.pallas_call(
        _swa_step_kernel, grid=(ns,), in_specs=[col, col, col] + cspecs,
        out_specs=pl.BlockSpec((1, SWA_W, 1), s3), out_shape=jax.ShapeDtypeStruct((ns, SWA_W, 1), F32),
        compiler_params=_params(("arbitrary",)), name="swa_step",
    )(qd.reshape(ns, w3, 1), kd.reshape(ns, w3, 1), vd.reshape(ns, w3, 1), *views)
    return out.reshape(ns, SWA_W)


def _swa_merge_kernel(*refs):
    ins, o_ref = refs[:-1], refs[-1]
    ng = len(ins) // 3
    mx = ins[1][...]
    for g in range(1, ng):
        mx = jnp.maximum(mx, ins[3 * g + 1][...])
    num = jnp.zeros_like(mx)
    den = jnp.zeros_like(mx)
    for g in range(ng):
        wgt = jnp.exp(ins[3 * g + 1][...] - mx)
        num = num + wgt * ins[3 * g][...]
        den = den + wgt * ins[3 * g + 2][...]
    o_ref[...] = num / den


def _swa_merge(parts, tm):
    flat = [a for grp in parts for a in grp]
    m = flat[0].shape[0]
    spec = pl.BlockSpec((tm, SWA_W), lambda i: (i, 0))
    return pl.pallas_call(
        _swa_merge_kernel, grid=(m // tm,), in_specs=[spec] * len(flat), out_specs=spec,
        out_shape=jax.ShapeDtypeStruct((m, SWA_W), F32),
        compiler_params=_params(("arbitrary",)), name="swa_merge",
    )(*flat)


def _split_cols(w, sizes):
    out, start = [], 0
    for s in sizes:
        out.append(w[:, start:start + s])
        start += s
    return out


def _trunk(x3, pos0, state, wts, tm):
    batch, seq, d = x3.shape
    m = batch * seq
    x = x3.reshape(m, d)
    pos = pos0 + jnp.arange(seq, dtype=jnp.int32)
    prompt = state is None
    if not prompt:
        pos = jnp.broadcast_to(pos, (m,))
    tabs_b, half_b = _rope_tables(pos, HD_B)
    tabs_d, half_d = _rope_tables(pos, HD_D)
    wa = NH_A * DK_A

    a_main, qb, kb, vb, g_col = _inproj(
        x, wts["norm_mix"][0], tabs_b, wts["l0_w"], (0, half_b, half_b, 0, 0), tm, "inproj_l0")
    if prompt:
        ha, c1, n1, m1 = _mlstm_prompt(a_main, g_col, wts["b_gates"], wts["norm_mlstm"], batch, seq)
        hb = _moba_prompt(qb, kb, vb, batch, seq)
    else:
        c0, n0, m0, k_cache, v_cache, page_table = state["even"]
        ha, c1, n1, m1 = _mlstm_step(a_main, g_col, wts["b_gates"], wts["norm_mlstm"], c0, n0, m0)
        ha = ha.reshape(m, wa)
        _, sel = _moba_select(qb, k_cache, page_table)
        hb = _moba_step(qb, kb, vb, k_cache, v_cache, page_table, sel[:, :NH_B, :MOBA_TOPK])
    m1 = m1[:, :, 0]
    x = _outproj(ha, hb, wts["l0_out_a"], wts["l0_out_b"], x, tm, "outproj_l0")
    ffn = wts["ffn"]
    if prompt:
        x, fbuf0 = _ffn_prompt(x, wts["norm_ffn"][0], *ffn[0], wts["norm_final"], False, batch, seq,
                               min(tm, 256), "ffn_l0")
    else:
        x, fbuf0 = _ffn_step(x, wts["norm_ffn"][0], *ffn[0], wts["norm_final"], False, state["ffn"][0], "ffn_step_l0")

    z, xbc, qd, kd, vd, d_col = _inproj(
        x, wts["norm_mix"][1], tabs_d, wts["l1_w"], (0, 0, half_d, half_d, 0, 0), tm, "inproj_l1")
    ssd_w = wts["ssd"]
    if prompt:
        yc, h1, conv1 = _ssd_prompt(z, xbc, d_col, *ssd_w, batch, seq)
        parts = [_swa_prompt(qd, kd, vd, g, dil, batch, seq) for g, (_, dil) in enumerate(SWA_GROUPS)]
        od = _swa_merge(parts, tm)
    else:
        h0, conv0, caches = state["odd"]
        yc, h1 = _ssd_step(z, xbc, d_col, conv0, *ssd_w, h0)
        conv1 = jnp.concatenate([conv0[:, 1:], xbc[:, None, :]], axis=1)
        od = _swa_step(qd, kd, vd, caches)
    x = _outproj(yc, od, wts["l1_out_a"], wts["l1_out_b"], x, tm, "outproj_l1")
    if prompt:
        x, fbuf1 = _ffn_prompt(x, wts["norm_ffn"][1], *ffn[1], wts["norm_final"], True, batch, seq,
                               min(tm, 256), "ffn_l1")
    else:
        x, fbuf1 = _ffn_step(x, wts["norm_ffn"][1], *ffn[1], wts["norm_final"], True, state["ffn"][1], "ffn_step_l1")

    kr = kb.reshape(batch, seq, KVH_B, HD_B)
    vr = vb.reshape(batch, seq, KVH_B, HD_B)
    rows = []
    for g, (win, _) in enumerate(SWA_GROUPS):
        keep = min(win, seq) if prompt else seq
        kg = kd.reshape(batch, seq, -1)[:, seq - keep:, g * SWA_W:(g + 1) * SWA_W].reshape(batch, keep, HPG_D, HD_D)
        vg = vd.reshape(batch, seq, -1)[:, seq - keep:, g * SWA_W:(g + 1) * SWA_W].reshape(batch, keep, HPG_D, HD_D)
        rows.append(jnp.stack([kg, vg], axis=2))
    return (x.reshape(batch, seq, d), (c1, n1, m1, kr, vr), (h1, conv1, rows[0], rows[1], rows[2]),
            jnp.stack([fbuf0, fbuf1]))


def kernel(x_prompt, x_sample, state_l0_mlstm_c, state_l0_mlstm_n, state_l0_mlstm_m, cache_l0_moba_k, cache_l0_moba_v, state_l1_ssd_h, state_l1_ssd_conv, cache_l1_swa_kv0, cache_l1_swa_kv1, cache_l1_swa_kv2, state_ffn_conv, page_table, norm_mix, norm_ffn, norm_final, w_in_l0, b_gates_l0, norm_mlstm_l0, w_out_l0, w_in_l1, conv_w_l1, conv_b_l1, dt_bias_l1, a_log_l1, d_skip_l1, norm_ssd_l1, w_out_l1, ffn_up, ffn_conv_w, ffn_conv_b, ffn_down):
    wa = NH_A * DK_A
    qa, ka, va, oa, ia, fa, qb, kb, vb = _split_cols(
        w_in_l0, (wa, wa, wa, wa, NH_A, NH_A, NH_B * HD_B, KVH_B * HD_B, KVH_B * HD_B))
    wd = 3 * SWA_W
    z, xbc, dtr, qd, kd, vd = _split_cols(w_in_l1, (DI_C, CONV_DIM_C, NH_C, wd, wd, wd))
    bf = lambda a: a.astype(BF16)
    wts = {
        "norm_mix": norm_mix, "norm_ffn": norm_ffn, "norm_final": norm_final,
        "l0_w": [bf(jnp.concatenate([qa, ka, va, oa], axis=1)), bf(qb), bf(kb), bf(vb),
                 bf(_pad_lanes(jnp.concatenate([ia, fa], axis=1)))],
        "b_gates": b_gates_l0, "norm_mlstm": norm_mlstm_l0,
        "l0_out_a": bf(w_out_l0[:wa]), "l0_out_b": bf(w_out_l0[wa:]),
        "l1_w": [bf(z), bf(xbc), bf(qd), bf(kd), bf(vd), bf(_pad_lanes(dtr))],
        "ssd": (conv_w_l1, conv_b_l1, dt_bias_l1, a_log_l1, d_skip_l1, norm_ssd_l1),
        "l1_out_a": bf(w_out_l1[:DI_C]), "l1_out_b": bf(w_out_l1[DI_C:]),
        "ffn": [(bf(ffn_up[l]), ffn_conv_w[l], ffn_conv_b[l], bf(ffn_down[l])) for l in range(ffn_up.shape[0])],
    }
    n_seq, n_pages = page_table.shape
    past_len = n_pages * cache_l0_moba_k.shape[1]
    y_p, ev_p, od_p, ffn_p = _trunk(x_prompt, 0, None, wts, 512)
    state = {
        "even": (state_l0_mlstm_c, state_l0_mlstm_n, state_l0_mlstm_m, cache_l0_moba_k, cache_l0_moba_v, page_table),
        "odd": (state_l1_ssd_h, state_l1_ssd_conv, (cache_l1_swa_kv0, cache_l1_swa_kv1, cache_l1_swa_kv2)),
        "ffn": state_ffn_conv,
    }
    y_s, ev_s, od_s, ffn_s = _trunk(x_sample, past_len, state, wts, x_sample.shape[0] * x_sample.shape[1])
    c_p, n_p, m_p, k_p, v_p = ev_p
    c_s, n_s, m_s, k_s, v_s = ev_s
    h_p, conv_p, sw0_p, sw1_p, sw2_p = od_p
    h_s, conv_s, sw0_s, sw1_s, sw2_s = od_s
    return (y_p, y_s, c_p, c_s, n_p, n_s, m_p, m_s, k_p, k_s, v_p, v_s, h_p, h_s, conv_p, conv_s,
            sw0_p, sw0_s, sw1_p, sw1_s, sw2_p, sw2_s, ffn_p, ffn_s)
```

```python
import functools
import math

import jax
import jax.numpy as jnp
from jax import lax
from jax.experimental import pallas as pl
from jax.experimental.pallas import tpu as pltpu

F32 = jnp.float32
BF16 = jnp.bfloat16
HIGHEST = lax.Precision.HIGHEST

EPS = 1e-6
ROPE_THETA = 500000.0
ROPE_FRACTION = 4
LANES = 128
SUBLANES = 8
VMEM_LIMIT = 56 * 1024 * 1024

NH_A, DK_A, DV_A, CHUNK_A = 4, 128, 128, 128
NH_B, KVH_B, HD_B, MOBA_BLOCK, MOBA_TOPK = 4, 2, 128, 256, 3
NH_C, HD_C, NG_C, DS_C, CONV_C, CHUNK_C = 8, 64, 2, 128, 4, 128
DI_C = NH_C * HD_C
CONV_DIM_C = DI_C + 2 * NG_C * DS_C
SWA_GROUPS = ((128, 1), (512, 4), (2048, 16))
HPG_D, HD_D = 4, 64
SWA_W = HPG_D * HD_D
SWA_KEYS = 128
D_FF, FFN_CONV = 2816, 3
FFN_CW = 256

NN = (((1,), (0,)), ((), ()))
NT = (((1,), (1,)), ((), ()))
TN = (((0,), (0,)), ((), ()))


def _dot(a, b, dims=NN, precision=None):
    return lax.dot_general(a, b, dims, precision=precision, preferred_element_type=F32)


def _bdot(a, b, dims=NN):
    return _dot(a.astype(BF16), b.astype(BF16), dims)


def _params(sem, vmem=VMEM_LIMIT):
    return pltpu.CompilerParams(dimension_semantics=sem, vmem_limit_bytes=vmem)


def _rms(x, w):
    return x * lax.rsqrt(jnp.mean(x * x, axis=-1, keepdims=True) + EPS) * w


def _sigmoid(x):
    return 1.0 / (1.0 + jnp.exp(-x))


def _silu(x):
    return x * _sigmoid(x)


def _log_sigmoid(x):
    return jnp.minimum(x, 0.0) - jnp.log1p(jnp.exp(-jnp.abs(x)))


def _softplus(x):
    return jnp.maximum(x, 0.0) + jnp.log1p(jnp.exp(-jnp.abs(x)))


def _rope_lanes(y, cos, sa, sb, half):
    parts = []
    for g in range(y.shape[1] // LANES):
        yg = y[:, g * LANES:(g + 1) * LANES]
        parts.append(yg * cos + pltpu.roll(yg, LANES - half, axis=1) * sa + pltpu.roll(yg, half, axis=1) * sb)
    return parts


def _inproj_kernel(rope_half, x_ref, nw_ref, cos_ref, sa_ref, sb_ref, *refs):
    n = len(rope_half)
    w_refs, out_refs = refs[:n], refs[n:]
    xb = _rms(x_ref[...], nw_ref[...]).astype(BF16)
    for w_ref, o_ref, half in zip(w_refs, out_refs, rope_half):
        y = _dot(xb, w_ref[...])
        if half:
            parts = _rope_lanes(y, cos_ref[...], sa_ref[...], sb_ref[...], half)
            for g, p in enumerate(parts):
                o_ref[:, g * LANES:(g + 1) * LANES] = p
        else:
            o_ref[...] = y


def _inproj(x, nw, tabs, weights, rope_half, tm, name):
    m, d = x.shape
    cos, sa, sb = tabs
    trows = cos.shape[0]
    tb = trows // tm if trows >= tm else 1
    row = lambda i: (i, 0)
    const = lambda i: (0, 0)
    tab_spec = pl.BlockSpec((tm, LANES), lambda i: (i % tb, 0))
    in_specs = [pl.BlockSpec((tm, d), row), pl.BlockSpec((1, d), const), tab_spec, tab_spec, tab_spec]
    in_specs += [pl.BlockSpec(w.shape, const) for w in weights]
    out_shape = [jax.ShapeDtypeStruct((m, w.shape[1]), F32) for w in weights]
    out_specs = [pl.BlockSpec((tm, w.shape[1]), row) for w in weights]
    return pl.pallas_call(
        functools.partial(_inproj_kernel, tuple(rope_half)),
        grid=(m // tm,), in_specs=in_specs, out_specs=out_specs, out_shape=out_shape,
        compiler_params=_params(("arbitrary",)), name=name,
    )(x, nw.reshape(1, d), cos, sa, sb, *weights)


def _pad_lanes(w):
    return jnp.zeros((w.shape[0], LANES), w.dtype).at[:, :w.shape[1]].set(w)


def _rope_tables(pos, head_dim):
    rd = head_dim // ROPE_FRACTION
    half = rd // 2
    inv = ROPE_THETA ** (-jnp.arange(half, dtype=F32) / half)
    ang = pos.astype(F32)[:, None] * inv[None, :]
    cos, sin = jnp.cos(ang), jnp.sin(ang)
    ones = jnp.ones((pos.shape[0], head_dim - rd), F32)
    zeros = jnp.zeros((pos.shape[0], head_dim - rd), F32)
    zh = jnp.zeros_like(sin)
    reps = LANES // head_dim
    cos_t = jnp.tile(jnp.concatenate([cos, cos, ones], axis=1), (1, reps))
    sa_t = jnp.tile(jnp.concatenate([-sin, zh, zeros], axis=1), (1, reps))
    sb_t = jnp.tile(jnp.concatenate([zh, sin, zeros], axis=1), (1, reps))
    return (cos_t, sa_t, sb_t), half


def _mlstm_kernel(q_ref, k_ref, v_ref, o_ref, gc_ref, bc_ref, nw_ref,
                  h_ref, cout_ref, nout_ref, mout_ref, c_scr, n_scr, m_scr):
    ci = pl.program_id(1)
    ln = CHUNK_A

    @pl.when(ci == 0)
    def _():
        c_scr[...] = jnp.zeros_like(c_scr)
        n_scr[...] = jnp.zeros_like(n_scr)
        m_scr[...] = jnp.zeros_like(m_scr)

    row = lax.broadcasted_iota(jnp.int32, (ln, ln), 0)
    col = lax.broadcasted_iota(jnp.int32, (ln, ln), 1)
    tril = row >= col
    gc = gc_ref[...] + bc_ref[...]
    gr = jnp.transpose(gc)[0:2 * NH_A, :]
    fcum_c = _dot(tril.astype(F32), _log_sigmoid(gc), NN, HIGHEST)
    fcum_r = _dot(_log_sigmoid(gr), (row <= col).astype(F32), NN, HIGHEST)
    for h in range(NH_A):
        sl = slice(h * DK_A, (h + 1) * DK_A)
        ic_r, ic_c = gr[h:h + 1, :], gc[:, h:h + 1]
        fc_c, fc_r = fcum_c[:, NH_A + h:NH_A + h + 1], fcum_r[NH_A + h:NH_A + h + 1, :]
        m = m_scr[h:h + 1, 0:1]
        c = c_scr[h]
        nrow = n_scr[h:h + 1, :]
        qh = q_ref[:, sl]
        kh = k_ref[:, sl] * (DK_A ** -0.5)
        vh = v_ref[:, sl]
        dmat = jnp.where(tril, fc_c - fc_r + ic_r, -jnp.inf)
        inter = fc_c + m
        mt = jnp.maximum(inter, jnp.max(dmat, axis=1, keepdims=True))
        wmat = _bdot(qh, kh, NT) * jnp.exp(dmat - mt)
        a_inter = jnp.exp(inter - mt)
        num = _bdot(wmat, vh) + a_inter * _bdot(qh, c)
        den = jnp.sum(wmat, axis=1, keepdims=True) + a_inter * jnp.sum(qh * nrow, axis=1, keepdims=True)
        hh = num / jnp.maximum(jnp.abs(den), jnp.exp(-mt))
        f_end = fc_c[ln - 1:ln, :]
        m_new = jnp.maximum(f_end + m, jnp.max(f_end - fc_r + ic_r, axis=1, keepdims=True))
        kw = kh * jnp.exp(f_end - fc_c + ic_c - m_new)
        decay = jnp.exp(f_end + m - m_new)
        c_scr[h] = decay * c + _bdot(kw, vh, TN)
        n_scr[h:h + 1, :] = decay * nrow + jnp.sum(kw, axis=0, keepdims=True)
        m_scr[h:h + 1, :] = jnp.broadcast_to(m_new, (1, LANES))
        ha = _sigmoid(o_ref[:, sl]) * hh
        h_ref[:, sl] = _rms(ha, nw_ref[:, sl])

    @pl.when(ci == pl.num_programs(1) - 1)
    def _():
        cout_ref[0] = c_scr[...]
        nout_ref[0] = n_scr[0:NH_A, :]
        mout_ref[0] = m_scr[0:NH_A, :]


def _mlstm_prompt(a_main, g_col, b_gates, norm_w, batch, seq):
    m = a_main.shape[0]
    ln = CHUNK_A
    nc = seq // ln
    w = NH_A * DK_A
    blk = lambda j: pl.BlockSpec((ln, w), lambda b, c, j=j: (b * nc + c, j))
    bc = jnp.zeros((1, LANES), F32).at[0, :2 * NH_A].set(b_gates)
    const = lambda b, c: (0, 0)
    return pl.pallas_call(
        _mlstm_kernel, grid=(batch, nc),
        in_specs=[blk(0), blk(1), blk(2), blk(3),
                  pl.BlockSpec((ln, LANES), lambda b, c: (b * nc + c, 0)),
                  pl.BlockSpec((1, LANES), const), pl.BlockSpec((1, w), const)],
        out_specs=[pl.BlockSpec((ln, w), lambda b, c: (b * nc + c, 0)),
                   pl.BlockSpec((1, NH_A, DK_A, DV_A), lambda b, c: (b, 0, 0, 0)),
                   pl.BlockSpec((1, NH_A, DK_A), lambda b, c: (b, 0, 0)),
                   pl.BlockSpec((1, NH_A, LANES), lambda b, c: (b, 0, 0))],
        out_shape=[jax.ShapeDtypeStruct((m, w), F32),
                   jax.ShapeDtypeStruct((batch, NH_A, DK_A, DV_A), F32),
                   jax.ShapeDtypeStruct((batch, NH_A, DK_A), F32),
                   jax.ShapeDtypeStruct((batch, NH_A, LANES), F32)],
        scratch_shapes=[pltpu.VMEM((NH_A, DK_A, DV_A), F32), pltpu.VMEM((SUBLANES, LANES), F32),
                        pltpu.VMEM((SUBLANES, LANES), F32)],
        compiler_params=_params(("arbitrary", "arbitrary")), name="mlstm_prompt",
    )(a_main, a_main, a_main, a_main, g_col, bc, norm_w.reshape(1, w))


def _mlstm_step_kernel(a_ref, qc_ref, kc_ref, g_ref, b_ref, m_ref, c_ref, n_ref, nw_ref,
                       h_ref, cout_ref, nout_ref, mout_ref):
    g = g_ref[0] + b_ref[...]
    lf_all = _log_sigmoid(g)
    w = NH_A * DK_A
    scale = DK_A ** -0.5
    for h in range(NH_A):
        sl = slice(h * DK_A, (h + 1) * DK_A)
        q_row = a_ref[0, :, sl]
        k_row = a_ref[0, :, w + h * DK_A:w + (h + 1) * DK_A] * scale
        v_row = a_ref[0, :, 2 * w + h * DV_A:2 * w + (h + 1) * DV_A]
        o_row = a_ref[0, :, 3 * w + h * DV_A:3 * w + (h + 1) * DV_A]
        q_col = qc_ref[0, sl, :]
        k_col = kc_ref[0, sl, :] * scale
        c = c_ref[0, h]
        n_row = n_ref[0, h:h + 1, :]
        m = m_ref[0, :, h:h + 1]
        ic = g[:, h:h + 1]
        lf = lf_all[:, NH_A + h:NH_A + h + 1]
        inter = lf + m
        mt = jnp.maximum(inter, ic)
        wm = jnp.sum(q_row * k_row, axis=1, keepdims=True) * jnp.exp(ic - mt)
        a_inter = jnp.exp(inter - mt)
        num = wm * v_row + a_inter * jnp.sum(q_col * c, axis=0, keepdims=True)
        den = wm + a_inter * jnp.sum(q_row * n_row, axis=1, keepdims=True)
        hh = num / jnp.maximum(jnp.abs(den), jnp.exp(-mt))
        m_new = jnp.maximum(inter, ic)
        wgt = jnp.exp(ic - m_new)
        decay = jnp.exp(inter - m_new)
        cout_ref[0, h] = decay * c + (wgt * k_col) * v_row
        nout_ref[0, h:h + 1, :] = decay * n_row + wgt * k_row
        mout_ref[0, h:h + 1, :] = jnp.broadcast_to(m_new, (1, LANES))
        ha = _sigmoid(o_row) * hh
        h_ref[0, :, sl] = _rms(ha, nw_ref[:, sl])


def _mlstm_step(a_main, g_col, b_gates, norm_w, c0, n0, m0):
    ns = a_main.shape[0]
    w = NH_A * DK_A
    a3 = a_main.reshape(ns, 1, 4 * w)
    q_col = a_main[:, :w].reshape(ns, w, 1)
    k_col = a_main[:, w:2 * w].reshape(ns, w, 1)
    bc = jnp.zeros((1, LANES), F32).at[0, :2 * NH_A].set(b_gates)
    s3 = lambda s: (s, 0, 0)
    s4 = lambda s: (s, 0, 0, 0)
    const = lambda s: (0, 0)
    return pl.pallas_call(
        _mlstm_step_kernel, grid=(ns,),
        in_specs=[pl.BlockSpec((1, 1, 4 * w), s3), pl.BlockSpec((1, w, 1), s3), pl.BlockSpec((1, w, 1), s3),
                  pl.BlockSpec((1, 1, LANES), s3), pl.BlockSpec((1, LANES), const),
                  pl.BlockSpec((1, 1, NH_A), s3), pl.BlockSpec((1, NH_A, DK_A, DV_A), s4),
                  pl.BlockSpec((1, NH_A, DK_A), s3), pl.BlockSpec((1, w), const)],
        out_specs=[pl.BlockSpec((1, 1, w), s3), pl.BlockSpec((1, NH_A, DK_A, DV_A), s4),
                   pl.BlockSpec((1, NH_A, DK_A), s3), pl.BlockSpec((1, NH_A, LANES), s3)],
        out_shape=[jax.ShapeDtypeStruct((ns, 1, w), F32), jax.ShapeDtypeStruct((ns, NH_A, DK_A, DV_A), F32),
                   jax.ShapeDtypeStruct((ns, NH_A, DK_A), F32), jax.ShapeDtypeStruct((ns, NH_A, LANES), F32)],
        compiler_params=_params(("arbitrary",)), name="mlstm_step",
    )(a3, q_col, k_col, g_col.reshape(ns, 1, LANES), bc, m0.reshape(ns, 1, NH_A), c0, n0, norm_w.reshape(1, w))


def _top_blocks(bs, topk):
    blk_id = lax.broadcasted_iota(jnp.int32, bs.shape, 0).astype(F32)
    sel = jnp.zeros(bs.shape, F32)
    for _ in range(topk):
        mx = jnp.max(bs, axis=0, keepdims=True)
        idx = jnp.min(jnp.where(bs == mx, blk_id, float(bs.shape[0])), axis=0, keepdims=True)
        pick = (blk_id == idx) & (mx > -jnp.inf)
        sel = jnp.where(pick, 1.0, sel)
        bs = jnp.where(pick, -jnp.inf, bs)
    return sel


MOBA_ONES = 16


def _moba_kernel(n_full, q_ref, k_ref, v_ref, o_ref, kmean_scr, kb_scr, vt_scr, sel_scr, m_scr, acc_scr):
    qi = pl.program_id(2)
    blk = MOBA_BLOCK
    grp = NH_B // KVH_B

    @pl.when(qi == 0)
    def _():
        kmean_scr[...] = jnp.zeros_like(kmean_scr)
        for n in range(n_full):
            kblk = k_ref[n * blk:(n + 1) * blk, :]
            kmean_scr[n:n + 1, :] = jnp.mean(kblk, axis=0, keepdims=True)
            kb_scr[n] = kblk.astype(BF16)
            vt_scr[n, 0:HD_B, :] = jnp.transpose(v_ref[n * blk:(n + 1) * blk, :]).astype(BF16)
            vt_scr[n, HD_B:HD_B + MOBA_ONES, :] = jnp.ones((MOBA_ONES, blk), BF16)

    q = jnp.concatenate([q_ref[:, g * HD_B:(g + 1) * HD_B] for g in range(grp)], axis=0)
    rows = grp * blk
    bs = _dot(kmean_scr[...], q, NT, HIGHEST)
    bs = jnp.where(lax.broadcasted_iota(jnp.int32, bs.shape, 0) < qi, bs, -jnp.inf)
    sel_scr[...] = _top_blocks(bs, min(MOBA_TOPK, n_full))
    qs = (q * (HD_B ** -0.5 * math.log2(math.e))).astype(BF16)

    key = lax.broadcasted_iota(jnp.int32, (blk, rows), 0)
    qpos = lax.broadcasted_iota(jnp.int32, (blk, rows), 1) % blk
    causal = key <= qpos

    def attend(first, own_group, m_old, acc_old):
        scores = []
        for g in range(MOBA_GROUP):
            b = first + g
            ok = sel_scr[pl.ds(b, 1), :] > 0.0
            if own_group:
                ok = ((b < qi) & ok) | ((b == qi) & causal)
            scores.append(jnp.where(ok, _dot(kb_scr[b], qs, NT), -jnp.inf))
        m_new = m_old
        for s in scores:
            m_new = jnp.maximum(m_new, jnp.max(s, axis=0, keepdims=True))
        acc_new = jnp.exp2(m_old - m_new) * acc_old
        for g, s in enumerate(scores):
            p = jnp.exp2(s - m_new)
            acc_new = acc_new + _dot(vt_scr[first + g], p.astype(BF16))
        m_scr[...] = m_new
        acc_scr[...] = acc_new

    own_first = pl.multiple_of((qi // MOBA_GROUP) * MOBA_GROUP, MOBA_GROUP)
    attend(own_first, True, jnp.full((1, rows), -jnp.inf, F32), jnp.zeros(acc_scr.shape, F32))

    def body(i, carry):
        attend(pl.multiple_of(i * MOBA_GROUP, MOBA_GROUP), False, m_scr[...], acc_scr[...])
        return carry

    lax.fori_loop(0, qi // MOBA_GROUP, body, 0)
    out = acc_scr[0:HD_B, :] / acc_scr[HD_B:HD_B + 1, :]
    for g in range(grp):
        o_ref[:, g * HD_B:(g + 1) * HD_B] = jnp.transpose(out[:, g * blk:(g + 1) * blk])


MOBA_GROUP = 4


def _moba_prompt(qb, kb, vb, batch, seq):
    m = qb.shape[0]
    blk = MOBA_BLOCK
    nq = seq // blk
    assert seq % (blk * MOBA_GROUP) == 0
    grp = NH_B // KVH_B
    return pl.pallas_call(
        functools.partial(_moba_kernel, seq // blk), grid=(batch, KVH_B, nq),
        in_specs=[pl.BlockSpec((blk, grp * HD_B), lambda b, j, i: (b * nq + i, j)),
                  pl.BlockSpec((seq, HD_B), lambda b, j, i: (b, j)),
                  pl.BlockSpec((seq, HD_B), lambda b, j, i: (b, j))],
        out_specs=pl.BlockSpec((blk, grp * HD_B), lambda b, j, i: (b * nq + i, j)),
        out_shape=jax.ShapeDtypeStruct((m, NH_B * HD_B), F32),
        scratch_shapes=[pltpu.VMEM((-(-nq // SUBLANES) * SUBLANES, HD_B), F32),
                        pltpu.VMEM((nq, blk, HD_B), BF16), pltpu.VMEM((nq, HD_B + MOBA_ONES, blk), BF16),
                        pltpu.VMEM((-(-nq // SUBLANES) * SUBLANES, grp * blk), F32),
                        pltpu.VMEM((1, grp * blk), F32), pltpu.VMEM((HD_B + MOBA_ONES, grp * blk), F32)],
        compiler_params=_params(("arbitrary", "arbitrary", "arbitrary")), name="moba_prompt",
    )(qb, kb, vb)


MOBA_PAGES_PER_STEP = 32


def _moba_select_kernel(n_steps, pt_ref, q_ref, *refs):
    pages, (kmean_ref, sel_ref) = refs[:-2], refs[-2:]
    j = pl.program_id(1)
    page = pages[0].shape[1] // KVH_B

    def page_sum(p):
        heads = [jnp.sum(p[0, pl.ds(kv, page, stride=KVH_B), :], axis=0, keepdims=True) for kv in range(KVH_B)]
        return jnp.concatenate(heads, axis=1)

    sums = [page_sum(p) for p in pages]
    nblk = len(pages) // 2
    per_blk = [(sums[2 * i] + sums[2 * i + 1]) * (1.0 / MOBA_BLOCK) for i in range(nblk)]
    kmean_ref[0, pl.ds(pl.multiple_of(j * nblk, SUBLANES), nblk), :] = jnp.concatenate(per_blk, axis=0)

    @pl.when(j == n_steps - 1)
    def _():
        grp = NH_B // KVH_B
        nb = kmean_ref.shape[1]
        rowi = lax.broadcasted_iota(jnp.int32, (nb, 1), 0).astype(F32)
        orow = lax.broadcasted_iota(jnp.int32, (SUBLANES, LANES), 0)
        olane = lax.broadcasted_iota(jnp.int32, (SUBLANES, LANES), 1)
        out = jnp.zeros((SUBLANES, LANES), F32)
        for h in range(NH_B):
            kv = h // grp
            km = kmean_ref[0, :, kv * HD_B:(kv + 1) * HD_B]
            qh = q_ref[0, :, h * HD_B:(h + 1) * HD_B]
            bs = jnp.sum(km * qh, axis=1, keepdims=True)
            for r in range(MOBA_TOPK):
                mx = jnp.max(bs, axis=0, keepdims=True)
                idx = jnp.min(jnp.where(bs == mx, rowi, float(nb)), axis=0, keepdims=True)
                out = jnp.where((orow == h) & (olane == r), idx, out)
                bs = jnp.where(rowi == idx, -jnp.inf, bs)
        sel_ref[0] = out.astype(jnp.int32)


def _moba_select(qb, k_cache, page_table):
    ns, n_pages = page_table.shape
    page = k_cache.shape[1]
    pps = min(MOBA_PAGES_PER_STEP, n_pages)
    assert 2 * page == MOBA_BLOCK and n_pages % pps == 0 and pps % (2 * SUBLANES) == 0
    n_steps = n_pages // pps
    kw = KVH_B * HD_B
    kc = k_cache.reshape(k_cache.shape[0], page * KVH_B, HD_B)
    page_spec = lambda i: pl.BlockSpec(
        (1, page * KVH_B, HD_B), lambda s, j, pt, i=i: (pt[s * n_pages + j * pps + i], 0, 0))
    nb = n_pages * page // MOBA_BLOCK
    grid_spec = pltpu.PrefetchScalarGridSpec(
        num_scalar_prefetch=1, grid=(ns, n_steps),
        in_specs=[pl.BlockSpec((1, 1, NH_B * HD_B), lambda s, j, pt: (s, 0, 0))]
        + [page_spec(i) for i in range(pps)],
        out_specs=[pl.BlockSpec((1, nb, kw), lambda s, j, pt: (s, 0, 0)),
                   pl.BlockSpec((1, SUBLANES, LANES), lambda s, j, pt: (s, 0, 0))])
    return pl.pallas_call(
        functools.partial(_moba_select_kernel, n_steps), grid_spec=grid_spec,
        out_shape=[jax.ShapeDtypeStruct((ns, nb, kw), F32), jax.ShapeDtypeStruct((ns, SUBLANES, LANES), jnp.int32)],
        compiler_params=_params(("arbitrary", "arbitrary")), name="moba_select",
    )(page_table.reshape(-1), qb.reshape(ns, 1, NH_B * HD_B), *([kc] * pps))


def _moba_step_kernel(sel_ref, pt_ref, q_ref, kn_ref, vn_ref, *refs):
    pages, o_ref, (m_scr, l_scr, acc_scr) = refs[:4 * NH_B], refs[4 * NH_B], refs[4 * NH_B + 1:]
    r = pl.program_id(1)
    scale = HD_B ** -0.5
    grp = NH_B // KVH_B
    page = pages[0].shape[1] // KVH_B
    for h in range(NH_B):
        kv = h // grp
        row = slice(h, h + 1)
        q = q_ref[0, :, h * HD_B:(h + 1) * HD_B]
        ka_ref, kb_ref, va_ref, vb_ref = pages[4 * h:4 * h + 4]

        @pl.when(r == 0)
        def _():
            s0 = jnp.sum(q * kn_ref[0, :, kv * HD_B:(kv + 1) * HD_B], axis=1, keepdims=True) * scale
            m_scr[row, :] = jnp.broadcast_to(s0, (1, LANES))
            l_scr[row, :] = jnp.ones((1, LANES), F32)
            acc_scr[row, :] = vn_ref[0, :, kv * HD_B:(kv + 1) * HD_B]

        rows = pl.ds(kv, page, stride=KVH_B)
        kk = jnp.concatenate([ka_ref[0, rows, :], kb_ref[0, rows, :]], axis=0)
        vv = jnp.concatenate([va_ref[0, rows, :], vb_ref[0, rows, :]], axis=0)
        s = jnp.sum(kk * q, axis=1, keepdims=True) * scale
        m_old = m_scr[row, 0:1]
        m_new = jnp.maximum(m_old, jnp.max(s, axis=0, keepdims=True))
        alpha = jnp.exp(m_old - m_new)
        p = jnp.exp(s - m_new)
        m_scr[row, :] = jnp.broadcast_to(m_new, (1, LANES))
        l_scr[row, :] = alpha * l_scr[row, :] + jnp.sum(p, axis=0, keepdims=True)
        acc_scr[row, :] = alpha * acc_scr[row, :] + jnp.sum(p * vv, axis=0, keepdims=True)

        @pl.when(r == pl.num_programs(1) - 1)
        def _():
            o_ref[0, :, h * HD_B:(h + 1) * HD_B] = acc_scr[row, :] / l_scr[row, :]


def _moba_step(qb, k_new, v_new, k_cache, v_cache, page_table, sel):
    ns, n_pages = page_table.shape
    page = k_cache.shape[1]
    kw = KVH_B * HD_B
    kc = k_cache.reshape(k_cache.shape[0], page * KVH_B, HD_B)
    vc = v_cache.reshape(v_cache.shape[0], page * KVH_B, HD_B)

    def page_spec(h, half):
        def imap(s, r, sel_r, pt_r):
            blk = sel_r[(s * NH_B + h) * MOBA_TOPK + r]
            return (pt_r[s * n_pages + 2 * blk + half], 0, 0)
        return pl.BlockSpec((1, page * KVH_B, HD_B), imap)

    seq_row = lambda w: pl.BlockSpec((1, 1, w), lambda s, r, a, b: (s, 0, 0))
    page_specs, page_args = [], []
    for h in range(NH_B):
        page_specs += [page_spec(h, 0), page_spec(h, 1), page_spec(h, 0), page_spec(h, 1)]
        page_args += [kc, kc, vc, vc]
    grid_spec = pltpu.PrefetchScalarGridSpec(
        num_scalar_prefetch=2, grid=(ns, MOBA_TOPK),
        in_specs=[seq_row(NH_B * HD_B), seq_row(kw), seq_row(kw)] + page_specs,
        out_specs=seq_row(NH_B * HD_B),
        scratch_shapes=[pltpu.VMEM((SUBLANES, LANES), F32), pltpu.VMEM((SUBLANES, LANES), F32),
                        pltpu.VMEM((SUBLANES, HD_B), F32)])
    out = pl.pallas_call(
        _moba_step_kernel, grid_spec=grid_spec,
        out_shape=jax.ShapeDtypeStruct((ns, 1, NH_B * HD_B), F32),
        compiler_params=_params(("arbitrary", "arbitrary")), name="moba_step",
    )(sel.reshape(-1), page_table.reshape(-1), qb.reshape(ns, 1, NH_B * HD_B),
      k_new.reshape(ns, 1, kw), v_new.reshape(ns, 1, kw), *page_args)
    return out.reshape(ns, NH_B * HD_B)


def _outproj_kernel(a_ref, b_ref, wa_ref, wb_ref, x_ref, o_ref):
    o_ref[...] = x_ref[...] + _bdot(a_ref[...], wa_ref[...]) + _bdot(b_ref[...], wb_ref[...])


def _outproj(a, b, wa, wb, x, tm, name):
    m, d = x.shape
    row = lambda i: (i, 0)
    const = lambda i: (0, 0)
    return pl.pallas_call(
        _outproj_kernel, grid=(m // tm,),
        in_specs=[pl.BlockSpec((tm, a.shape[1]), row), pl.BlockSpec((tm, b.shape[1]), row),
                  pl.BlockSpec(wa.shape, const), pl.BlockSpec(wb.shape, const), pl.BlockSpec((tm, d), row)],
        out_specs=pl.BlockSpec((tm, d), row), out_shape=jax.ShapeDtypeStruct((m, d), F32),
        compiler_params=_params(("arbitrary",)), name=name,
    )(a, b, wa, wb, x)


FFN_PHASES = 4


def _ffn_kernel(final, x_ref, nw_ref, wup_ref, cw_ref, cb_ref, wdn_ref, nf_ref, o_ref, cs_ref, ext_scr, act_scr):
    tm = x_ref.shape[0]
    pad = SUBLANES
    rows = tm // FFN_PHASES
    nlane = FFN_CW // LANES
    gate_blocks = D_FF // LANES

    @pl.when(pl.program_id(1) == 0)
    def _():
        ext_scr[:, 0:pad, :] = jnp.zeros((ext_scr.shape[0], pad, LANES), F32)

    x = x_ref[...]
    xb = _rms(x, nw_ref[...]).astype(BF16)
    acc = x

    def up(c):
        for half, col0 in enumerate((c * FFN_CW, D_FF + c * FFN_CW)):
            u = _dot(xb, wup_ref[:, col0:col0 + FFN_CW])
            for i in range(nlane):
                ext_scr[half * gate_blocks + c * nlane + i, pad:pad + tm, :] = u[:, i * LANES:(i + 1) * LANES]

    def conv(k, p):
        cols = slice(k * LANES, (k + 1) * LANES)
        y = cb_ref[:, cols]
        for j in range(FFN_CONV):
            start = pad + p - (FFN_CONV - 1) + j
            y = y + cw_ref[j:j + 1, cols] * ext_scr[k, pl.ds(start, rows, stride=FFN_PHASES), :]
        return y

    nch = D_FF // FFN_CW
    up(0)
    for c in range(nch):
        if c + 1 < nch:
            up(c + 1)
        for i in range(nlane):
            kg = c * nlane + i
            kv = gate_blocks + kg
            for p in range(FFN_PHASES):
                act_scr[c % 2, i, pl.ds(p, rows, stride=FFN_PHASES), :] = _silu(conv(kg, p)) * conv(kv, p)
            for k in (kg, kv):
                ext_scr[k, 0:pad, :] = ext_scr[k, tm:tm + pad, :]
        act = jnp.concatenate([act_scr[c % 2, i] for i in range(nlane)], axis=1).astype(BF16)
        acc = acc + _dot(act, wdn_ref[c * FFN_CW:(c + 1) * FFN_CW, :])
    if final:
        acc = _rms(acc, nf_ref[...])
    o_ref[...] = acc
    for k in range(ext_scr.shape[0]):
        cs_ref[0, :, k * LANES:(k + 1) * LANES] = ext_scr[k, 0:pad, :]


def _ffn_prompt(x, nw, w_up, conv_w, conv_b, w_down, nf, final, batch, seq, tm, name):
    m, d = x.shape
    nt = seq // tm
    row = lambda b, i: (b * nt + i, 0)
    const = lambda b, i: (0, 0)
    out, cs = pl.pallas_call(
        functools.partial(_ffn_kernel, final), grid=(batch, nt),
        in_specs=[pl.BlockSpec((tm, d), row), pl.BlockSpec((1, d), const), pl.BlockSpec(w_up.shape, const),
                  pl.BlockSpec(conv_w.shape, const), pl.BlockSpec((1, 2 * D_FF), const),
                  pl.BlockSpec(w_down.shape, const), pl.BlockSpec((1, d), const)],
        out_specs=[pl.BlockSpec((tm, d), row), pl.BlockSpec((1, SUBLANES, 2 * D_FF), lambda b, i: (b, 0, 0))],
        out_shape=[jax.ShapeDtypeStruct((m, d), F32), jax.ShapeDtypeStruct((batch, SUBLANES, 2 * D_FF), F32)],
        scratch_shapes=[pltpu.VMEM((2 * D_FF // LANES, tm + SUBLANES, LANES), F32),
                        pltpu.VMEM((2, FFN_CW // LANES, tm, LANES), F32)],
        compiler_params=_params(("arbitrary", "arbitrary")), name=name,
    )(x, nw.reshape(1, d), w_up, conv_w, conv_b.reshape(1, -1), w_down, nf.reshape(1, d))
    return out, cs[:, SUBLANES - (FFN_CONV - 1):, :]


def _ffn_step_kernel(final, x_ref, nw_ref, wg_ref, wv_ref, g0_ref, g1_ref, v0_ref, v1_ref,
                     cwg_ref, cwv_ref, cbg_ref, cbv_ref, wdn_ref, nf_ref, o_ref, ug_ref, uv_ref, acc_scr):
    c = pl.program_id(0)

    @pl.when(c == 0)
    def _():
        acc_scr[...] = x_ref[...]

    xb = _rms(x_ref[...], nw_ref[...]).astype(BF16)
    ug = _dot(xb, wg_ref[...])
    uv = _dot(xb, wv_ref[...])
    ug_ref[...] = ug
    uv_ref[...] = uv
    gate = cbg_ref[...] + cwg_ref[0:1, :] * g0_ref[...] + cwg_ref[1:2, :] * g1_ref[...] + cwg_ref[2:3, :] * ug
    val = cbv_ref[...] + cwv_ref[0:1, :] * v0_ref[...] + cwv_ref[1:2, :] * v1_ref[...] + cwv_ref[2:3, :] * uv
    acc_scr[...] += _dot((_silu(gate) * val).astype(BF16), wdn_ref[...])

    @pl.when(c == pl.num_programs(0) - 1)
    def _():
        acc = acc_scr[...]
        o_ref[...] = _rms(acc, nf_ref[...]) if final else acc


def _ffn_step(x, nw, w_up, conv_w, conv_b, w_down, nf, final, buf, name):
    ns, d = x.shape
    cw = FFN_CW
    nch = D_FF // cw
    ntot = 2 * D_FF // cw
    bufw = buf.reshape(ns, 2 * 2 * D_FF)
    cb = conv_b.reshape(1, -1)
    const = lambda c: (0, 0)
    gcol = lambda c: (0, c)
    vcol = lambda c: (0, nch + c)
    out, u_g, u_v = pl.pallas_call(
        functools.partial(_ffn_step_kernel, final), grid=(nch,),
        in_specs=[pl.BlockSpec((ns, d), const), pl.BlockSpec((1, d), const),
                  pl.BlockSpec((d, cw), gcol), pl.BlockSpec((d, cw), vcol),
                  pl.BlockSpec((ns, cw), gcol), pl.BlockSpec((ns, cw), lambda c: (0, ntot + c)),
                  pl.BlockSpec((ns, cw), vcol), pl.BlockSpec((ns, cw), lambda c: (0, ntot + nch + c)),
                  pl.BlockSpec((FFN_CONV, cw), gcol), pl.BlockSpec((FFN_CONV, cw), vcol),
                  pl.BlockSpec((1, cw), gcol), pl.BlockSpec((1, cw), vcol),
                  pl.BlockSpec((cw, d), lambda c: (c, 0)), pl.BlockSpec((1, d), const)],
        out_specs=[pl.BlockSpec((ns, d), const), pl.BlockSpec((ns, cw), gcol), pl.BlockSpec((ns, cw), gcol)],
        out_shape=[jax.ShapeDtypeStruct((ns, d), F32), jax.ShapeDtypeStruct((ns, D_FF), F32),
                   jax.ShapeDtypeStruct((ns, D_FF), F32)],
        scratch_shapes=[pltpu.VMEM((ns, d), F32)],
        compiler_params=_params(("arbitrary",)), name=name,
    )(x, nw.reshape(1, d), w_up, w_up, bufw, bufw, bufw, bufw, conv_w, conv_w, cb, cb, w_down, nf.reshape(1, d))
    u = jnp.concatenate([u_g, u_v], axis=1)
    return out, jnp.stack([buf[:, 1], u], axis=1)


def _ssd_kernel(z_ref, xbc_ref, dc_ref, cw_ref, cb_ref, dbc_ref, alc_ref,
                dsk_ref, nw_ref, y_ref, hout_ref, cs_ref, h_scr, carry_scr, ext_scr, y_scr):
    ci = pl.program_id(1)
    ln = CHUNK_C
    pad = SUBLANES

    @pl.when(ci == 0)
    def _():
        h_scr[...] = jnp.zeros_like(h_scr)
        carry_scr[...] = jnp.zeros_like(carry_scr)

    raw = xbc_ref[...]
    ext_scr[0:pad, :] = carry_scr[...]
    ext_scr[pad:pad + ln, :] = raw
    carry_scr[...] = ext_scr[ln:ln + pad, :]
    y = cb_ref[...] + cw_ref[CONV_C - 1:CONV_C, :] * raw
    for j in range(CONV_C - 1):
        off = pad - (CONV_C - 1) + j
        y = y + cw_ref[j:j + 1, :] * ext_scr[off:off + ln, :]
    xbc = _silu(y)
    bw = NG_C * DS_C
    xs, bm, cm = xbc[:, :DI_C], xbc[:, DI_C:DI_C + bw], xbc[:, DI_C + bw:]

    row = lax.broadcasted_iota(jnp.int32, (ln, ln), 0)
    col = lax.broadcasted_iota(jnp.int32, (ln, ln), 1)
    tril = row >= col
    lane_lo = lax.broadcasted_iota(jnp.int32, (ln, LANES), 1) < HD_C
    row_lo = lax.broadcasted_iota(jnp.int32, (2 * HD_C, DS_C), 0) < HD_C
    dt_c = _softplus(dc_ref[...] + dbc_ref[...])
    ac_c = dt_c * (-jnp.exp(alc_ref[...]))
    dt_r = jnp.transpose(dt_c)[0:NH_C, :]
    ac_r = jnp.transpose(ac_c)[0:NH_C, :]
    acum_c = _dot(tril.astype(F32), ac_c, NN, HIGHEST)
    acum_r = _dot(ac_r, (row <= col).astype(F32), NN, HIGHEST)
    hpg = NH_C // NG_C
    for g in range(NG_C):
        bg = bm[:, g * DS_C:(g + 1) * DS_C]
        cg = cm[:, g * DS_C:(g + 1) * DS_C]
        cb_mat = _bdot(cg, bg, NT)
        for pi in range(hpg // 2):
            pair = g * (hpg // 2) + pi
            h0, h1 = 2 * pair, 2 * pair + 1
            xp = xs[:, pair * LANES:(pair + 1) * LANES]
            hp = h_scr[pair]
            a0, a1 = acum_c[:, h0:h0 + 1], acum_c[:, h1:h1 + 1]

            def mmat(h, a_col):
                decay = jnp.exp(jnp.where(tril, a_col - acum_r[h:h + 1, :], -jnp.inf))
                return cb_mat * decay * dt_r[h:h + 1, :]

            yy = _bdot(mmat(h0, a0), jnp.where(lane_lo, xp, 0.0)) + _bdot(mmat(h1, a1), jnp.where(lane_lo, 0.0, xp))
            yy = yy + jnp.where(lane_lo, jnp.exp(a0), jnp.exp(a1)) * _bdot(cg, hp, NT)
            e0, e1 = a0[ln - 1:ln, :], a1[ln - 1:ln, :]
            wend = jnp.where(lane_lo, jnp.exp(e0 - a0) * dt_c[:, h0:h0 + 1], jnp.exp(e1 - a1) * dt_c[:, h1:h1 + 1])
            h_scr[pair] = jnp.where(row_lo, jnp.exp(e0), jnp.exp(e1)) * hp + _bdot(xp * wend, bg, TN)
            y_scr[:, pair * LANES:(pair + 1) * LANES] = yy + dsk_ref[:, pair * LANES:(pair + 1) * LANES] * xp
    gated = y_scr[...] * _silu(z_ref[...])
    gw = DI_C // NG_C
    for g in range(NG_C):
        sl = slice(g * gw, (g + 1) * gw)
        y_ref[:, sl] = _rms(gated[:, sl], nw_ref[:, sl])
    cs_ref[0] = carry_scr[...]

    @pl.when(ci == pl.num_programs(1) - 1)
    def _():
        hout_ref[0] = h_scr[...]


def _head_lanes(v, width):
    return jnp.repeat(v.astype(F32), width).reshape(1, -1)


def _ssd_prompt(z, xbc, d_col, conv_w, conv_b, dt_bias, a_log, d_skip, norm_w, batch, seq):
    m = z.shape[0]
    ln = CHUNK_C
    nc = seq // ln
    npair = NH_C // 2
    lane8 = lambda v: jnp.zeros((1, LANES), F32).at[0, :NH_C].set(v)
    row = lambda b, c: (b * nc + c, 0)
    const = lambda b, c: (0, 0)
    y, h1, cs = pl.pallas_call(
        _ssd_kernel, grid=(batch, nc),
        in_specs=[pl.BlockSpec((ln, DI_C), row), pl.BlockSpec((ln, CONV_DIM_C), row),
                  pl.BlockSpec((ln, LANES), row),
                  pl.BlockSpec((CONV_C, CONV_DIM_C), const), pl.BlockSpec((1, CONV_DIM_C), const),
                  pl.BlockSpec((1, LANES), const), pl.BlockSpec((1, LANES), const),
                  pl.BlockSpec((1, DI_C), const), pl.BlockSpec((1, DI_C), const)],
        out_specs=[pl.BlockSpec((ln, DI_C), row),
                   pl.BlockSpec((1, npair, 2 * HD_C, DS_C), lambda b, c: (b, 0, 0, 0)),
                   pl.BlockSpec((1, SUBLANES, CONV_DIM_C), lambda b, c: (b, 0, 0))],
        out_shape=[jax.ShapeDtypeStruct((m, DI_C), F32),
                   jax.ShapeDtypeStruct((batch, npair, 2 * HD_C, DS_C), F32),
                   jax.ShapeDtypeStruct((batch, SUBLANES, CONV_DIM_C), F32)],
        scratch_shapes=[pltpu.VMEM((npair, 2 * HD_C, DS_C), F32), pltpu.VMEM((SUBLANES, CONV_DIM_C), F32),
                        pltpu.VMEM((ln + SUBLANES, CONV_DIM_C), F32), pltpu.VMEM((ln, DI_C), F32)],
        compiler_params=_params(("arbitrary", "arbitrary")), name="ssd_prompt",
    )(z, xbc, d_col, conv_w, conv_b.reshape(1, -1), lane8(dt_bias), lane8(a_log),
      _head_lanes(d_skip, HD_C), norm_w.reshape(1, DI_C))
    return y, h1.reshape(batch, NH_C, HD_C, DS_C), cs[:, SUBLANES - (CONV_C - 1):, :]


def _lane_bcast_col(row_vec):
    return jnp.transpose(jnp.broadcast_to(row_vec, (LANES, LANES)))


def _ssd_step_kernel(z_ref, x_ref, cs_ref, d_ref, cw_ref, cb_ref, db_ref, al_ref, dsk_ref, nw_ref, h_ref,
                     y_ref, hout_ref):
    y = cb_ref[...] + cw_ref[CONV_C - 1:CONV_C, :] * x_ref[0]
    for j in range(CONV_C - 1):
        y = y + cw_ref[j:j + 1, :] * cs_ref[0, j:j + 1, :]
    xbc = _silu(y)
    bw = NG_C * DS_C
    dt = _softplus(d_ref[0] + db_ref[...])
    ea = jnp.exp(dt * (-jnp.exp(al_ref[...])))
    row_lo = lax.broadcasted_iota(jnp.int32, (LANES, 1), 0) < HD_C
    lane_lo = lax.broadcasted_iota(jnp.int32, (1, LANES), 1) < HD_C
    hpg = NH_C // NG_C
    parts = []
    for pair in range(NH_C // 2):
        g = (2 * pair) // hpg
        h0, h1 = 2 * pair, 2 * pair + 1
        xp = xbc[:, pair * LANES:(pair + 1) * LANES]
        bg = xbc[:, DI_C + g * DS_C:DI_C + (g + 1) * DS_C]
        cg = xbc[:, DI_C + bw + g * DS_C:DI_C + bw + (g + 1) * DS_C]
        hp = h_ref[0, pair]
        x_col = _lane_bcast_col(xp)
        ea_col = jnp.where(row_lo, ea[:, h0:h0 + 1], ea[:, h1:h1 + 1])
        dt_col = jnp.where(row_lo, dt[:, h0:h0 + 1], dt[:, h1:h1 + 1])
        hout_ref[0, pair] = ea_col * hp + (dt_col * x_col) * bg
        y_inter = jnp.sum(hp * cg, axis=1, keepdims=True)
        y_col = ea_col * y_inter
        y_row = jnp.transpose(jnp.broadcast_to(y_col, (LANES, LANES)))[0:1, :]
        dt_row = jnp.where(lane_lo, dt[:, h0:h0 + 1], dt[:, h1:h1 + 1])
        cb = jnp.sum(cg * bg, axis=1, keepdims=True)
        parts.append(y_row + (cb * dt_row) * xp + dsk_ref[:, pair * LANES:(pair + 1) * LANES] * xp)
    gated = jnp.concatenate(parts, axis=1) * _silu(z_ref[0])
    gw = DI_C // NG_C
    for g in range(NG_C):
        sl = slice(g * gw, (g + 1) * gw)
        y_ref[0, :, sl] = _rms(gated[:, sl], nw_ref[:, sl])


def _ssd_step(z, xbc, d_col, conv_state, conv_w, conv_b, dt_bias, a_log, d_skip, norm_w, h0):
    ns = z.shape[0]
    npair = NH_C // 2
    lane8 = lambda v: jnp.zeros((1, LANES), F32).at[0, :NH_C].set(v)
    s3 = lambda s: (s, 0, 0)
    s4 = lambda s: (s, 0, 0, 0)
    const = lambda s: (0, 0)
    y, h1 = pl.pallas_call(
        _ssd_step_kernel, grid=(ns,),
        in_specs=[pl.BlockSpec((1, 1, DI_C), s3), pl.BlockSpec((1, 1, CONV_DIM_C), s3),
                  pl.BlockSpec((1, CONV_C - 1, CONV_DIM_C), s3), pl.BlockSpec((1, 1, LANES), s3),
                  pl.BlockSpec((CONV_C, CONV_DIM_C), const), pl.BlockSpec((1, CONV_DIM_C), const),
                  pl.BlockSpec((1, LANES), const), pl.BlockSpec((1, LANES), const),
                  pl.BlockSpec((1, DI_C), const), pl.BlockSpec((1, DI_C), const),
                  pl.BlockSpec((1, npair, 2 * HD_C, DS_C), s4)],
        out_specs=[pl.BlockSpec((1, 1, DI_C), s3), pl.BlockSpec((1, npair, 2 * HD_C, DS_C), s4)],
        out_shape=[jax.ShapeDtypeStruct((ns, 1, DI_C), F32), jax.ShapeDtypeStruct((ns, npair, 2 * HD_C, DS_C), F32)],
        compiler_params=_params(("arbitrary",)), name="ssd_step",
    )(z.reshape(ns, 1, DI_C), xbc.reshape(ns, 1, CONV_DIM_C), conv_state, d_col.reshape(ns, 1, LANES),
      conv_w, conv_b.reshape(1, -1), lane8(dt_bias), lane8(a_log), _head_lanes(d_skip, HD_C),
      norm_w.reshape(1, DI_C), h0.reshape(ns, npair, 2 * HD_C, DS_C))
    return y.reshape(ns, DI_C), h1.reshape(ns, NH_C, HD_C, DS_C)


def _head_mask(width):
    lane = lax.broadcasted_iota(jnp.int32, (1, width), 1)
    return [(lane >= h * HD_D) & (lane < (h + 1) * HD_D) for h in range(width // HD_D)]


SWA_TILE = 2048


def _swa_kernel(dil, q_ref, kt_ref, kc_ref, vt_ref, vc_ref, acc_ref, m_ref, l_ref, kbuf, vbuf):
    t = pl.program_id(1)
    nk = SWA_KEYS
    tail = nk * dil
    tile = q_ref.shape[0]
    kbuf[0:tail, :] = kt_ref[...]
    kbuf[tail:tail + tile, :] = kc_ref[...]
    vbuf[0:tail, :] = vt_ref[...]
    vbuf[tail:tail + tile, :] = vc_ref[...]
    qi = lax.broadcasted_iota(jnp.int32, (nk, 2 * nk), 0)
    kj = lax.broadcasted_iota(jnp.int32, (nk, 2 * nk), 1)
    dist = nk + qi - kj
    band = (dist >= 0) & (dist <= nk)
    heads = _head_mask(LANES)

    for j in range(tile // tail):
        first = j * tail
        ok = band if j > 0 else band & ((t > 0) | (kj >= nk))
        q_win = q_ref.at[pl.ds(first, tail), :]
        k_win = kbuf.at[pl.ds(first, 2 * tail), :]
        v_win = vbuf.at[pl.ds(first, 2 * tail), :]
        outs = [o.at[pl.ds(first, tail), :] for o in (acc_ref, m_ref, l_ref)]
        for r in range(dil):
            q = q_win[pl.ds(r, nk, stride=dil), :] * (HD_D ** -0.5)
            kk = k_win[pl.ds(r, 2 * nk, stride=dil), :].astype(BF16)
            vv = v_win[pl.ds(r, 2 * nk, stride=dil), :].astype(BF16)
            acc = jnp.zeros((nk, LANES), F32)
            mm = jnp.zeros((nk, LANES), F32)
            ll = jnp.zeros((nk, LANES), F32)
            for hm in heads:
                s = _dot(jnp.where(hm, q, 0.0).astype(BF16), kk, NT)
                s = jnp.where(ok, s, -jnp.inf)
                mx = jnp.max(s, axis=1, keepdims=True)
                p = jnp.exp(s - mx)
                acc = jnp.where(hm, _dot(p.astype(BF16), vv), acc)
                mm = jnp.where(hm, mx, mm)
                ll = jnp.where(hm, jnp.sum(p, axis=1, keepdims=True), ll)
            for o, val in zip(outs, (acc, mm, ll)):
                o[pl.ds(r, nk, stride=dil), :] = val


def _swa_prompt(qd, kd, vd, g, dil, batch, seq):
    m = qd.shape[0]
    tile = min(SWA_TILE, seq)
    tail = SWA_KEYS * dil
    assert seq % tile == 0 and tile % tail == 0
    nt = seq // tile
    per = tile // tail
    halves = SWA_W // LANES
    cur = pl.BlockSpec((tile, LANES), lambda b, t, f: (b * nt + t, g * halves + f))
    prev = pl.BlockSpec((tail, LANES), lambda b, t, f: (jnp.maximum((b * nt + t) * per - 1, 0), g * halves + f))
    ospec = pl.BlockSpec((tile, LANES), lambda b, t, f: (b * nt + t, f))
    oshape = jax.ShapeDtypeStruct((m, SWA_W), F32)
    return pl.pallas_call(
        functools.partial(_swa_kernel, dil), grid=(batch, nt, halves),
        in_specs=[cur, prev, cur, prev, cur], out_specs=[ospec] * 3, out_shape=[oshape] * 3,
        scratch_shapes=[pltpu.VMEM((tail + tile, LANES), F32), pltpu.VMEM((tail + tile, LANES), F32)],
        compiler_params=_params(("arbitrary", "arbitrary", "arbitrary")), name=f"swa_prompt_{g}",
    )(qd, kd, kd, vd, vd)


def _swa_step_kernel(q_ref, kn_ref, vn_ref, c0_ref, c1_ref, c2_ref, o_ref):
    scale = HD_D ** -0.5
    ngrp = len(SWA_GROUPS)
    stats = []
    for g, (c_ref, (win, dil)) in enumerate(zip((c0_ref, c1_ref, c2_ref), SWA_GROUPS)):
        on_stride = lax.broadcasted_iota(jnp.int32, (1, win), 1) % dil == 0
        per_head = []
        for h in range(HPG_D):
            sl = slice((g * HPG_D + h) * HD_D, (g * HPG_D + h + 1) * HD_D)
            q = q_ref[0, sl, :] * scale
            s = jnp.sum(c_ref[0, 0, h] * q, axis=0, keepdims=True)
            s = jnp.where(on_stride, s, -jnp.inf)
            sn = jnp.sum(kn_ref[0, sl, :] * q, axis=0, keepdims=True)
            mx = jnp.maximum(jnp.max(s, axis=1, keepdims=True), sn)
            p = jnp.exp(s - mx)
            pn = jnp.exp(sn - mx)
            den = jnp.sum(p, axis=1, keepdims=True) + pn
            acc = jnp.sum(c_ref[0, 1, h] * p, axis=1, keepdims=True) + pn * vn_ref[0, sl, :]
            per_head.append((acc, mx, den))
        stats.append(per_head)
    for h in range(HPG_D):
        mx = functools.reduce(jnp.maximum, [stats[g][h][1] for g in range(ngrp)])
        wgt = [jnp.exp(stats[g][h][1] - mx) for g in range(ngrp)]
        num = functools.reduce(jnp.add, [wgt[g] * stats[g][h][0] for g in range(ngrp)])
        den = functools.reduce(jnp.add, [wgt[g] * stats[g][h][2] for g in range(ngrp)])
        o_ref[0, h * HD_D:(h + 1) * HD_D, :] = num / den


def _swa_step(qd, kd, vd, caches):
    ns = qd.shape[0]
    w3 = qd.shape[1]
    views, cspecs = [], []
    for (win, dil), cache in zip(SWA_GROUPS, caches):
        assert cache.shape[1] == win and win // dil == SWA_KEYS
        views.append(jnp.transpose(cache, (0, 2, 3, 4, 1)))
        cspecs.append(pl.BlockSpec((1, 2, HPG_D, HD_D, win), lambda s: (s, 0, 0, 0, 0)))
    s3 = lambda s: (s, 0, 0)
    col = pl.BlockSpec((1, w3, 1), s3)
    out = pl.pallas_call(
        _swa_step_kernel, grid=(ns,), in_specs=[col, col, col] + cspecs,
        out_specs=pl.BlockSpec((1, SWA_W, 1), s3), out_shape=jax.ShapeDtypeStruct((ns, SWA_W, 1), F32),
        compiler_params=_params(("arbitrary",)), name="swa_step",
    )(qd.reshape(ns, w3, 1), kd.reshape(ns, w3, 1), vd.reshape(ns, w3, 1), *views)
    return out.reshape(ns, SWA_W)


def _swa_merge_kernel(*refs):
    ins, o_ref = refs[:-1], refs[-1]
    ng = len(ins) // 3
    mx = ins[1][...]
    for g in range(1, ng):
        mx = jnp.maximum(mx, ins[3 * g + 1][...])
    num = jnp.zeros_like(mx)
    den = jnp.zeros_like(mx)
    for g in range(ng):
        wgt = jnp.exp(ins[3 * g + 1][...] - mx)
        num = num + wgt * ins[3 * g][...]
        den = den + wgt * ins[3 * g + 2][...]
    o_ref[...] = num / den


def _swa_merge(parts, tm):
    flat = [a for grp in parts for a in grp]
    m = flat[0].shape[0]
    spec = pl.BlockSpec((tm, SWA_W), lambda i: (i, 0))
    return pl.pallas_call(
        _swa_merge_kernel, grid=(m // tm,), in_specs=[spec] * len(flat), out_specs=spec,
        out_shape=jax.ShapeDtypeStruct((m, SWA_W), F32),
        compiler_params=_params(("arbitrary",)), name="swa_merge",
    )(*flat)


def _split_cols(w, sizes):
    out, start = [], 0
    for s in sizes:
        out.append(w[:, start:start + s])
        start += s
    return out


def _trunk(x3, pos0, state, wts, tm):
    batch, seq, d = x3.shape
    m = batch * seq
    x = x3.reshape(m, d)
    pos = pos0 + jnp.arange(seq, dtype=jnp.int32)
    prompt = state is None
    if not prompt:
        pos = jnp.broadcast_to(pos, (m,))
    tabs_b, half_b = _rope_tables(pos, HD_B)
    tabs_d, half_d = _rope_tables(pos, HD_D)
    wa = NH_A * DK_A

    a_main, qb, kb, vb, g_col = _inproj(
        x, wts["norm_mix"][0], tabs_b, wts["l0_w"], (0, half_b, half_b, 0, 0), tm, "inproj_l0")
    if prompt:
        ha, c1, n1, m1 = _mlstm_prompt(a_main, g_col, wts["b_gates"], wts["norm_mlstm"], batch, seq)
        hb = _moba_prompt(qb, kb, vb, batch, seq)
    else:
        c0, n0, m0, k_cache, v_cache, page_table = state["even"]
        ha, c1, n1, m1 = _mlstm_step(a_main, g_col, wts["b_gates"], wts["norm_mlstm"], c0, n0, m0)
        ha = ha.reshape(m, wa)
        _, sel = _moba_select(qb, k_cache, page_table)
        hb = _moba_step(qb, kb, vb, k_cache, v_cache, page_table, sel[:, :NH_B, :MOBA_TOPK])
    m1 = m1[:, :, 0]
    x = _outproj(ha, hb, wts["l0_out_a"], wts["l0_out_b"], x, tm, "outproj_l0")
    ffn = wts["ffn"]
    if prompt:
        x, fbuf0 = _ffn_prompt(x, wts["norm_ffn"][0], *ffn[0], wts["norm_final"], False, batch, seq,
                               min(tm, 256), "ffn_l0")
    else:
        x, fbuf0 = _ffn_step(x, wts["norm_ffn"][0], *ffn[0], wts["norm_final"], False, state["ffn"][0], "ffn_step_l0")

    z, xbc, qd, kd, vd, d_col = _inproj(
        x, wts["norm_mix"][1], tabs_d, wts["l1_w"], (0, 0, half_d, half_d, 0, 0), tm, "inproj_l1")
    ssd_w = wts["ssd"]
    if prompt:
        yc, h1, conv1 = _ssd_prompt(z, xbc, d_col, *ssd_w, batch, seq)
        parts = [_swa_prompt(qd, kd, vd, g, dil, batch, seq) for g, (_, dil) in enumerate(SWA_GROUPS)]
        od = _swa_merge(parts, tm)
    else:
        h0, conv0, caches = state["odd"]
        yc, h1 = _ssd_step(z, xbc, d_col, conv0, *ssd_w, h0)
        conv1 = jnp.concatenate([conv0[:, 1:], xbc[:, None, :]], axis=1)
        od = _swa_step(qd, kd, vd, caches)
    x = _outproj(yc, od, wts["l1_out_a"], wts["l1_out_b"], x, tm, "outproj_l1")
    if prompt:
        x, fbuf1 = _ffn_prompt(x, wts["norm_ffn"][1], *ffn[1], wts["norm_final"], True, batch, seq,
                               min(tm, 256), "ffn_l1")
    else:
        x, fbuf1 = _ffn_step(x, wts["norm_ffn"][1], *ffn[1], wts["norm_final"], True, state["ffn"][1], "ffn_step_l1")

    kr = kb.reshape(batch, seq, KVH_B, HD_B)
    vr = vb.reshape(batch, seq, KVH_B, HD_B)
    rows = []
    for g, (win, _) in enumerate(SWA_GROUPS):
        keep = min(win, seq) if prompt else seq
        kg = kd.reshape(batch, seq, -1)[:, seq - keep:, g * SWA_W:(g + 1) * SWA_W].reshape(batch, keep, HPG_D, HD_D)
        vg = vd.reshape(batch, seq, -1)[:, seq - keep:, g * SWA_W:(g + 1) * SWA_W].reshape(batch, keep, HPG_D, HD_D)
        rows.append(jnp.stack([kg, vg], axis=2))
    return (x.reshape(batch, seq, d), (c1, n1, m1, kr, vr), (h1, conv1, rows[0], rows[1], rows[2]),
            jnp.stack([fbuf0, fbuf1]))


def kernel(x_prompt, x_sample, state_l0_mlstm_c, state_l0_mlstm_n, state_l0_mlstm_m, cache_l0_moba_k, cache_l0_moba_v, state_l1_ssd_h, state_l1_ssd_conv, cache_l1_swa_kv0, cache_l1_swa_kv1, cache_l1_swa_kv2, state_ffn_conv, page_table, norm_mix, norm_ffn, norm_final, w_in_l0, b_gates_l0, norm_mlstm_l0, w_out_l0, w_in_l1, conv_w_l1, conv_b_l1, dt_bias_l1, a_log_l1, d_skip_l1, norm_ssd_l1, w_out_l1, ffn_up, ffn_conv_w, ffn_conv_b, ffn_down):
    wa = NH_A * DK_A
    qa, ka, va, oa, ia, fa, qb, kb, vb = _split_cols(
        w_in_l0, (wa, wa, wa, wa, NH_A, NH_A, NH_B * HD_B, KVH_B * HD_B, KVH_B * HD_B))
    wd = 3 * SWA_W
    z, xbc, dtr, qd, kd, vd = _split_cols(w_in_l1, (DI_C, CONV_DIM_C, NH_C, wd, wd, wd))
    bf = lambda a: a.astype(BF16)
    wts = {
        "norm_mix": norm_mix, "norm_ffn": norm_ffn, "norm_final": norm_final,
        "l0_w": [bf(jnp.concatenate([qa, ka, va, oa], axis=1)), bf(qb), bf(kb), bf(vb),
                 bf(_pad_lanes(jnp.concatenate([ia, fa], axis=1)))],
        "b_gates": b_gates_l0, "norm_mlstm": norm_mlstm_l0,
        "l0_out_a": bf(w_out_l0[:wa]), "l0_out_b": bf(w_out_l0[wa:]),
        "l1_w": [bf(z), bf(xbc), bf(qd), bf(kd), bf(vd), bf(_pad_lanes(dtr))],
        "ssd": (conv_w_l1, conv_b_l1, dt_bias_l1, a_log_l1, d_skip_l1, norm_ssd_l1),
        "l1_out_a": bf(w_out_l1[:DI_C]), "l1_out_b": bf(w_out_l1[DI_C:]),
        "ffn": [(bf(ffn_up[l]), ffn_conv_w[l], ffn_conv_b[l], bf(ffn_down[l])) for l in range(ffn_up.shape[0])],
    }
    n_seq, n_pages = page_table.shape
    past_len = n_pages * cache_l0_moba_k.shape[1]
    y_p, ev_p, od_p, ffn_p = _trunk(x_prompt, 0, None, wts, 512)
    state = {
        "even": (state_l0_mlstm_c, state_l0_mlstm_n, state_l0_mlstm_m, cache_l0_moba_k, cache_l0_moba_v, page_table),
        "odd": (state_l1_ssd_h, state_l1_ssd_conv, (cache_l1_swa_kv0, cache_l1_swa_kv1, cache_l1_swa_kv2)),
        "ffn": state_ffn_conv,
    }
    y_s, ev_s, od_s, ffn_s = _trunk(x_sample, past_len, state, wts, x_sample.shape[0] * x_sample.shape[1])
    c_p, n_p, m_p, k_p, v_p = ev_p
    c_s, n_s, m_s, k_s, v_s = ev_s
    h_p, conv_p, sw0_p, sw1_p, sw2_p = od_p
    h_s, conv_s, sw0_s, sw1_s, sw2_s = od_s
    return (y_p, y_s, c_p, c_s, n_p, n_s, m_p, m_s, k_p, k_s, v_p, v_s, h_p, h_s, conv_p, conv_s,
            sw0_p, sw0_s, sw1_p, sw1_s, sw2_p, sw2_s, ffn_p, ffn_s)
```

```python
import functools
import math

import jax
import jax.numpy as jnp
from jax import lax
from jax.experimental import pallas as pl
from jax.experimental.pallas import tpu as pltpu

F32 = jnp.float32
BF16 = jnp.bfloat16
HIGHEST = lax.Precision.HIGHEST

EPS = 1e-6
ROPE_THETA = 500000.0
ROPE_FRACTION = 4
LANES = 128
SUBLANES = 8
VMEM_LIMIT = 56 * 1024 * 1024

NH_A, DK_A, DV_A, CHUNK_A = 4, 128, 128, 128
NH_B, KVH_B, HD_B, MOBA_BLOCK, MOBA_TOPK = 4, 2, 128, 256, 3
NH_C, HD_C, NG_C, DS_C, CONV_C, CHUNK_C = 8, 64, 2, 128, 4, 128
DI_C = NH_C * HD_C
CONV_DIM_C = DI_C + 2 * NG_C * DS_C
SWA_GROUPS = ((128, 1), (512, 4), (2048, 16))
HPG_D, HD_D = 4, 64
SWA_W = HPG_D * HD_D
SWA_KEYS = 128
D_FF, FFN_CONV = 2816, 3
FFN_CW = 256

NN = (((1,), (0,)), ((), ()))
NT = (((1,), (1,)), ((), ()))
TN = (((0,), (0,)), ((), ()))


def _dot(a, b, dims=NN, precision=None):
    return lax.dot_general(a, b, dims, precision=precision, preferred_element_type=F32)


def _bdot(a, b, dims=NN):
    return _dot(a.astype(BF16), b.astype(BF16), dims)


def _params(sem, vmem=VMEM_LIMIT):
    return pltpu.CompilerParams(dimension_semantics=sem, vmem_limit_bytes=vmem)


def _rms(x, w):
    return x * lax.rsqrt(jnp.mean(x * x, axis=-1, keepdims=True) + EPS) * w


def _sigmoid(x):
    return 1.0 / (1.0 + jnp.exp(-x))


def _silu(x):
    return x * _sigmoid(x)


def _log_sigmoid(x):
    return jnp.minimum(x, 0.0) - jnp.log1p(jnp.exp(-jnp.abs(x)))


def _softplus(x):
    return jnp.maximum(x, 0.0) + jnp.log1p(jnp.exp(-jnp.abs(x)))


def _rope_lanes(y, cos, sa, sb, half):
    parts = []
    for g in range(y.shape[1] // LANES):
        yg = y[:, g * LANES:(g + 1) * LANES]
        parts.append(yg * cos + pltpu.roll(yg, LANES - half, axis=1) * sa + pltpu.roll(yg, half, axis=1) * sb)
    return parts


def _inproj_kernel(rope_half, x_ref, nw_ref, cos_ref, sa_ref, sb_ref, *refs):
    n = len(rope_half)
    w_refs, out_refs = refs[:n], refs[n:]
    xb = _rms(x_ref[...], nw_ref[...]).astype(BF16)
    for w_ref, o_ref, half in zip(w_refs, out_refs, rope_half):
        y = _dot(xb, w_ref[...])
        if half:
            parts = _rope_lanes(y, cos_ref[...], sa_ref[...], sb_ref[...], half)
            for g, p in enumerate(parts):
                o_ref[:, g * LANES:(g + 1) * LANES] = p
        else:
            o_ref[...] = y


def _inproj(x, nw, tabs, weights, rope_half, tm, name):
    m, d = x.shape
    cos, sa, sb = tabs
    trows = cos.shape[0]
    tb = trows // tm if trows >= tm else 1
    row = lambda i: (i, 0)
    const = lambda i: (0, 0)
    tab_spec = pl.BlockSpec((tm, LANES), lambda i: (i % tb, 0))
    in_specs = [pl.BlockSpec((tm, d), row), pl.BlockSpec((1, d), const), tab_spec, tab_spec, tab_spec]
    in_specs += [pl.BlockSpec(w.shape, const) for w in weights]
    out_shape = [jax.ShapeDtypeStruct((m, w.shape[1]), F32) for w in weights]
    out_specs = [pl.BlockSpec((tm, w.shape[1]), row) for w in weights]
    return pl.pallas_call(
        functools.partial(_inproj_kernel, tuple(rope_half)),
        grid=(m // tm,), in_specs=in_specs, out_specs=out_specs, out_shape=out_shape,
        compiler_params=_params(("arbitrary",)), name=name,
    )(x, nw.reshape(1, d), cos, sa, sb, *weights)


def _pad_lanes(w):
    return jnp.zeros((w.shape[0], LANES), w.dtype).at[:, :w.shape[1]].set(w)


def _rope_tables(pos, head_dim):
    rd = head_dim // ROPE_FRACTION
    half = rd // 2
    inv = ROPE_THETA ** (-jnp.arange(half, dtype=F32) / half)
    ang = pos.astype(F32)[:, None] * inv[None, :]
    cos, sin = jnp.cos(ang), jnp.sin(ang)
    ones = jnp.ones((pos.shape[0], head_dim - rd), F32)
    zeros = jnp.zeros((pos.shape[0], head_dim - rd), F32)
    zh = jnp.zeros_like(sin)
    reps = LANES // head_dim
    cos_t = jnp.tile(jnp.concatenate([cos, cos, ones], axis=1), (1, reps))
    sa_t = jnp.tile(jnp.concatenate([-sin, zh, zeros], axis=1), (1, reps))
    sb_t = jnp.tile(jnp.concatenate([zh, sin, zeros], axis=1), (1, reps))
    return (cos_t, sa_t, sb_t), half


def _mlstm_kernel(q_ref, k_ref, v_ref, o_ref, gc_ref, bc_ref, nw_ref,
                  h_ref, cout_ref, nout_ref, mout_ref, c_scr, n_scr, m_scr):
    ci = pl.program_id(0)
    ln = CHUNK_A

    @pl.when(ci == 0)
    def _():
        c_scr[...] = jnp.zeros_like(c_scr)
        n_scr[...] = jnp.zeros_like(n_scr)
        m_scr[...] = jnp.zeros_like(m_scr)

    row = lax.broadcasted_iota(jnp.int32, (ln, ln), 0)
    col = lax.broadcasted_iota(jnp.int32, (ln, ln), 1)
    tril = row >= col
    heads = [(bi, h) for bi in range(q_ref.shape[0]) for h in range(NH_A)]
    gates = []
    for bi in range(q_ref.shape[0]):
        gc = gc_ref[bi] + bc_ref[...]
        gr = jnp.transpose(gc)[0:2 * NH_A, :]
        fcum_c = _dot(tril.astype(F32), _log_sigmoid(gc), NN, HIGHEST)
        fcum_r = _dot(_log_sigmoid(gr), (row <= col).astype(F32), NN, HIGHEST)
        gates.append((gc, gr, fcum_c, fcum_r))
    staged = []
    for bi, h in heads:
        gc, gr, fcum_c, fcum_r = gates[bi]
        sl = slice(h * DK_A, (h + 1) * DK_A)
        hs = bi * NH_A + h
        ic_r, ic_c = gr[h:h + 1, :], gc[:, h:h + 1]
        fc_c, fc_r = fcum_c[:, NH_A + h:NH_A + h + 1], fcum_r[NH_A + h:NH_A + h + 1, :]
        m = m_scr[hs:hs + 1, 0:1]
        c = c_scr[hs]
        nrow = n_scr[hs:hs + 1, :]
        qh = q_ref[bi, :, sl]
        kh = k_ref[bi, :, sl] * (DK_A ** -0.5)
        vh = v_ref[bi, :, sl]
        s = _bdot(qh, kh, NT)
        qc = _bdot(qh, c)
        qn = jnp.sum(qh * nrow, axis=1, keepdims=True)
        f_end = fc_c[ln - 1:ln, :]
        m_new = jnp.maximum(f_end + m, jnp.max(f_end - fc_r + ic_r, axis=1, keepdims=True))
        kw = kh * jnp.exp(f_end - fc_c + ic_c - m_new)
        decay = jnp.exp(f_end + m - m_new)
        c_scr[hs] = decay * c + _bdot(kw, vh, TN)
        n_scr[hs:hs + 1, :] = decay * nrow + jnp.sum(kw, axis=0, keepdims=True)
        m_scr[hs:hs + 1, :] = jnp.broadcast_to(m_new, (1, LANES))
        dmat = jnp.where(tril, fc_c - fc_r + ic_r, -jnp.inf)
        inter = fc_c + m
        mt = jnp.maximum(inter, jnp.max(dmat, axis=1, keepdims=True))
        wmat = s * jnp.exp(dmat - mt)
        staged.append((wmat, vh, qc, qn, inter, mt))
    for (bi, h), (wmat, vh, qc, qn, inter, mt) in zip(heads, staged):
        sl = slice(h * DK_A, (h + 1) * DK_A)
        a_inter = jnp.exp(inter - mt)
        num = _bdot(wmat, vh) + a_inter * qc
        den = jnp.sum(wmat, axis=1, keepdims=True) + a_inter * qn
        hh = num / jnp.maximum(jnp.abs(den), jnp.exp(-mt))
        ha = _sigmoid(o_ref[bi, :, sl]) * hh
        h_ref[bi, :, sl] = _rms(ha, nw_ref[:, sl])

    @pl.when(ci == pl.num_programs(0) - 1)
    def _():
        cout_ref[...] = c_scr[...]
        nout_ref[...] = n_scr[...]
        mout_ref[...] = m_scr[...]


def _mlstm_prompt(a_main, g_col, b_gates, norm_w, batch, seq):
    ln = CHUNK_A
    nc = seq // ln
    w = NH_A * DK_A
    nst = batch * NH_A
    a3 = a_main.reshape(batch, seq, 4 * w)
    blk = lambda j: pl.BlockSpec((batch, ln, w), lambda c, j=j: (0, c, j))
    bc = jnp.zeros((1, LANES), F32).at[0, :2 * NH_A].set(b_gates)
    const = lambda c: (0, 0)
    ha, c1, n1, m1 = pl.pallas_call(
        _mlstm_kernel, grid=(nc,),
        in_specs=[blk(0), blk(1), blk(2), blk(3),
                  pl.BlockSpec((batch, ln, LANES), lambda c: (0, c, 0)),
                  pl.BlockSpec((1, LANES), const), pl.BlockSpec((1, w), const)],
        out_specs=[pl.BlockSpec((batch, ln, w), lambda c: (0, c, 0)),
                   pl.BlockSpec((nst, DK_A, DV_A), lambda c: (0, 0, 0)),
                   pl.BlockSpec((nst, DK_A), const), pl.BlockSpec((nst, LANES), const)],
        out_shape=[jax.ShapeDtypeStruct((batch, seq, w), F32),
                   jax.ShapeDtypeStruct((nst, DK_A, DV_A), F32),
                   jax.ShapeDtypeStruct((nst, DK_A), F32),
                   jax.ShapeDtypeStruct((nst, LANES), F32)],
        scratch_shapes=[pltpu.VMEM((nst, DK_A, DV_A), F32), pltpu.VMEM((nst, LANES), F32),
                        pltpu.VMEM((nst, LANES), F32)],
        compiler_params=_params(("arbitrary",)), name="mlstm_prompt",
    )(a3, a3, a3, a3, g_col.reshape(batch, seq, LANES), bc, norm_w.reshape(1, w))
    return (ha.reshape(batch * seq, w), c1.reshape(batch, NH_A, DK_A, DV_A), n1.reshape(batch, NH_A, DK_A),
            m1.reshape(batch, NH_A, LANES))


def _mlstm_step_kernel(a_ref, qc_ref, kc_ref, g_ref, b_ref, m_ref, c_ref, n_ref, nw_ref,
                       h_ref, cout_ref, nout_ref, mout_ref):
    g = g_ref[0] + b_ref[...]
    lf_all = _log_sigmoid(g)
    w = NH_A * DK_A
    scale = DK_A ** -0.5
    for h in range(NH_A):
        sl = slice(h * DK_A, (h + 1) * DK_A)
        q_row = a_ref[0, :, sl]
        k_row = a_ref[0, :, w + h * DK_A:w + (h + 1) * DK_A] * scale
        v_row = a_ref[0, :, 2 * w + h * DV_A:2 * w + (h + 1) * DV_A]
        o_row = a_ref[0, :, 3 * w + h * DV_A:3 * w + (h + 1) * DV_A]
        q_col = qc_ref[0, sl, :]
        k_col = kc_ref[0, sl, :] * scale
        c = c_ref[0, h]
        n_row = n_ref[0, h:h + 1, :]
        m = m_ref[0, :, h:h + 1]
        ic = g[:, h:h + 1]
        lf = lf_all[:, NH_A + h:NH_A + h + 1]
        inter = lf + m
        mt = jnp.maximum(inter, ic)
        wm = jnp.sum(q_row * k_row, axis=1, keepdims=True) * jnp.exp(ic - mt)
        a_inter = jnp.exp(inter - mt)
        num = wm * v_row + a_inter * jnp.sum(q_col * c, axis=0, keepdims=True)
        den = wm + a_inter * jnp.sum(q_row * n_row, axis=1, keepdims=True)
        hh = num / jnp.maximum(jnp.abs(den), jnp.exp(-mt))
        m_new = jnp.maximum(inter, ic)
        wgt = jnp.exp(ic - m_new)
        decay = jnp.exp(inter - m_new)
        cout_ref[0, h] = decay * c + (wgt * k_col) * v_row
        nout_ref[0, h:h + 1, :] = decay * n_row + wgt * k_row
        mout_ref[0, h:h + 1, :] = jnp.broadcast_to(m_new, (1, LANES))
        ha = _sigmoid(o_row) * hh
        h_ref[0, :, sl] = _rms(ha, nw_ref[:, sl])


def _mlstm_step(a_main, g_col, b_gates, norm_w, c0, n0, m0):
    ns = a_main.shape[0]
    w = NH_A * DK_A
    a3 = a_main.reshape(ns, 1, 4 * w)
    q_col = a_main[:, :w].reshape(ns, w, 1)
    k_col = a_main[:, w:2 * w].reshape(ns, w, 1)
    bc = jnp.zeros((1, LANES), F32).at[0, :2 * NH_A].set(b_gates)
    s3 = lambda s: (s, 0, 0)
    s4 = lambda s: (s, 0, 0, 0)
    const = lambda s: (0, 0)
    return pl.pallas_call(
        _mlstm_step_kernel, grid=(ns,),
        in_specs=[pl.BlockSpec((1, 1, 4 * w), s3), pl.BlockSpec((1, w, 1), s3), pl.BlockSpec((1, w, 1), s3),
                  pl.BlockSpec((1, 1, LANES), s3), pl.BlockSpec((1, LANES), const),
                  pl.BlockSpec((1, 1, NH_A), s3), pl.BlockSpec((1, NH_A, DK_A, DV_A), s4),
                  pl.BlockSpec((1, NH_A, DK_A), s3), pl.BlockSpec((1, w), const)],
        out_specs=[pl.BlockSpec((1, 1, w), s3), pl.BlockSpec((1, NH_A, DK_A, DV_A), s4),
                   pl.BlockSpec((1, NH_A, DK_A), s3), pl.BlockSpec((1, NH_A, LANES), s3)],
        out_shape=[jax.ShapeDtypeStruct((ns, 1, w), F32), jax.ShapeDtypeStruct((ns, NH_A, DK_A, DV_A), F32),
                   jax.ShapeDtypeStruct((ns, NH_A, DK_A), F32), jax.ShapeDtypeStruct((ns, NH_A, LANES), F32)],
        compiler_params=_params(("arbitrary",)), name="mlstm_step",
    )(a3, q_col, k_col, g_col.reshape(ns, 1, LANES), bc, m0.reshape(ns, 1, NH_A), c0, n0, norm_w.reshape(1, w))


def _top_blocks(bs, topk):
    blk_id = lax.broadcasted_iota(jnp.int32, bs.shape, 0).astype(F32)
    sel = jnp.zeros(bs.shape, F32)
    for _ in range(topk):
        mx = jnp.max(bs, axis=0, keepdims=True)
        idx = jnp.min(jnp.where(bs == mx, blk_id, float(bs.shape[0])), axis=0, keepdims=True)
        pick = (blk_id == idx) & (mx > -jnp.inf)
        sel = jnp.where(pick, 1.0, sel)
        bs = jnp.where(pick, -jnp.inf, bs)
    return sel


MOBA_ONES = 16


def _moba_kernel(n_full, q_ref, k_ref, v_ref, o_ref, kmean_scr, kb_scr, vt_scr, sel_scr, m_scr, acc_scr,
                 sa_scr, sb_scr, ga_scr, gb_scr):
    qi = pl.program_id(2)
    blk = MOBA_BLOCK
    grp = NH_B // KVH_B

    @pl.when(qi == 0)
    def _():
        kmean_scr[...] = jnp.zeros_like(kmean_scr)
        for n in range(n_full):
            kblk = k_ref[n * blk:(n + 1) * blk, :]
            kmean_scr[n:n + 1, :] = jnp.mean(kblk, axis=0, keepdims=True)
            kb_scr[n] = kblk.astype(BF16)
            vt_scr[n, 0:HD_B, :] = jnp.transpose(v_ref[n * blk:(n + 1) * blk, :]).astype(BF16)
            vt_scr[n, HD_B:HD_B + MOBA_ONES, :] = jnp.ones((MOBA_ONES, blk), BF16)

    q = jnp.concatenate([q_ref[:, g * HD_B:(g + 1) * HD_B] for g in range(grp)], axis=0)
    rows = grp * blk
    bs = _dot(kmean_scr[...], q, NT, HIGHEST)
    bs = jnp.where(lax.broadcasted_iota(jnp.int32, bs.shape, 0) < qi, bs, -jnp.inf)
    sel_scr[...] = _top_blocks(bs, min(MOBA_TOPK, n_full))
    qs = (q * (HD_B ** -0.5 * math.log2(math.e))).astype(BF16)

    key = lax.broadcasted_iota(jnp.int32, (blk, rows), 0)
    qpos = lax.broadcasted_iota(jnp.int32, (blk, rows), 1) % blk
    causal = key <= qpos

    def score(first, own_group, s_scr, g_scr):
        gmax = None
        for g in range(MOBA_GROUP):
            b = first + g
            ok = sel_scr[pl.ds(b, 1), :] > 0.0
            if own_group:
                ok = ((b < qi) & ok) | ((b == qi) & causal)
            s = jnp.where(ok, _dot(kb_scr[b], qs, NT), -jnp.inf)
            s_scr[g] = s
            mx = jnp.max(s, axis=0, keepdims=True)
            gmax = mx if gmax is None else jnp.maximum(gmax, mx)
        g_scr[...] = gmax

    def update(first, s_scr, g_scr):
        m_old = m_scr[...]
        m_new = jnp.maximum(m_old, g_scr[...])
        acc_new = jnp.exp2(m_old - m_new) * acc_scr[...]
        for g in range(MOBA_GROUP):
            p = jnp.exp2(s_scr[g] - m_new)
            acc_new = acc_new + _dot(vt_scr[first + g], p.astype(BF16))
        m_scr[...] = m_new
        acc_scr[...] = acc_new

    m_scr[...] = jnp.full(m_scr.shape, -jnp.inf, F32)
    acc_scr[...] = jnp.zeros(acc_scr.shape, F32)
    n_past = qi // MOBA_GROUP
    own_first = n_past * MOBA_GROUP
    score(own_first, True, sa_scr, ga_scr)

    def body(i, carry):
        f0 = 2 * i * MOBA_GROUP
        f1 = f0 + MOBA_GROUP
        in_a = jnp.where(i == 0, own_first, f0 - MOBA_GROUP)
        score(f0, False, sb_scr, gb_scr)
        update(in_a, sa_scr, ga_scr)
        score(f1, False, sa_scr, ga_scr)
        update(f0, sb_scr, gb_scr)
        return carry

    n_pair = n_past // 2
    lax.fori_loop(0, n_pair, body, 0)
    in_a = jnp.where(n_pair == 0, own_first, (2 * n_pair - 1) * MOBA_GROUP)

    @pl.when(n_past % 2 == 1)
    def _():
        last = (n_past - 1) * MOBA_GROUP
        score(last, False, sb_scr, gb_scr)
        update(in_a, sa_scr, ga_scr)
        update(last, sb_scr, gb_scr)

    @pl.when(n_past % 2 == 0)
    def _():
        update(in_a, sa_scr, ga_scr)

    out = acc_scr[0:HD_B, :] / acc_scr[HD_B:HD_B + 1, :]
    for g in range(grp):
        o_ref[:, g * HD_B:(g + 1) * HD_B] = jnp.transpose(out[:, g * blk:(g + 1) * blk])


MOBA_GROUP = 4


def _moba_prompt(qb, kb, vb, batch, seq):
    m = qb.shape[0]
    blk = MOBA_BLOCK
    nq = seq // blk
    assert seq % (blk * MOBA_GROUP) == 0
    grp = NH_B // KVH_B
    return pl.pallas_call(
        functools.partial(_moba_kernel, seq // blk), grid=(batch, KVH_B, nq),
        in_specs=[pl.BlockSpec((blk, grp * HD_B), lambda b, j, i: (b * nq + i, j)),
                  pl.BlockSpec((seq, HD_B), lambda b, j, i: (b, j)),
                  pl.BlockSpec((seq, HD_B), lambda b, j, i: (b, j))],
        out_specs=pl.BlockSpec((blk, grp * HD_B), lambda b, j, i: (b * nq + i, j)),
        out_shape=jax.ShapeDtypeStruct((m, NH_B * HD_B), F32),
        scratch_shapes=[pltpu.VMEM((-(-nq // SUBLANES) * SUBLANES, HD_B), F32),
                        pltpu.VMEM((nq, blk, HD_B), BF16), pltpu.VMEM((nq, HD_B + MOBA_ONES, blk), BF16),
                        pltpu.VMEM((-(-nq // SUBLANES) * SUBLANES, grp * blk), F32),
                        pltpu.VMEM((1, grp * blk), F32), pltpu.VMEM((HD_B + MOBA_ONES, grp * blk), F32),
                        pltpu.VMEM((MOBA_GROUP, blk, grp * blk), F32), pltpu.VMEM((MOBA_GROUP, blk, grp * blk), F32),
                        pltpu.VMEM((1, grp * blk), F32), pltpu.VMEM((1, grp * blk), F32)],
        compiler_params=_params(("arbitrary", "arbitrary", "arbitrary")), name="moba_prompt",
    )(qb, kb, vb)


MOBA_PAGES_PER_STEP = 32


def _moba_select_kernel(n_steps, pt_ref, q_ref, *refs):
    pages, (kmean_ref, sel_ref) = refs[:-2], refs[-2:]
    j = pl.program_id(1)
    page = pages[0].shape[1] // KVH_B

    def page_sum(p):
        heads = [jnp.sum(p[0, pl.ds(kv, page, stride=KVH_B), :], axis=0, keepdims=True) for kv in range(KVH_B)]
        return jnp.concatenate(heads, axis=1)

    sums = [page_sum(p) for p in pages]
    nblk = len(pages) // 2
    per_blk = [(sums[2 * i] + sums[2 * i + 1]) * (1.0 / MOBA_BLOCK) for i in range(nblk)]
    kmean_ref[0, pl.ds(pl.multiple_of(j * nblk, SUBLANES), nblk), :] = jnp.concatenate(per_blk, axis=0)

    @pl.when(j == n_steps - 1)
    def _():
        grp = NH_B // KVH_B
        nb = kmean_ref.shape[1]
        rowi = lax.broadcasted_iota(jnp.int32, (nb, 1), 0).astype(F32)
        orow = lax.broadcasted_iota(jnp.int32, (SUBLANES, LANES), 0)
        olane = lax.broadcasted_iota(jnp.int32, (SUBLANES, LANES), 1)
        out = jnp.zeros((SUBLANES, LANES), F32)
        for h in range(NH_B):
            kv = h // grp
            km = kmean_ref[0, :, kv * HD_B:(kv + 1) * HD_B]
            qh = q_ref[0, :, h * HD_B:(h + 1) * HD_B]
            bs = jnp.sum(km * qh, axis=1, keepdims=True)
            for r in range(MOBA_TOPK):
                mx = jnp.max(bs, axis=0, keepdims=True)
                idx = jnp.min(jnp.where(bs == mx, rowi, float(nb)), axis=0, keepdims=True)
                out = jnp.where((orow == h) & (olane == r), idx, out)
                bs = jnp.where(rowi == idx, -jnp.inf, bs)
        sel_ref[0] = out.astype(jnp.int32)


def _moba_select(qb, k_cache, page_table):
    ns, n_pages = page_table.shape
    page = k_cache.shape[1]
    pps = min(MOBA_PAGES_PER_STEP, n_pages)
    assert 2 * page == MOBA_BLOCK and n_pages % pps == 0 and pps % (2 * SUBLANES) == 0
    n_steps = n_pages // pps
    kw = KVH_B * HD_B
    kc = k_cache.reshape(k_cache.shape[0], page * KVH_B, HD_B)
    page_spec = lambda i: pl.BlockSpec(
        (1, page * KVH_B, HD_B), lambda s, j, pt, i=i: (pt[s * n_pages + j * pps + i], 0, 0))
    nb = n_pages * page // MOBA_BLOCK
    grid_spec = pltpu.PrefetchScalarGridSpec(
        num_scalar_prefetch=1, grid=(ns, n_steps),
        in_specs=[pl.BlockSpec((1, 1, NH_B * HD_B), lambda s, j, pt: (s, 0, 0))]
        + [page_spec(i) for i in range(pps)],
        out_specs=[pl.BlockSpec((1, nb, kw), lambda s, j, pt: (s, 0, 0)),
                   pl.BlockSpec((1, SUBLANES, LANES), lambda s, j, pt: (s, 0, 0))])
    return pl.pallas_call(
        functools.partial(_moba_select_kernel, n_steps), grid_spec=grid_spec,
        out_shape=[jax.ShapeDtypeStruct((ns, nb, kw), F32), jax.ShapeDtypeStruct((ns, SUBLANES, LANES), jnp.int32)],
        compiler_params=_params(("arbitrary", "arbitrary")), name="moba_select",
    )(page_table.reshape(-1), qb.reshape(ns, 1, NH_B * HD_B), *([kc] * pps))


def _moba_step_kernel(sel_ref, pt_ref, q_ref, kn_ref, vn_ref, *refs):
    pages, o_ref, (m_scr, l_scr, acc_scr) = refs[:4 * NH_B], refs[4 * NH_B], refs[4 * NH_B + 1:]
    r = pl.program_id(1)
    scale = HD_B ** -0.5
    grp = NH_B // KVH_B
    page = pages[0].shape[1] // KVH_B
    for h in range(NH_B):
        kv = h // grp
        row = slice(h, h + 1)
        q = q_ref[0, :, h * HD_B:(h + 1) * HD_B]
        ka_ref, kb_ref, va_ref, vb_ref = pages[4 * h:4 * h + 4]

        @pl.when(r == 0)
        def _():
            s0 = jnp.sum(q * kn_ref[0, :, kv * HD_B:(kv + 1) * HD_B], axis=1, keepdims=True) * scale
            m_scr[row, :] = jnp.broadcast_to(s0, (1, LANES))
            l_scr[row, :] = jnp.ones((1, LANES), F32)
            acc_scr[row, :] = vn_ref[0, :, kv * HD_B:(kv + 1) * HD_B]

        rows = pl.ds(kv, page, stride=KVH_B)
        kk = jnp.concatenate([ka_ref[0, rows, :], kb_ref[0, rows, :]], axis=0)
        vv = jnp.concatenate([va_ref[0, rows, :], vb_ref[0, rows, :]], axis=0)
        s = jnp.sum(kk * q, axis=1, keepdims=True) * scale
        m_old = m_scr[row, 0:1]
        m_new = jnp.maximum(m_old, jnp.max(s, axis=0, keepdims=True))
        alpha = jnp.exp(m_old - m_new)
        p = jnp.exp(s - m_new)
        m_scr[row, :] = jnp.broadcast_to(m_new, (1, LANES))
        l_scr[row, :] = alpha * l_scr[row, :] + jnp.sum(p, axis=0, keepdims=True)
        acc_scr[row, :] = alpha * acc_scr[row, :] + jnp.sum(p * vv, axis=0, keepdims=True)

        @pl.when(r == pl.num_programs(1) - 1)
        def _():
            o_ref[0, :, h * HD_B:(h + 1) * HD_B] = acc_scr[row, :] / l_scr[row, :]


def _moba_step(qb, k_new, v_new, k_cache, v_cache, page_table, sel):
    ns, n_pages = page_table.shape
    page = k_cache.shape[1]
    kw = KVH_B * HD_B
    kc = k_cache.reshape(k_cache.shape[0], page * KVH_B, HD_B)
    vc = v_cache.reshape(v_cache.shape[0], page * KVH_B, HD_B)

    def page_spec(h, half):
        def imap(s, r, sel_r, pt_r):
            blk = sel_r[(s * NH_B + h) * MOBA_TOPK + r]
            return (pt_r[s * n_pages + 2 * blk + half], 0, 0)
        return pl.BlockSpec((1, page * KVH_B, HD_B), imap)

    seq_row = lambda w: pl.BlockSpec((1, 1, w), lambda s, r, a, b: (s, 0, 0))
    page_specs, page_args = [], []
    for h in range(NH_B):
        page_specs += [page_spec(h, 0), page_spec(h, 1), page_spec(h, 0), page_spec(h, 1)]
        page_args += [kc, kc, vc, vc]
    grid_spec = pltpu.PrefetchScalarGridSpec(
        num_scalar_prefetch=2, grid=(ns, MOBA_TOPK),
        in_specs=[seq_row(NH_B * HD_B), seq_row(kw), seq_row(kw)] + page_specs,
        out_specs=seq_row(NH_B * HD_B),
        scratch_shapes=[pltpu.VMEM((SUBLANES, LANES), F32), pltpu.VMEM((SUBLANES, LANES), F32),
                        pltpu.VMEM((SUBLANES, HD_B), F32)])
    out = pl.pallas_call(
        _moba_step_kernel, grid_spec=grid_spec,
        out_shape=jax.ShapeDtypeStruct((ns, 1, NH_B * HD_B), F32),
        compiler_params=_params(("arbitrary", "arbitrary")), name="moba_step",
    )(sel.reshape(-1), page_table.reshape(-1), qb.reshape(ns, 1, NH_B * HD_B),
      k_new.reshape(ns, 1, kw), v_new.reshape(ns, 1, kw), *page_args)
    return out.reshape(ns, NH_B * HD_B)


def _outproj_kernel(a_ref, b_ref, wa_ref, wb_ref, x_ref, o_ref):
    o_ref[...] = x_ref[...] + _bdot(a_ref[...], wa_ref[...]) + _bdot(b_ref[...], wb_ref[...])


def _outproj(a, b, wa, wb, x, tm, name):
    m, d = x.shape
    row = lambda i: (i, 0)
    const = lambda i: (0, 0)
    return pl.pallas_call(
        _outproj_kernel, grid=(m // tm,),
        in_specs=[pl.BlockSpec((tm, a.shape[1]), row), pl.BlockSpec((tm, b.shape[1]), row),
                  pl.BlockSpec(wa.shape, const), pl.BlockSpec(wb.shape, const), pl.BlockSpec((tm, d), row)],
        out_specs=pl.BlockSpec((tm, d), row), out_shape=jax.ShapeDtypeStruct((m, d), F32),
        compiler_params=_params(("arbitrary",)), name=name,
    )(a, b, wa, wb, x)


FFN_PHASES = 4


def _ffn_kernel(final, x_ref, nw_ref, wup_ref, cw_ref, cb_ref, wdn_ref, nf_ref, o_ref, cs_ref, ext_scr, act_scr):
    tm = x_ref.shape[0]
    pad = SUBLANES
    rows = tm // FFN_PHASES
    nlane = FFN_CW // LANES
    gate_blocks = D_FF // LANES

    @pl.when(pl.program_id(1) == 0)
    def _():
        ext_scr[:, 0:pad, :] = jnp.zeros((ext_scr.shape[0], pad, LANES), F32)

    x = x_ref[...]
    xb = _rms(x, nw_ref[...]).astype(BF16)
    acc = x

    def up(c):
        for half, col0 in enumerate((c * FFN_CW, D_FF + c * FFN_CW)):
            u = _dot(xb, wup_ref[:, col0:col0 + FFN_CW])
            for i in range(nlane):
                ext_scr[half * gate_blocks + c * nlane + i, pad:pad + tm, :] = u[:, i * LANES:(i + 1) * LANES]

    def conv(k, p):
        cols = slice(k * LANES, (k + 1) * LANES)
        y = cb_ref[:, cols]
        for j in range(FFN_CONV):
            start = pad + p - (FFN_CONV - 1) + j
            y = y + cw_ref[j:j + 1, cols] * ext_scr[k, pl.ds(start, rows, stride=FFN_PHASES), :]
        return y

    nch = D_FF // FFN_CW
    up(0)
    for c in range(nch):
        if c + 1 < nch:
            up(c + 1)
        for i in range(nlane):
            kg = c * nlane + i
            kv = gate_blocks + kg
            for p in range(FFN_PHASES):
                act_scr[c % 2, i, pl.ds(p, rows, stride=FFN_PHASES), :] = _silu(conv(kg, p)) * conv(kv, p)
            for k in (kg, kv):
                ext_scr[k, 0:pad, :] = ext_scr[k, tm:tm + pad, :]
        act = jnp.concatenate([act_scr[c % 2, i] for i in range(nlane)], axis=1).astype(BF16)
        acc = acc + _dot(act, wdn_ref[c * FFN_CW:(c + 1) * FFN_CW, :])
    if final:
        acc = _rms(acc, nf_ref[...])
    o_ref[...] = acc
    for k in range(ext_scr.shape[0]):
        cs_ref[0, :, k * LANES:(k + 1) * LANES] = ext_scr[k, 0:pad, :]


def _ffn_prompt(x, nw, w_up, conv_w, conv_b, w_down, nf, final, batch, seq, tm, name):
    m, d = x.shape
    nt = seq // tm
    row = lambda b, i: (b * nt + i, 0)
    const = lambda b, i: (0, 0)
    out, cs = pl.pallas_call(
        functools.partial(_ffn_kernel, final), grid=(batch, nt),
        in_specs=[pl.BlockSpec((tm, d), row), pl.BlockSpec((1, d), const), pl.BlockSpec(w_up.shape, const),
                  pl.BlockSpec(conv_w.shape, const), pl.BlockSpec((1, 2 * D_FF), const),
                  pl.BlockSpec(w_down.shape, const), pl.BlockSpec((1, d), const)],
        out_specs=[pl.BlockSpec((tm, d), row), pl.BlockSpec((1, SUBLANES, 2 * D_FF), lambda b, i: (b, 0, 0))],
        out_shape=[jax.ShapeDtypeStruct((m, d), F32), jax.ShapeDtypeStruct((batch, SUBLANES, 2 * D_FF), F32)],
        scratch_shapes=[pltpu.VMEM((2 * D_FF // LANES, tm + SUBLANES, LANES), F32),
                        pltpu.VMEM((2, FFN_CW // LANES, tm, LANES), F32)],
        compiler_params=_params(("arbitrary", "arbitrary")), name=name,
    )(x, nw.reshape(1, d), w_up, conv_w, conv_b.reshape(1, -1), w_down, nf.reshape(1, d))
    return out, cs[:, SUBLANES - (FFN_CONV - 1):, :]


def _ffn_step_kernel(final, x_ref, nw_ref, wg_ref, wv_ref, g0_ref, g1_ref, v0_ref, v1_ref,
                     cwg_ref, cwv_ref, cbg_ref, cbv_ref, wdn_ref, nf_ref, o_ref, ug_ref, uv_ref, acc_scr):
    c = pl.program_id(0)

    @pl.when(c == 0)
    def _():
        acc_scr[...] = x_ref[...]

    xb = _rms(x_ref[...], nw_ref[...]).astype(BF16)
    ug = _dot(xb, wg_ref[...])
    uv = _dot(xb, wv_ref[...])
    ug_ref[...] = ug
    uv_ref[...] = uv
    gate = cbg_ref[...] + cwg_ref[0:1, :] * g0_ref[...] + cwg_ref[1:2, :] * g1_ref[...] + cwg_ref[2:3, :] * ug
    val = cbv_ref[...] + cwv_ref[0:1, :] * v0_ref[...] + cwv_ref[1:2, :] * v1_ref[...] + cwv_ref[2:3, :] * uv
    acc_scr[...] += _dot((_silu(gate) * val).astype(BF16), wdn_ref[...])

    @pl.when(c == pl.num_programs(0) - 1)
    def _():
        acc = acc_scr[...]
        o_ref[...] = _rms(acc, nf_ref[...]) if final else acc


def _ffn_step(x, nw, w_up, conv_w, conv_b, w_down, nf, final, buf, name):
    ns, d = x.shape
    cw = FFN_CW
    nch = D_FF // cw
    ntot = 2 * D_FF // cw
    bufw = buf.reshape(ns, 2 * 2 * D_FF)
    cb = conv_b.reshape(1, -1)
    const = lambda c: (0, 0)
    gcol = lambda c: (0, c)
    vcol = lambda c: (0, nch + c)
    out, u_g, u_v = pl.pallas_call(
        functools.partial(_ffn_step_kernel, final), grid=(nch,),
        in_specs=[pl.BlockSpec((ns, d), const), pl.BlockSpec((1, d), const),
                  pl.BlockSpec((d, cw), gcol), pl.BlockSpec((d, cw), vcol),
                  pl.BlockSpec((ns, cw), gcol), pl.BlockSpec((ns, cw), lambda c: (0, ntot + c)),
                  pl.BlockSpec((ns, cw), vcol), pl.BlockSpec((ns, cw), lambda c: (0, ntot + nch + c)),
                  pl.BlockSpec((FFN_CONV, cw), gcol), pl.BlockSpec((FFN_CONV, cw), vcol),
                  pl.BlockSpec((1, cw), gcol), pl.BlockSpec((1, cw), vcol),
                  pl.BlockSpec((cw, d), lambda c: (c, 0)), pl.BlockSpec((1, d), const)],
        out_specs=[pl.BlockSpec((ns, d), const), pl.BlockSpec((ns, cw), gcol), pl.BlockSpec((ns, cw), gcol)],
        out_shape=[jax.ShapeDtypeStruct((ns, d), F32), jax.ShapeDtypeStruct((ns, D_FF), F32),
                   jax.ShapeDtypeStruct((ns, D_FF), F32)],
        scratch_shapes=[pltpu.VMEM((ns, d), F32)],
        compiler_params=_params(("arbitrary",)), name=name,
    )(x, nw.reshape(1, d), w_up, w_up, bufw, bufw, bufw, bufw, conv_w, conv_w, cb, cb, w_down, nf.reshape(1, d))
    u = jnp.concatenate([u_g, u_v], axis=1)
    return out, jnp.stack([buf[:, 1], u], axis=1)


def _ssd_kernel(z_ref, xbc_ref, dc_ref, cw_ref, cb_ref, dbc_ref, alc_ref,
                dsk_ref, nw_ref, y_ref, hout_ref, cs_ref, h_scr, carry_scr, ext_scr, y_scr):
    ci = pl.program_id(1)
    ln = CHUNK_C
    pad = SUBLANES

    @pl.when(ci == 0)
    def _():
        h_scr[...] = jnp.zeros_like(h_scr)
        carry_scr[...] = jnp.zeros_like(carry_scr)

    raw = xbc_ref[...]
    ext_scr[0:pad, :] = carry_scr[...]
    ext_scr[pad:pad + ln, :] = raw
    carry_scr[...] = ext_scr[ln:ln + pad, :]
    y = cb_ref[...] + cw_ref[CONV_C - 1:CONV_C, :] * raw
    for j in range(CONV_C - 1):
        off = pad - (CONV_C - 1) + j
        y = y + cw_ref[j:j + 1, :] * ext_scr[off:off + ln, :]
    xbc = _silu(y)
    bw = NG_C * DS_C
    xs, bm, cm = xbc[:, :DI_C], xbc[:, DI_C:DI_C + bw], xbc[:, DI_C + bw:]

    row = lax.broadcasted_iota(jnp.int32, (ln, ln), 0)
    col = lax.broadcasted_iota(jnp.int32, (ln, ln), 1)
    tril = row >= col
    lane_lo = lax.broadcasted_iota(jnp.int32, (ln, LANES), 1) < HD_C
    row_lo = lax.broadcasted_iota(jnp.int32, (2 * HD_C, DS_C), 0) < HD_C
    dt_c = _softplus(dc_ref[...] + dbc_ref[...])
    ac_c = dt_c * (-jnp.exp(alc_ref[...]))
    dt_r = jnp.transpose(dt_c)[0:NH_C, :]
    ac_r = jnp.transpose(ac_c)[0:NH_C, :]
    acum_c = _dot(tril.astype(F32), ac_c, NN, HIGHEST)
    acum_r = _dot(ac_r, (row <= col).astype(F32), NN, HIGHEST)
    hpg = NH_C // NG_C
    for g in range(NG_C):
        bg = bm[:, g * DS_C:(g + 1) * DS_C]
        cg = cm[:, g * DS_C:(g + 1) * DS_C]
        cb_mat = _bdot(cg, bg, NT)
        for pi in range(hpg // 2):
            pair = g * (hpg // 2) + pi
            h0, h1 = 2 * pair, 2 * pair + 1
            xp = xs[:, pair * LANES:(pair + 1) * LANES]
            hp = h_scr[pair]
            a0, a1 = acum_c[:, h0:h0 + 1], acum_c[:, h1:h1 + 1]

            def mmat(h, a_col):
                decay = jnp.exp(jnp.where(tril, a_col - acum_r[h:h + 1, :], -jnp.inf))
                return cb_mat * decay * dt_r[h:h + 1, :]

            yy = _bdot(mmat(h0, a0), jnp.where(lane_lo, xp, 0.0)) + _bdot(mmat(h1, a1), jnp.where(lane_lo, 0.0, xp))
            yy = yy + jnp.where(lane_lo, jnp.exp(a0), jnp.exp(a1)) * _bdot(cg, hp, NT)
            e0, e1 = a0[ln - 1:ln, :], a1[ln - 1:ln, :]
            wend = jnp.where(lane_lo, jnp.exp(e0 - a0) * dt_c[:, h0:h0 + 1], jnp.exp(e1 - a1) * dt_c[:, h1:h1 + 1])
            h_scr[pair] = jnp.where(row_lo, jnp.exp(e0), jnp.exp(e1)) * hp + _bdot(xp * wend, bg, TN)
            y_scr[:, pair * LANES:(pair + 1) * LANES] = yy + dsk_ref[:, pair * LANES:(pair + 1) * LANES] * xp
    gated = y_scr[...] * _silu(z_ref[...])
    gw = DI_C // NG_C
    for g in range(NG_C):
        sl = slice(g * gw, (g + 1) * gw)
        y_ref[:, sl] = _rms(gated[:, sl], nw_ref[:, sl])
    cs_ref[0] = carry_scr[...]

    @pl.when(ci == pl.num_programs(1) - 1)
    def _():
        hout_ref[0] = h_scr[...]


def _head_lanes(v, width):
    return jnp.repeat(v.astype(F32), width).reshape(1, -1)


def _ssd_prompt(z, xbc, d_col, conv_w, conv_b, dt_bias, a_log, d_skip, norm_w, batch, seq):
    m = z.shape[0]
    ln = CHUNK_C
    nc = seq // ln
    npair = NH_C // 2
    lane8 = lambda v: jnp.zeros((1, LANES), F32).at[0, :NH_C].set(v)
    row = lambda b, c: (b * nc + c, 0)
    const = lambda b, c: (0, 0)
    y, h1, cs = pl.pallas_call(
        _ssd_kernel, grid=(batch, nc),
        in_specs=[pl.BlockSpec((ln, DI_C), row), pl.BlockSpec((ln, CONV_DIM_C), row),
                  pl.BlockSpec((ln, LANES), row),
                  pl.BlockSpec((CONV_C, CONV_DIM_C), const), pl.BlockSpec((1, CONV_DIM_C), const),
                  pl.BlockSpec((1, LANES), const), pl.BlockSpec((1, LANES), const),
                  pl.BlockSpec((1, DI_C), const), pl.BlockSpec((1, DI_C), const)],
        out_specs=[pl.BlockSpec((ln, DI_C), row),
                   pl.BlockSpec((1, npair, 2 * HD_C, DS_C), lambda b, c: (b, 0, 0, 0)),
                   pl.BlockSpec((1, SUBLANES, CONV_DIM_C), lambda b, c: (b, 0, 0))],
        out_shape=[jax.ShapeDtypeStruct((m, DI_C), F32),
                   jax.ShapeDtypeStruct((batch, npair, 2 * HD_C, DS_C), F32),
                   jax.ShapeDtypeStruct((batch, SUBLANES, CONV_DIM_C), F32)],
        scratch_shapes=[pltpu.VMEM((npair, 2 * HD_C, DS_C), F32), pltpu.VMEM((SUBLANES, CONV_DIM_C), F32),
                        pltpu.VMEM((ln + SUBLANES, CONV_DIM_C), F32), pltpu.VMEM((ln, DI_C), F32)],
        compiler_params=_params(("arbitrary", "arbitrary")), name="ssd_prompt",
    )(z, xbc, d_col, conv_w, conv_b.reshape(1, -1), lane8(dt_bias), lane8(a_log),
      _head_lanes(d_skip, HD_C), norm_w.reshape(1, DI_C))
    return y, h1.reshape(batch, NH_C, HD_C, DS_C), cs[:, SUBLANES - (CONV_C - 1):, :]


def _lane_bcast_col(row_vec):
    return jnp.transpose(jnp.broadcast_to(row_vec, (LANES, LANES)))


def _ssd_step_kernel(z_ref, x_ref, cs_ref, d_ref, cw_ref, cb_ref, db_ref, al_ref, dsk_ref, nw_ref, h_ref,
                     y_ref, hout_ref):
    y = cb_ref[...] + cw_ref[CONV_C - 1:CONV_C, :] * x_ref[0]
    for j in range(CONV_C - 1):
        y = y + cw_ref[j:j + 1, :] * cs_ref[0, j:j + 1, :]
    xbc = _silu(y)
    bw = NG_C * DS_C
    dt = _softplus(d_ref[0] + db_ref[...])
    ea = jnp.exp(dt * (-jnp.exp(al_ref[...])))
    row_lo = lax.broadcasted_iota(jnp.int32, (LANES, 1), 0) < HD_C
    lane_lo = lax.broadcasted_iota(jnp.int32, (1, LANES), 1) < HD_C
    hpg = NH_C // NG_C
    parts = []
    for pair in range(NH_C // 2):
        g = (2 * pair) // hpg
        h0, h1 = 2 * pair, 2 * pair + 1
        xp = xbc[:, pair * LANES:(pair + 1) * LANES]
        bg = xbc[:, DI_C + g * DS_C:DI_C + (g + 1) * DS_C]
        cg = xbc[:, DI_C + bw + g * DS_C:DI_C + bw + (g + 1) * DS_C]
        hp = h_ref[0, pair]
        x_col = _lane_bcast_col(xp)
        ea_col = jnp.where(row_lo, ea[:, h0:h0 + 1], ea[:, h1:h1 + 1])
        dt_col = jnp.where(row_lo, dt[:, h0:h0 + 1], dt[:, h1:h1 + 1])
        hout_ref[0, pair] = ea_col * hp + (dt_col * x_col) * bg
        y_inter = jnp.sum(hp * cg, axis=1, keepdims=True)
        y_col = ea_col * y_inter
        y_row = jnp.transpose(jnp.broadcast_to(y_col, (LANES, LANES)))[0:1, :]
        dt_row = jnp.where(lane_lo, dt[:, h0:h0 + 1], dt[:, h1:h1 + 1])
        cb = jnp.sum(cg * bg, axis=1, keepdims=True)
        parts.append(y_row + (cb * dt_row) * xp + dsk_ref[:, pair * LANES:(pair + 1) * LANES] * xp)
    gated = jnp.concatenate(parts, axis=1) * _silu(z_ref[0])
    gw = DI_C // NG_C
    for g in range(NG_C):
        sl = slice(g * gw, (g + 1) * gw)
        y_ref[0, :, sl] = _rms(gated[:, sl], nw_ref[:, sl])


def _ssd_step(z, xbc, d_col, conv_state, conv_w, conv_b, dt_bias, a_log, d_skip, norm_w, h0):
    ns = z.shape[0]
    npair = NH_C // 2
    lane8 = lambda v: jnp.zeros((1, LANES), F32).at[0, :NH_C].set(v)
    s3 = lambda s: (s, 0, 0)
    s4 = lambda s: (s, 0, 0, 0)
    const = lambda s: (0, 0)
    y, h1 = pl.pallas_call(
        _ssd_step_kernel, grid=(ns,),
        in_specs=[pl.BlockSpec((1, 1, DI_C), s3), pl.BlockSpec((1, 1, CONV_DIM_C), s3),
                  pl.BlockSpec((1, CONV_C - 1, CONV_DIM_C), s3), pl.BlockSpec((1, 1, LANES), s3),
                  pl.BlockSpec((CONV_C, CONV_DIM_C), const), pl.BlockSpec((1, CONV_DIM_C), const),
                  pl.BlockSpec((1, LANES), const), pl.BlockSpec((1, LANES), const),
                  pl.BlockSpec((1, DI_C), const), pl.BlockSpec((1, DI_C), const),
                  pl.BlockSpec((1, npair, 2 * HD_C, DS_C), s4)],
        out_specs=[pl.BlockSpec((1, 1, DI_C), s3), pl.BlockSpec((1, npair, 2 * HD_C, DS_C), s4)],
        out_shape=[jax.ShapeDtypeStruct((ns, 1, DI_C), F32), jax.ShapeDtypeStruct((ns, npair, 2 * HD_C, DS_C), F32)],
        compiler_params=_params(("arbitrary",)), name="ssd_step",
    )(z.reshape(ns, 1, DI_C), xbc.reshape(ns, 1, CONV_DIM_C), conv_state, d_col.reshape(ns, 1, LANES),
      conv_w, conv_b.reshape(1, -1), lane8(dt_bias), lane8(a_log), _head_lanes(d_skip, HD_C),
      norm_w.reshape(1, DI_C), h0.reshape(ns, npair, 2 * HD_C, DS_C))
    return y.reshape(ns, DI_C), h1.reshape(ns, NH_C, HD_C, DS_C)


def _head_mask(width):
    lane = lax.broadcasted_iota(jnp.int32, (1, width), 1)
    return [(lane >= h * HD_D) & (lane < (h + 1) * HD_D) for h in range(width // HD_D)]


SWA_TILE = 2048


def _swa_kernel(dil, q_ref, kt_ref, kc_ref, vt_ref, vc_ref, acc_ref, m_ref, l_ref, kbuf, vbuf):
    t = pl.program_id(1)
    nk = SWA_KEYS
    tail = nk * dil
    tile = q_ref.shape[0]
    kbuf[0:tail, :] = kt_ref[...]
    kbuf[tail:tail + tile, :] = kc_ref[...]
    vbuf[0:tail, :] = vt_ref[...]
    vbuf[tail:tail + tile, :] = vc_ref[...]
    qi = lax.broadcasted_iota(jnp.int32, (nk, 2 * nk), 0)
    kj = lax.broadcasted_iota(jnp.int32, (nk, 2 * nk), 1)
    dist = nk + qi - kj
    band = (dist >= 0) & (dist <= nk)
    heads = _head_mask(LANES)

    for j in range(tile // tail):
        first = j * tail
        ok = band if j > 0 else band & ((t > 0) | (kj >= nk))
        q_win = q_ref.at[pl.ds(first, tail), :]
        k_win = kbuf.at[pl.ds(first, 2 * tail), :]
        v_win = vbuf.at[pl.ds(first, 2 * tail), :]
        outs = [o.at[pl.ds(first, tail), :] for o in (acc_ref, m_ref, l_ref)]
        for r in range(dil):
            q = q_win[pl.ds(r, nk, stride=dil), :] * (HD_D ** -0.5)
            kk = k_win[pl.ds(r, 2 * nk, stride=dil), :].astype(BF16)
            vv = v_win[pl.ds(r, 2 * nk, stride=dil), :].astype(BF16)
            acc = jnp.zeros((nk, LANES), F32)
            mm = jnp.zeros((nk, LANES), F32)
            ll = jnp.zeros((nk, LANES), F32)
            for hm in heads:
                s = _dot(jnp.where(hm, q, 0.0).astype(BF16), kk, NT)
                s = jnp.where(ok, s, -jnp.inf)
                mx = jnp.max(s, axis=1, keepdims=True)
                p = jnp.exp(s - mx)
                acc = jnp.where(hm, _dot(p.astype(BF16), vv), acc)
                mm = jnp.where(hm, mx, mm)
                ll = jnp.where(hm, jnp.sum(p, axis=1, keepdims=True), ll)
            for o, val in zip(outs, (acc, mm, ll)):
                o[pl.ds(r, nk, stride=dil), :] = val


def _swa_prompt(qd, kd, vd, g, dil, batch, seq):
    m = qd.shape[0]
    tile = min(SWA_TILE, seq)
    tail = SWA_KEYS * dil
    assert seq % tile == 0 and tile % tail == 0
    nt = seq // tile
    per = tile // tail
    halves = SWA_W // LANES
    cur = pl.BlockSpec((tile, LANES), lambda b, t, f: (b * nt + t, g * halves + f))
    prev = pl.BlockSpec((tail, LANES), lambda b, t, f: (jnp.maximum((b * nt + t) * per - 1, 0), g * halves + f))
    ospec = pl.BlockSpec((tile, LANES), lambda b, t, f: (b * nt + t, f))
    oshape = jax.ShapeDtypeStruct((m, SWA_W), F32)
    return pl.pallas_call(
        functools.partial(_swa_kernel, dil), grid=(batch, nt, halves),
        in_specs=[cur, prev, cur, prev, cur], out_specs=[ospec] * 3, out_shape=[oshape] * 3,
        scratch_shapes=[pltpu.VMEM((tail + tile, LANES), F32), pltpu.VMEM((tail + tile, LANES), F32)],
        compiler_params=_params(("arbitrary", "arbitrary", "arbitrary")), name=f"swa_prompt_{g}",
    )(qd, kd, kd, vd, vd)


def _swa_step_kernel(q_ref, kn_ref, vn_ref, c0_ref, c1_ref, c2_ref, o_ref):
    scale = HD_D ** -0.5
    ngrp = len(SWA_GROUPS)
    stats = []
    for g, (c_ref, (win, dil)) in enumerate(zip((c0_ref, c1_ref, c2_ref), SWA_GROUPS)):
        on_stride = lax.broadcasted_iota(jnp.int32, (1, win), 1) % dil == 0
        per_head = []
        for h in range(HPG_D):
            sl = slice((g * HPG_D + h) * HD_D, (g * HPG_D + h + 1) * HD_D)
            q = q_ref[0, sl, :] * scale
            s = jnp.sum(c_ref[0, 0, h] * q, axis=0, keepdims=True)
            s = jnp.where(on_stride, s, -jnp.inf)
            sn = jnp.sum(kn_ref[0, sl, :] * q, axis=0, keepdims=True)
            mx = jnp.maximum(jnp.max(s, axis=1, keepdims=True), sn)
            p = jnp.exp(s - mx)
            pn = jnp.exp(sn - mx)
            den = jnp.sum(p, axis=1, keepdims=True) + pn
            acc = jnp.sum(c_ref[0, 1, h] * p, axis=1, keepdims=True) + pn * vn_ref[0, sl, :]
            per_head.append((acc, mx, den))
        stats.append(per_head)
    for h in range(HPG_D):
        mx = functools.reduce(jnp.maximum, [stats[g][h][1] for g in range(ngrp)])
        wgt = [jnp.exp(stats[g][h][1] - mx) for g in range(ngrp)]
        num = functools.reduce(jnp.add, [wgt[g] * stats[g][h][0] for g in range(ngrp)])
        den = functools.reduce(jnp.add, [wgt[g] * stats[g][h][2] for g in range(ngrp)])
        o_ref[0, h * HD_D:(h + 1) * HD_D, :] = num / den


def _swa_step(qd, kd, vd, caches):
    ns = qd.shape[0]
    w3 = qd.shape[1]
    views, cspecs = [], []
    for (win, dil), cache in zip(SWA_GROUPS, caches):
        assert cache.shape[1] == win and win // dil == SWA_KEYS
        views.append(jnp.transpose(cache, (0, 2, 3, 4, 1)))
        cspecs.append(pl.BlockSpec((1, 2, HPG_D, HD_D, win), lambda s: (s, 0, 0, 0, 0)))
    s3 = lambda s: (s, 0, 0)
    col = pl.BlockSpec((1, w3, 1), s3)
    out = pl.pallas_call(
        _swa_step_kernel, grid=(ns,), in_specs=[col, col, col] + cspecs,
        out_specs=pl.BlockSpec((1, SWA_W, 1), s3), out_shape=jax.ShapeDtypeStruct((ns, SWA_W, 1), F32),
        compiler_params=_params(("arbitrary",)), name="swa_step",
    )(qd.reshape(ns, w3, 1), kd.reshape(ns, w3, 1), vd.reshape(ns, w3, 1), *views)
    return out.reshape(ns, SWA_W)


def _swa_merge_kernel(*refs):
    ins, o_ref = refs[:-1], refs[-1]
    ng = len(ins) // 3
    mx = ins[1][...]
    for g in range(1, ng):
        mx = jnp.maximum(mx, ins[3 * g + 1][...])
    num = jnp.zeros_like(mx)
    den = jnp.zeros_like(mx)
    for g in range(ng):
        wgt = jnp.exp(ins[3 * g + 1][...] - mx)
        num = num + wgt * ins[3 * g][...]
        den = den + wgt * ins[3 * g + 2][...]
    o_ref[...] = num / den


def _swa_merge(parts, tm):
    flat = [a for grp in parts for a in grp]
    m = flat[0].shape[0]
    spec = pl.BlockSpec((tm, SWA_W), lambda i: (i, 0))
    return pl.pallas_call(
        _swa_merge_kernel, grid=(m // tm,), in_specs=[spec] * len(flat), out_specs=spec,
        out_shape=jax.ShapeDtypeStruct((m, SWA_W), F32),
        compiler_params=_params(("arbitrary",)), name="swa_merge",
    )(*flat)


def _split_cols(w, sizes):
    out, start = [], 0
    for s in sizes:
        out.append(w[:, start:start + s])
        start += s
    return out


def _trunk(x3, pos0, state, wts, tm):
    batch, seq, d = x3.shape
    m = batch * seq
    x = x3.reshape(m, d)
    pos = pos0 + jnp.arange(seq, dtype=jnp.int32)
    prompt = state is None
    if not prompt:
        pos = jnp.broadcast_to(pos, (m,))
    tabs_b, half_b = _rope_tables(pos, HD_B)
    tabs_d, half_d = _rope_tables(pos, HD_D)
    wa = NH_A * DK_A

    a_main, qb, kb, vb, g_col = _inproj(
        x, wts["norm_mix"][0], tabs_b, wts["l0_w"], (0, half_b, half_b, 0, 0), tm, "inproj_l0")
    if prompt:
        ha, c1, n1, m1 = _mlstm_prompt(a_main, g_col, wts["b_gates"], wts["norm_mlstm"], batch, seq)
        hb = _moba_prompt(qb, kb, vb, batch, seq)
    else:
        c0, n0, m0, k_cache, v_cache, page_table = state["even"]
        ha, c1, n1, m1 = _mlstm_step(a_main, g_col, wts["b_gates"], wts["norm_mlstm"], c0, n0, m0)
        ha = ha.reshape(m, wa)
        _, sel = _moba_select(qb, k_cache, page_table)
        hb = _moba_step(qb, kb, vb, k_cache, v_cache, page_table, sel[:, :NH_B, :MOBA_TOPK])
    m1 = m1[:, :, 0]
    x = _outproj(ha, hb, wts["l0_out_a"], wts["l0_out_b"], x, tm, "outproj_l0")
    ffn = wts["ffn"]
    if prompt:
        x, fbuf0 = _ffn_prompt(x, wts["norm_ffn"][0], *ffn[0], wts["norm_final"], False, batch, seq,
                               min(tm, 256), "ffn_l0")
    else:
        x, fbuf0 = _ffn_step(x, wts["norm_ffn"][0], *ffn[0], wts["norm_final"], False, state["ffn"][0], "ffn_step_l0")

    z, xbc, qd, kd, vd, d_col = _inproj(
        x, wts["norm_mix"][1], tabs_d, wts["l1_w"], (0, 0, half_d, half_d, 0, 0), tm, "inproj_l1")
    ssd_w = wts["ssd"]
    if prompt:
        yc, h1, conv1 = _ssd_prompt(z, xbc, d_col, *ssd_w, batch, seq)
        parts = [_swa_prompt(qd, kd, vd, g, dil, batch, seq) for g, (_, dil) in enumerate(SWA_GROUPS)]
        od = _swa_merge(parts, tm)
    else:
        h0, conv0, caches = state["odd"]
        yc, h1 = _ssd_step(z, xbc, d_col, conv0, *ssd_w, h0)
        conv1 = jnp.concatenate([conv0[:, 1:], xbc[:, None, :]], axis=1)
        od = _swa_step(qd, kd, vd, caches)
    x = _outproj(yc, od, wts["l1_out_a"], wts["l1_out_b"], x, tm, "outproj_l1")
    if prompt:
        x, fbuf1 = _ffn_prompt(x, wts["norm_ffn"][1], *ffn[1], wts["norm_final"], True, batch, seq,
                               min(tm, 256), "ffn_l1")
    else:
        x, fbuf1 = _ffn_step(x, wts["norm_ffn"][1], *ffn[1], wts["norm_final"], True, state["ffn"][1], "ffn_step_l1")

    kr = kb.reshape(batch, seq, KVH_B, HD_B)
    vr = vb.reshape(batch, seq, KVH_B, HD_B)
    rows = []
    for g, (win, _) in enumerate(SWA_GROUPS):
        keep = min(win, seq) if prompt else seq
        kg = kd.reshape(batch, seq, -1)[:, seq - keep:, g * SWA_W:(g + 1) * SWA_W].reshape(batch, keep, HPG_D, HD_D)
        vg = vd.reshape(batch, seq, -1)[:, seq - keep:, g * SWA_W:(g + 1) * SWA_W].reshape(batch, keep, HPG_D, HD_D)
        rows.append(jnp.stack([kg, vg], axis=2))
    return (x.reshape(batch, seq, d), (c1, n1, m1, kr, vr), (h1, conv1, rows[0], rows[1], rows[2]),
            jnp.stack([fbuf0, fbuf1]))


def kernel(x_prompt, x_sample, state_l0_mlstm_c, state_l0_mlstm_n, state_l0_mlstm_m, cache_l0_moba_k, cache_l0_moba_v, state_l1_ssd_h, state_l1_ssd_conv, cache_l1_swa_kv0, cache_l1_swa_kv1, cache_l1_swa_kv2, state_ffn_conv, page_table, norm_mix, norm_ffn, norm_final, w_in_l0, b_gates_l0, norm_mlstm_l0, w_out_l0, w_in_l1, conv_w_l1, conv_b_l1, dt_bias_l1, a_log_l1, d_skip_l1, norm_ssd_l1, w_out_l1, ffn_up, ffn_conv_w, ffn_conv_b, ffn_down):
    wa = NH_A * DK_A
    qa, ka, va, oa, ia, fa, qb, kb, vb = _split_cols(
        w_in_l0, (wa, wa, wa, wa, NH_A, NH_A, NH_B * HD_B, KVH_B * HD_B, KVH_B * HD_B))
    wd = 3 * SWA_W
    z, xbc, dtr, qd, kd, vd = _split_cols(w_in_l1, (DI_C, CONV_DIM_C, NH_C, wd, wd, wd))
    bf = lambda a: a.astype(BF16)
    wts = {
        "norm_mix": norm_mix, "norm_ffn": norm_ffn, "norm_final": norm_final,
        "l0_w": [bf(jnp.concatenate([qa, ka, va, oa], axis=1)), bf(qb), bf(kb), bf(vb),
                 bf(_pad_lanes(jnp.concatenate([ia, fa], axis=1)))],
        "b_gates": b_gates_l0, "norm_mlstm": norm_mlstm_l0,
        "l0_out_a": bf(w_out_l0[:wa]), "l0_out_b": bf(w_out_l0[wa:]),
        "l1_w": [bf(z), bf(xbc), bf(qd), bf(kd), bf(vd), bf(_pad_lanes(dtr))],
        "ssd": (conv_w_l1, conv_b_l1, dt_bias_l1, a_log_l1, d_skip_l1, norm_ssd_l1),
        "l1_out_a": bf(w_out_l1[:DI_C]), "l1_out_b": bf(w_out_l1[DI_C:]),
        "ffn": [(bf(ffn_up[l]), ffn_conv_w[l], ffn_conv_b[l], bf(ffn_down[l])) for l in range(ffn_up.shape[0])],
    }
    n_seq, n_pages = page_table.shape
    past_len = n_pages * cache_l0_moba_k.shape[1]
    y_p, ev_p, od_p, ffn_p = _trunk(x_prompt, 0, None, wts, 512)
    state = {
        "even": (state_l0_mlstm_c, state_l0_mlstm_n, state_l0_mlstm_m, cache_l0_moba_k, cache_l0_moba_v, page_table),
        "odd": (state_l1_ssd_h, state_l1_ssd_conv, (cache_l1_swa_kv0, cache_l1_swa_kv1, cache_l1_swa_kv2)),
        "ffn": state_ffn_conv,
    }
    y_s, ev_s, od_s, ffn_s = _trunk(x_sample, past_len, state, wts, x_sample.shape[0] * x_sample.shape[1])
    c_p, n_p, m_p, k_p, v_p = ev_p
    c_s, n_s, m_s, k_s, v_s = ev_s
    h_p, conv_p, sw0_p, sw1_p, sw2_p = od_p
    h_s, conv_s, sw0_s, sw1_s, sw2_s = od_s
    return (y_p, y_s, c_p, c_s, n_p, n_s, m_p, m_s, k_p, k_s, v_p, v_s, h_p, h_s, conv_p, conv_s,
            sw0_p, sw0_s, sw1_p, sw1_s, sw2_p, sw2_s, ffn_p, ffn_s)
```

```python
import functools
import math

import numpy as np
import jax
import jax.numpy as jnp
from jax import lax
from jax.experimental import pallas as pl
from jax.experimental.pallas import tpu as pltpu

F32 = jnp.float32
BF16 = jnp.bfloat16
HIGHEST = lax.Precision.HIGHEST

EPS = 1e-6
ROPE_THETA = 500000.0
ROPE_FRACTION = 4
LANES = 128
SUBLANES = 8
VMEM_LIMIT = 56 * 1024 * 1024

NH_A, DK_A, DV_A, CHUNK_A = 4, 128, 128, 128
NH_B, KVH_B, HD_B, MOBA_BLOCK, MOBA_TOPK = 4, 2, 128, 256, 3
NH_C, HD_C, NG_C, DS_C, CONV_C, CHUNK_C = 8, 64, 2, 128, 4, 128
DI_C = NH_C * HD_C
CONV_DIM_C = DI_C + 2 * NG_C * DS_C
SWA_GROUPS = ((128, 1), (512, 4), (2048, 16))
HPG_D, HD_D = 4, 64
SWA_W = HPG_D * HD_D
SWA_KEYS = 128
D_FF, FFN_CONV = 2816, 3
FFN_CW = 256

NN = (((1,), (0,)), ((), ()))
NT = (((1,), (1,)), ((), ()))
TN = (((0,), (0,)), ((), ()))


def _dot(a, b, dims=NN, precision=None):
    return lax.dot_general(a, b, dims, precision=precision, preferred_element_type=F32)


def _bdot(a, b, dims=NN):
    return _dot(a.astype(BF16), b.astype(BF16), dims)


def _params(sem, vmem=VMEM_LIMIT):
    return pltpu.CompilerParams(dimension_semantics=sem, vmem_limit_bytes=vmem)


def _rms(x, w):
    return x * lax.rsqrt(jnp.mean(x * x, axis=-1, keepdims=True) + EPS) * w


def _sigmoid(x):
    return 1.0 / (1.0 + jnp.exp(-x))


def _silu(x):
    return x * _sigmoid(x)


def _log_sigmoid(x):
    return jnp.minimum(x, 0.0) - jnp.log1p(jnp.exp(-jnp.abs(x)))


def _softplus(x):
    return jnp.maximum(x, 0.0) + jnp.log1p(jnp.exp(-jnp.abs(x)))


def _rope_lanes(y, cos, sa, sb, half):
    parts = []
    for g in range(y.shape[1] // LANES):
        yg = y[:, g * LANES:(g + 1) * LANES]
        parts.append(yg * cos + pltpu.roll(yg, LANES - half, axis=1) * sa + pltpu.roll(yg, half, axis=1) * sb)
    return parts


def _inproj_kernel(rope_half, x_ref, nw_ref, cos_ref, sa_ref, sb_ref, *refs):
    n = len(rope_half)
    w_refs, out_refs = refs[:n], refs[n:]
    xb = _rms(x_ref[...], nw_ref[...]).astype(BF16)
    for w_ref, o_ref, half in zip(w_refs, out_refs, rope_half):
        y = _dot(xb, w_ref[...])
        if half:
            parts = _rope_lanes(y, cos_ref[...], sa_ref[...], sb_ref[...], half)
            for g, p in enumerate(parts):
                o_ref[:, g * LANES:(g + 1) * LANES] = p
        else:
            o_ref[...] = y


def _inproj(x, nw, tabs, weights, rope_half, tm, name):
    m, d = x.shape
    cos, sa, sb = tabs
    trows = cos.shape[0]
    tb = trows // tm if trows >= tm else 1
    row = lambda i: (i, 0)
    const = lambda i: (0, 0)
    tab_spec = pl.BlockSpec((tm, LANES), lambda i: (i % tb, 0))
    in_specs = [pl.BlockSpec((tm, d), row), pl.BlockSpec((1, d), const), tab_spec, tab_spec, tab_spec]
    in_specs += [pl.BlockSpec(w.shape, const) for w in weights]
    out_shape = [jax.ShapeDtypeStruct((m, w.shape[1]), F32) for w in weights]
    out_specs = [pl.BlockSpec((tm, w.shape[1]), row) for w in weights]
    return pl.pallas_call(
        functools.partial(_inproj_kernel, tuple(rope_half)),
        grid=(m // tm,), in_specs=in_specs, out_specs=out_specs, out_shape=out_shape,
        compiler_params=_params(("arbitrary",)), name=name,
    )(x, nw.reshape(1, d), cos, sa, sb, *weights)


def _pad_lanes(w):
    return jnp.zeros((w.shape[0], LANES), w.dtype).at[:, :w.shape[1]].set(w)


def _rope_tables(pos, head_dim):
    rd = head_dim // ROPE_FRACTION
    half = rd // 2
    inv = ROPE_THETA ** (-jnp.arange(half, dtype=F32) / half)
    reps = LANES // head_dim
    inv_lane = jnp.tile(jnp.concatenate([inv, inv, jnp.zeros((head_dim - rd,), F32)]), reps)
    lane = np.arange(LANES) % head_dim
    ang = pos.astype(F32)[:, None] * inv_lane[None, :]
    sin = jnp.sin(ang)
    cos_t = jnp.cos(ang)
    sa_t = jnp.where(lane < half, -sin, 0.0)
    sb_t = jnp.where((lane >= half) & (lane < rd), sin, 0.0)
    return (cos_t, sa_t, sb_t), half


def _mlstm_kernel(q_ref, k_ref, v_ref, o_ref, gc_ref, bc_ref, nw_ref,
                  h_ref, cout_ref, nout_ref, mout_ref, c_scr, n_scr, m_scr):
    ci = pl.program_id(0)
    ln = CHUNK_A

    @pl.when(ci == 0)
    def _():
        c_scr[...] = jnp.zeros_like(c_scr)
        n_scr[...] = jnp.zeros_like(n_scr)
        m_scr[...] = jnp.zeros_like(m_scr)

    row = lax.broadcasted_iota(jnp.int32, (ln, ln), 0)
    col = lax.broadcasted_iota(jnp.int32, (ln, ln), 1)
    tril = row >= col
    heads = [(bi, h) for bi in range(q_ref.shape[0]) for h in range(NH_A)]
    gates = []
    for bi in range(q_ref.shape[0]):
        gc = gc_ref[bi] + bc_ref[...]
        gr = jnp.transpose(gc)[0:2 * NH_A, :]
        fcum_c = _dot(tril.astype(F32), _log_sigmoid(gc), NN, HIGHEST)
        fcum_r = _dot(_log_sigmoid(gr), (row <= col).astype(F32), NN, HIGHEST)
        gates.append((gc, gr, fcum_c, fcum_r))
    staged = []
    for bi, h in heads:
        gc, gr, fcum_c, fcum_r = gates[bi]
        sl = slice(h * DK_A, (h + 1) * DK_A)
        hs = bi * NH_A + h
        ic_r, ic_c = gr[h:h + 1, :], gc[:, h:h + 1]
        fc_c, fc_r = fcum_c[:, NH_A + h:NH_A + h + 1], fcum_r[NH_A + h:NH_A + h + 1, :]
        m = m_scr[hs:hs + 1, 0:1]
        c = c_scr[hs]
        nrow = n_scr[hs:hs + 1, :]
        qh = q_ref[bi, :, sl]
        kh = k_ref[bi, :, sl] * (DK_A ** -0.5)
        vh = v_ref[bi, :, sl]
        s = _bdot(qh, kh, NT)
        qc = _bdot(qh, c)
        qn = jnp.sum(qh * nrow, axis=1, keepdims=True)
        f_end = fc_c[ln - 1:ln, :]
        m_new = jnp.maximum(f_end + m, jnp.max(f_end - fc_r + ic_r, axis=1, keepdims=True))
        kw = kh * jnp.exp(f_end - fc_c + ic_c - m_new)
        decay = jnp.exp(f_end + m - m_new)
        c_scr[hs] = decay * c + _bdot(kw, vh, TN)
        n_scr[hs:hs + 1, :] = decay * nrow + jnp.sum(kw, axis=0, keepdims=True)
        m_scr[hs:hs + 1, :] = jnp.broadcast_to(m_new, (1, LANES))
        dmat = jnp.where(tril, fc_c - fc_r + ic_r, -jnp.inf)
        inter = fc_c + m
        mt = jnp.maximum(inter, jnp.max(dmat, axis=1, keepdims=True))
        wmat = s * jnp.exp(dmat - mt)
        staged.append((wmat, vh, qc, qn, inter, mt))
    for (bi, h), (wmat, vh, qc, qn, inter, mt) in zip(heads, staged):
        sl = slice(h * DK_A, (h + 1) * DK_A)
        a_inter = jnp.exp(inter - mt)
        num = _bdot(wmat, vh) + a_inter * qc
        den = jnp.sum(wmat, axis=1, keepdims=True) + a_inter * qn
        hh = num / jnp.maximum(jnp.abs(den), jnp.exp(-mt))
        ha = _sigmoid(o_ref[bi, :, sl]) * hh
        h_ref[bi, :, sl] = _rms(ha, nw_ref[:, sl])

    @pl.when(ci == pl.num_programs(0) - 1)
    def _():
        cout_ref[...] = c_scr[...]
        nout_ref[...] = n_scr[...]
        mout_ref[...] = m_scr[...]


def _mlstm_prompt(a_main, g_col, b_gates, norm_w, batch, seq):
    ln = CHUNK_A
    nc = seq // ln
    w = NH_A * DK_A
    nst = batch * NH_A
    a3 = a_main.reshape(batch, seq, 4 * w)
    blk = lambda j: pl.BlockSpec((batch, ln, w), lambda c, j=j: (0, c, j))
    bc = jnp.zeros((1, LANES), F32).at[0, :2 * NH_A].set(b_gates)
    const = lambda c: (0, 0)
    ha, c1, n1, m1 = pl.pallas_call(
        _mlstm_kernel, grid=(nc,),
        in_specs=[blk(0), blk(1), blk(2), blk(3),
                  pl.BlockSpec((batch, ln, LANES), lambda c: (0, c, 0)),
                  pl.BlockSpec((1, LANES), const), pl.BlockSpec((1, w), const)],
        out_specs=[pl.BlockSpec((batch, ln, w), lambda c: (0, c, 0)),
                   pl.BlockSpec((nst, DK_A, DV_A), lambda c: (0, 0, 0)),
                   pl.BlockSpec((nst, DK_A), const), pl.BlockSpec((nst, LANES), const)],
        out_shape=[jax.ShapeDtypeStruct((batch, seq, w), F32),
                   jax.ShapeDtypeStruct((nst, DK_A, DV_A), F32),
                   jax.ShapeDtypeStruct((nst, DK_A), F32),
                   jax.ShapeDtypeStruct((nst, LANES), F32)],
        scratch_shapes=[pltpu.VMEM((nst, DK_A, DV_A), F32), pltpu.VMEM((nst, LANES), F32),
                        pltpu.VMEM((nst, LANES), F32)],
        compiler_params=_params(("arbitrary",)), name="mlstm_prompt",
    )(a3, a3, a3, a3, g_col.reshape(batch, seq, LANES), bc, norm_w.reshape(1, w))
    return (ha.reshape(batch * seq, w), c1.reshape(batch, NH_A, DK_A, DV_A), n1.reshape(batch, NH_A, DK_A),
            m1.reshape(batch, NH_A, LANES))


def _mlstm_step_kernel(a_ref, qc_ref, kc_ref, g_ref, b_ref, m_ref, c_ref, n_ref, nw_ref,
                       h_ref, cout_ref, nout_ref, mout_ref):
    g = g_ref[0] + b_ref[...]
    lf_all = _log_sigmoid(g)
    w = NH_A * DK_A
    scale = DK_A ** -0.5
    for h in range(NH_A):
        sl = slice(h * DK_A, (h + 1) * DK_A)
        q_row = a_ref[0, :, sl]
        k_row = a_ref[0, :, w + h * DK_A:w + (h + 1) * DK_A] * scale
        v_row = a_ref[0, :, 2 * w + h * DV_A:2 * w + (h + 1) * DV_A]
        o_row = a_ref[0, :, 3 * w + h * DV_A:3 * w + (h + 1) * DV_A]
        q_col = qc_ref[0, sl, :]
        k_col = kc_ref[0, sl, :] * scale
        c = c_ref[0, h]
        n_row = n_ref[0, h:h + 1, :]
        m = m_ref[0, :, h:h + 1]
        ic = g[:, h:h + 1]
        lf = lf_all[:, NH_A + h:NH_A + h + 1]
        inter = lf + m
        mt = jnp.maximum(inter, ic)
        wm = jnp.sum(q_row * k_row, axis=1, keepdims=True) * jnp.exp(ic - mt)
        a_inter = jnp.exp(inter - mt)
        num = wm * v_row + a_inter * jnp.sum(q_col * c, axis=0, keepdims=True)
        den = wm + a_inter * jnp.sum(q_row * n_row, axis=1, keepdims=True)
        hh = num / jnp.maximum(jnp.abs(den), jnp.exp(-mt))
        m_new = jnp.maximum(inter, ic)
        wgt = jnp.exp(ic - m_new)
        decay = jnp.exp(inter - m_new)
        cout_ref[0, h] = decay * c + (wgt * k_col) * v_row
        nout_ref[0, h:h + 1, :] = decay * n_row + wgt * k_row
        mout_ref[0, h:h + 1, :] = jnp.broadcast_to(m_new, (1, LANES))
        ha = _sigmoid(o_row) * hh
        h_ref[0, :, sl] = _rms(ha, nw_ref[:, sl])


def _mlstm_step(a_main, g_col, b_gates, norm_w, c0, n0, m0):
    ns = a_main.shape[0]
    w = NH_A * DK_A
    a3 = a_main.reshape(ns, 1, 4 * w)
    q_col = a_main[:, :w].reshape(ns, w, 1)
    k_col = a_main[:, w:2 * w].reshape(ns, w, 1)
    bc = jnp.zeros((1, LANES), F32).at[0, :2 * NH_A].set(b_gates)
    s3 = lambda s: (s, 0, 0)
    s4 = lambda s: (s, 0, 0, 0)
    const = lambda s: (0, 0)
    return pl.pallas_call(
        _mlstm_step_kernel, grid=(ns,),
        in_specs=[pl.BlockSpec((1, 1, 4 * w), s3), pl.BlockSpec((1, w, 1), s3), pl.BlockSpec((1, w, 1), s3),
                  pl.BlockSpec((1, 1, LANES), s3), pl.BlockSpec((1, LANES), const),
                  pl.BlockSpec((1, 1, NH_A), s3), pl.BlockSpec((1, NH_A, DK_A, DV_A), s4),
                  pl.BlockSpec((1, NH_A, DK_A), s3), pl.BlockSpec((1, w), const)],
        out_specs=[pl.BlockSpec((1, 1, w), s3), pl.BlockSpec((1, NH_A, DK_A, DV_A), s4),
                   pl.BlockSpec((1, NH_A, DK_A), s3), pl.BlockSpec((1, NH_A, LANES), s3)],
        out_shape=[jax.ShapeDtypeStruct((ns, 1, w), F32), jax.ShapeDtypeStruct((ns, NH_A, DK_A, DV_A), F32),
                   jax.ShapeDtypeStruct((ns, NH_A, DK_A), F32), jax.ShapeDtypeStruct((ns, NH_A, LANES), F32)],
        compiler_params=_params(("arbitrary",)), name="mlstm_step",
    )(a3, q_col, k_col, g_col.reshape(ns, 1, LANES), bc, m0.reshape(ns, 1, NH_A), c0, n0, norm_w.reshape(1, w))


def _top_blocks(bs, topk):
    blk_id = lax.broadcasted_iota(jnp.int32, bs.shape, 0).astype(F32)
    sel = jnp.zeros(bs.shape, F32)
    for _ in range(topk):
        mx = jnp.max(bs, axis=0, keepdims=True)
        idx = jnp.min(jnp.where(bs == mx, blk_id, float(bs.shape[0])), axis=0, keepdims=True)
        pick = (blk_id == idx) & (mx > -jnp.inf)
        sel = jnp.where(pick, 1.0, sel)
        bs = jnp.where(pick, -jnp.inf, bs)
    return sel


MOBA_ONES = 16


def _moba_kernel(n_full, q_ref, k_ref, v_ref, o_ref, kmean_scr, kb_scr, vt_scr, sel_scr, m_scr, acc_scr,
                 sa_scr, sb_scr, ga_scr, gb_scr):
    qi = pl.program_id(2)
    blk = MOBA_BLOCK
    grp = NH_B // KVH_B

    @pl.when(qi == 0)
    def _():
        kmean_scr[...] = jnp.zeros_like(kmean_scr)
        for n in range(n_full):
            kblk = k_ref[n * blk:(n + 1) * blk, :]
            kmean_scr[n:n + 1, :] = jnp.mean(kblk, axis=0, keepdims=True)
            kb_scr[n] = kblk.astype(BF16)
            vt_scr[n, 0:HD_B, :] = jnp.transpose(v_ref[n * blk:(n + 1) * blk, :]).astype(BF16)
            vt_scr[n, HD_B:HD_B + MOBA_ONES, :] = jnp.ones((MOBA_ONES, blk), BF16)

    q = jnp.concatenate([q_ref[:, g * HD_B:(g + 1) * HD_B] for g in range(grp)], axis=0)
    rows = grp * blk
    bs = _dot(kmean_scr[...], q, NT, HIGHEST)
    bs = jnp.where(lax.broadcasted_iota(jnp.int32, bs.shape, 0) < qi, bs, -jnp.inf)
    sel_scr[...] = _top_blocks(bs, min(MOBA_TOPK, n_full))
    qs = (q * (HD_B ** -0.5 * math.log2(math.e))).astype(BF16)

    key = lax.broadcasted_iota(jnp.int32, (blk, rows), 0)
    qpos = lax.broadcasted_iota(jnp.int32, (blk, rows), 1) % blk
    causal = key <= qpos

    def score(first, own_group, s_scr, g_scr):
        gmax = None
        for g in range(MOBA_GROUP):
            b = first + g
            ok = sel_scr[pl.ds(b, 1), :] > 0.0
            if own_group:
                ok = ((b < qi) & ok) | ((b == qi) & causal)
            s = jnp.where(ok, _dot(kb_scr[b], qs, NT), -jnp.inf)
            s_scr[g] = s
            mx = jnp.max(s, axis=0, keepdims=True)
            gmax = mx if gmax is None else jnp.maximum(gmax, mx)
        g_scr[...] = gmax

    def update(first, s_scr, g_scr):
        m_old = m_scr[...]
        m_new = jnp.maximum(m_old, g_scr[...])
        acc_new = jnp.exp2(m_old - m_new) * acc_scr[...]
        for g in range(MOBA_GROUP):
            p = jnp.exp2(s_scr[g] - m_new)
            acc_new = acc_new + _dot(vt_scr[first + g], p.astype(BF16))
        m_scr[...] = m_new
        acc_scr[...] = acc_new

    m_scr[...] = jnp.full(m_scr.shape, -jnp.inf, F32)
    acc_scr[...] = jnp.zeros(acc_scr.shape, F32)
    n_past = qi // MOBA_GROUP
    own_first = n_past * MOBA_GROUP
    score(own_first, True, sa_scr, ga_scr)

    def body(i, carry):
        f0 = 2 * i * MOBA_GROUP
        f1 = f0 + MOBA_GROUP
        in_a = jnp.where(i == 0, own_first, f0 - MOBA_GROUP)
        score(f0, False, sb_scr, gb_scr)
        update(in_a, sa_scr, ga_scr)
        score(f1, False, sa_scr, ga_scr)
        update(f0, sb_scr, gb_scr)
        return carry

    n_pair = n_past // 2
    lax.fori_loop(0, n_pair, body, 0)
    in_a = jnp.where(n_pair == 0, own_first, (2 * n_pair - 1) * MOBA_GROUP)

    @pl.when(n_past % 2 == 1)
    def _():
        last = (n_past - 1) * MOBA_GROUP
        score(last, False, sb_scr, gb_scr)
        update(in_a, sa_scr, ga_scr)
        update(last, sb_scr, gb_scr)

    @pl.when(n_past % 2 == 0)
    def _():
        update(in_a, sa_scr, ga_scr)

    out = acc_scr[0:HD_B, :] / acc_scr[HD_B:HD_B + 1, :]
    for g in range(grp):
        o_ref[:, g * HD_B:(g + 1) * HD_B] = jnp.transpose(out[:, g * blk:(g + 1) * blk])


MOBA_GROUP = 4


def _moba_prompt(qb, kb, vb, batch, seq):
    m = qb.shape[0]
    blk = MOBA_BLOCK
    nq = seq // blk
    assert seq % (blk * MOBA_GROUP) == 0
    grp = NH_B // KVH_B
    return pl.pallas_call(
        functools.partial(_moba_kernel, seq // blk), grid=(batch, KVH_B, nq),
        in_specs=[pl.BlockSpec((blk, grp * HD_B), lambda b, j, i: (b * nq + i, j)),
                  pl.BlockSpec((seq, HD_B), lambda b, j, i: (b, j)),
                  pl.BlockSpec((seq, HD_B), lambda b, j, i: (b, j))],
        out_specs=pl.BlockSpec((blk, grp * HD_B), lambda b, j, i: (b * nq + i, j)),
        out_shape=jax.ShapeDtypeStruct((m, NH_B * HD_B), F32),
        scratch_shapes=[pltpu.VMEM((-(-nq // SUBLANES) * SUBLANES, HD_B), F32),
                        pltpu.VMEM((nq, blk, HD_B), BF16), pltpu.VMEM((nq, HD_B + MOBA_ONES, blk), BF16),
                        pltpu.VMEM((-(-nq // SUBLANES) * SUBLANES, grp * blk), F32),
                        pltpu.VMEM((1, grp * blk), F32), pltpu.VMEM((HD_B + MOBA_ONES, grp * blk), F32),
                        pltpu.VMEM((MOBA_GROUP, blk, grp * blk), F32), pltpu.VMEM((MOBA_GROUP, blk, grp * blk), F32),
                        pltpu.VMEM((1, grp * blk), F32), pltpu.VMEM((1, grp * blk), F32)],
        compiler_params=_params(("arbitrary", "arbitrary", "arbitrary")), name="moba_prompt",
    )(qb, kb, vb)


MOBA_PAGES_PER_STEP = 64


def _moba_select_kernel(n_steps, pt_ref, q_ref, *refs):
    pages, (kmean_ref, sel_ref) = refs[:-2], refs[-2:]
    j = pl.program_id(1)
    page = pages[0].shape[1] // KVH_B

    def page_sum(p):
        heads = [jnp.sum(p[0, pl.ds(kv, page, stride=KVH_B), :], axis=0, keepdims=True) for kv in range(KVH_B)]
        return jnp.concatenate(heads, axis=1)

    sums = [page_sum(p) for p in pages]
    nblk = len(pages) // 2
    per_blk = [(sums[2 * i] + sums[2 * i + 1]) * (1.0 / MOBA_BLOCK) for i in range(nblk)]
    kmean_ref[0, pl.ds(pl.multiple_of(j * nblk, SUBLANES), nblk), :] = jnp.concatenate(per_blk, axis=0)

    @pl.when(j == n_steps - 1)
    def _():
        grp = NH_B // KVH_B
        nb = kmean_ref.shape[1]
        rowi = lax.broadcasted_iota(jnp.int32, (nb, 1), 0).astype(F32)
        orow = lax.broadcasted_iota(jnp.int32, (SUBLANES, LANES), 0)
        olane = lax.broadcasted_iota(jnp.int32, (SUBLANES, LANES), 1)
        out = jnp.zeros((SUBLANES, LANES), F32)
        for h in range(NH_B):
            kv = h // grp
            km = kmean_ref[0, :, kv * HD_B:(kv + 1) * HD_B]
            qh = q_ref[0, :, h * HD_B:(h + 1) * HD_B]
            bs = jnp.sum(km * qh, axis=1, keepdims=True)
            for r in range(MOBA_TOPK):
                mx = jnp.max(bs, axis=0, keepdims=True)
                idx = jnp.min(jnp.where(bs == mx, rowi, float(nb)), axis=0, keepdims=True)
                out = jnp.where((orow == h) & (olane == r), idx, out)
                bs = jnp.where(rowi == idx, -jnp.inf, bs)
        sel_ref[0] = out.astype(jnp.int32)


def _moba_select(qb, k_cache, page_table):
    ns, n_pages = page_table.shape
    page = k_cache.shape[1]
    pps = min(MOBA_PAGES_PER_STEP, n_pages)
    assert 2 * page == MOBA_BLOCK and n_pages % pps == 0 and pps % (2 * SUBLANES) == 0
    n_steps = n_pages // pps
    kw = KVH_B * HD_B
    kc = k_cache.reshape(k_cache.shape[0], page * KVH_B, HD_B)
    page_spec = lambda i: pl.BlockSpec(
        (1, page * KVH_B, HD_B), lambda s, j, pt, i=i: (pt[s * n_pages + j * pps + i], 0, 0))
    nb = n_pages * page // MOBA_BLOCK
    grid_spec = pltpu.PrefetchScalarGridSpec(
        num_scalar_prefetch=1, grid=(ns, n_steps),
        in_specs=[pl.BlockSpec((1, 1, NH_B * HD_B), lambda s, j, pt: (s, 0, 0))]
        + [page_spec(i) for i in range(pps)],
        out_specs=[pl.BlockSpec((1, nb, kw), lambda s, j, pt: (s, 0, 0)),
                   pl.BlockSpec((1, SUBLANES, LANES), lambda s, j, pt: (s, 0, 0))])
    return pl.pallas_call(
        functools.partial(_moba_select_kernel, n_steps), grid_spec=grid_spec,
        out_shape=[jax.ShapeDtypeStruct((ns, nb, kw), F32), jax.ShapeDtypeStruct((ns, SUBLANES, LANES), jnp.int32)],
        compiler_params=_params(("arbitrary", "arbitrary")), name="moba_select",
    )(page_table.reshape(-1), qb.reshape(ns, 1, NH_B * HD_B), *([kc] * pps))


def _moba_step_kernel(sel_ref, pt_ref, q_ref, kn_ref, vn_ref, *refs):
    pages, o_ref, (m_scr, l_scr, acc_scr) = refs[:4 * NH_B], refs[4 * NH_B], refs[4 * NH_B + 1:]
    r = pl.program_id(1)
    scale = HD_B ** -0.5
    grp = NH_B // KVH_B
    page = pages[0].shape[1] // KVH_B
    first = r == 0
    for h in range(NH_B):
        kv = h // grp
        row = slice(h, h + 1)
        q = q_ref[0, :, h * HD_B:(h + 1) * HD_B]
        ka_ref, kb_ref, va_ref, vb_ref = pages[4 * h:4 * h + 4]
        s0 = jnp.sum(q * kn_ref[0, :, kv * HD_B:(kv + 1) * HD_B], axis=1, keepdims=True) * scale
        m_old = jnp.where(first, s0, m_scr[row, 0:1])
        l_old = jnp.where(first, 1.0, l_scr[row, 0:1])
        acc_old = jnp.where(first, vn_ref[0, :, kv * HD_B:(kv + 1) * HD_B], acc_scr[row, :])
        rows = pl.ds(kv, page, stride=KVH_B)
        kk = jnp.concatenate([ka_ref[0, rows, :], kb_ref[0, rows, :]], axis=0)
        vv = jnp.concatenate([va_ref[0, rows, :], vb_ref[0, rows, :]], axis=0)
        s = jnp.sum(kk * q, axis=1, keepdims=True) * scale
        m_new = jnp.maximum(m_old, jnp.max(s, axis=0, keepdims=True))
        alpha = jnp.exp(m_old - m_new)
        p = jnp.exp(s - m_new)
        l_new = alpha * l_old + jnp.sum(p, axis=0, keepdims=True)
        acc_new = alpha * acc_old + jnp.sum(p * vv, axis=0, keepdims=True)
        m_scr[row, :] = jnp.broadcast_to(m_new, (1, LANES))
        l_scr[row, :] = jnp.broadcast_to(l_new, (1, LANES))
        acc_scr[row, :] = acc_new
        o_ref[0, :, h * HD_B:(h + 1) * HD_B] = acc_new / l_new


def _moba_step(qb, k_new, v_new, k_cache, v_cache, page_table, sel):
    ns, n_pages = page_table.shape
    page = k_cache.shape[1]
    kw = KVH_B * HD_B
    kc = k_cache.reshape(k_cache.shape[0], page * KVH_B, HD_B)
    vc = v_cache.reshape(v_cache.shape[0], page * KVH_B, HD_B)

    def page_spec(h, half):
        def imap(s, r, sel_r, pt_r):
            blk = sel_r[(s * NH_B + h) * MOBA_TOPK + r]
            return (pt_r[s * n_pages + 2 * blk + half], 0, 0)
        return pl.BlockSpec((1, page * KVH_B, HD_B), imap)

    seq_row = lambda w: pl.BlockSpec((1, 1, w), lambda s, r, a, b: (s, 0, 0))
    page_specs, page_args = [], []
    for h in range(NH_B):
        page_specs += [page_spec(h, 0), page_spec(h, 1), page_spec(h, 0), page_spec(h, 1)]
        page_args += [kc, kc, vc, vc]
    grid_spec = pltpu.PrefetchScalarGridSpec(
        num_scalar_prefetch=2, grid=(ns, MOBA_TOPK),
        in_specs=[seq_row(NH_B * HD_B), seq_row(kw), seq_row(kw)] + page_specs,
        out_specs=seq_row(NH_B * HD_B),
        scratch_shapes=[pltpu.VMEM((SUBLANES, LANES), F32), pltpu.VMEM((SUBLANES, LANES), F32),
                        pltpu.VMEM((SUBLANES, HD_B), F32)])
    out = pl.pallas_call(
        _moba_step_kernel, grid_spec=grid_spec,
        out_shape=jax.ShapeDtypeStruct((ns, 1, NH_B * HD_B), F32),
        compiler_params=_params(("arbitrary", "arbitrary")), name="moba_step",
    )(sel.reshape(-1), page_table.reshape(-1), qb.reshape(ns, 1, NH_B * HD_B),
      k_new.reshape(ns, 1, kw), v_new.reshape(ns, 1, kw), *page_args)
    return out.reshape(ns, NH_B * HD_B)


def _outproj_kernel(a_ref, b_ref, wa_ref, wb_ref, x_ref, o_ref):
    o_ref[...] = x_ref[...] + _bdot(a_ref[...], wa_ref[...]) + _bdot(b_ref[...], wb_ref[...])


def _outproj(a, b, wa, wb, x, tm, name):
    m, d = x.shape
    row = lambda i: (i, 0)
    const = lambda i: (0, 0)
    return pl.pallas_call(
        _outproj_kernel, grid=(m // tm,),
        in_specs=[pl.BlockSpec((tm, a.shape[1]), row), pl.BlockSpec((tm, b.shape[1]), row),
                  pl.BlockSpec(wa.shape, const), pl.BlockSpec(wb.shape, const), pl.BlockSpec((tm, d), row)],
        out_specs=pl.BlockSpec((tm, d), row), out_shape=jax.ShapeDtypeStruct((m, d), F32),
        compiler_params=_params(("arbitrary",)), name=name,
    )(a, b, wa, wb, x)


FFN_PHASES = 4


def _ffn_kernel(final, x_ref, nw_ref, wup_ref, cw_ref, cb_ref, wdn_ref, nf_ref,
                o_ref, cs_ref, ext_scr, act_scr):
    tm = x_ref.shape[0]
    pad = SUBLANES
    rows = tm // FFN_PHASES
    nlane = FFN_CW // LANES
    gate_blocks = D_FF // LANES

    @pl.when(pl.program_id(1) == 0)
    def _():
        ext_scr[:, 0:pad, :] = jnp.zeros((ext_scr.shape[0], pad, LANES), F32)

    x = x_ref[...]
    xb = _rms(x, nw_ref[...]).astype(BF16)
    acc = x

    def up(c):
        for half, col0 in enumerate((c * FFN_CW, D_FF + c * FFN_CW)):
            u = _dot(xb, wup_ref[:, col0:col0 + FFN_CW])
            for i in range(nlane):
                ext_scr[half * gate_blocks + c * nlane + i, pad:pad + tm, :] = u[:, i * LANES:(i + 1) * LANES]

    def conv(k, p):
        cols = slice(k * LANES, (k + 1) * LANES)
        y = cb_ref[:, cols]
        for j in range(FFN_CONV):
            start = pad + p - (FFN_CONV - 1) + j
            y = y + cw_ref[j:j + 1, cols] * ext_scr[k, pl.ds(start, rows, stride=FFN_PHASES), :]
        return y

    nch = D_FF // FFN_CW
    up(0)
    for c in range(nch):
        if c + 1 < nch:
            up(c + 1)
        for i in range(nlane):
            kg = c * nlane + i
            kv = gate_blocks + kg
            for p in range(FFN_PHASES):
                act_scr[c % 2, i, pl.ds(p, rows, stride=FFN_PHASES), :] = _silu(conv(kg, p)) * conv(kv, p)
            for k in (kg, kv):
                ext_scr[k, 0:pad, :] = ext_scr[k, tm:tm + pad, :]
        act = jnp.concatenate([act_scr[c % 2, i] for i in range(nlane)], axis=1).astype(BF16)
        acc = acc + _dot(act, wdn_ref[c * FFN_CW:(c + 1) * FFN_CW, :])
    if final:
        acc = _rms(acc, nf_ref[...])
    o_ref[...] = acc
    for k in range(ext_scr.shape[0]):
        cs_ref[0, :, k * LANES:(k + 1) * LANES] = ext_scr[k, 0:pad, :]


def _ffn_prompt(x, nw, w_up, conv_w, conv_b, w_down, nf, final, batch, seq, tm, name):
    m, d = x.shape
    nt = seq // tm
    row = lambda b, i: (b * nt + i, 0)
    const = lambda b, i: (0, 0)
    out, cs = pl.pallas_call(
        functools.partial(_ffn_kernel, final), grid=(batch, nt),
        in_specs=[pl.BlockSpec((tm, d), row), pl.BlockSpec((1, d), const), pl.BlockSpec(w_up.shape, const),
                  pl.BlockSpec(conv_w.shape, const), pl.BlockSpec((1, 2 * D_FF), const),
                  pl.BlockSpec(w_down.shape, const), pl.BlockSpec((1, d), const)],
        out_specs=[pl.BlockSpec((tm, d), row), pl.BlockSpec((1, SUBLANES, 2 * D_FF), lambda b, i: (b, 0, 0))],
        out_shape=[jax.ShapeDtypeStruct((m, d), F32), jax.ShapeDtypeStruct((batch, SUBLANES, 2 * D_FF), F32)],
        scratch_shapes=[pltpu.VMEM((2 * D_FF // LANES, tm + SUBLANES, LANES), F32),
                        pltpu.VMEM((2, FFN_CW // LANES, tm, LANES), F32)],
        compiler_params=_params(("arbitrary", "arbitrary")), name=name,
    )(x, nw.reshape(1, d), w_up, conv_w, conv_b.reshape(1, -1), w_down, nf.reshape(1, d))
    return out, cs[:, SUBLANES - (FFN_CONV - 1):, :]


def _ffn_step_kernel(final, x_ref, nw_ref, wg_ref, wv_ref, g0_ref, g1_ref, v0_ref, v1_ref,
                     cwg_ref, cwv_ref, cbg_ref, cbv_ref, wdn_ref, nf_ref, o_ref, ug_ref, uv_ref, acc_scr):
    c = pl.program_id(0)

    @pl.when(c == 0)
    def _():
        acc_scr[...] = x_ref[...]

    xb = _rms(x_ref[...], nw_ref[...]).astype(BF16)
    ug = _dot(xb, wg_ref[...])
    uv = _dot(xb, wv_ref[...])
    ug_ref[...] = ug
    uv_ref[...] = uv
    gate = cbg_ref[...] + cwg_ref[0:1, :] * g0_ref[...] + cwg_ref[1:2, :] * g1_ref[...] + cwg_ref[2:3, :] * ug
    val = cbv_ref[...] + cwv_ref[0:1, :] * v0_ref[...] + cwv_ref[1:2, :] * v1_ref[...] + cwv_ref[2:3, :] * uv
    acc_scr[...] += _dot((_silu(gate) * val).astype(BF16), wdn_ref[...])

    @pl.when(c == pl.num_programs(0) - 1)
    def _():
        acc = acc_scr[...]
        o_ref[...] = _rms(acc, nf_ref[...]) if final else acc


def _ffn_step(x, nw, w_up, conv_w, conv_b, w_down, nf, final, buf, name):
    ns, d = x.shape
    cw = FFN_CW
    nch = D_FF // cw
    ntot = 2 * D_FF // cw
    bufw = buf.reshape(ns, 2 * 2 * D_FF)
    cb = conv_b.reshape(1, -1)
    const = lambda c: (0, 0)
    gcol = lambda c: (0, c)
    vcol = lambda c: (0, nch + c)
    out, u_g, u_v = pl.pallas_call(
        functools.partial(_ffn_step_kernel, final), grid=(nch,),
        in_specs=[pl.BlockSpec((ns, d), const), pl.BlockSpec((1, d), const),
                  pl.BlockSpec((d, cw), gcol), pl.BlockSpec((d, cw), vcol),
                  pl.BlockSpec((ns, cw), gcol), pl.BlockSpec((ns, cw), lambda c: (0, ntot + c)),
                  pl.BlockSpec((ns, cw), vcol), pl.BlockSpec((ns, cw), lambda c: (0, ntot + nch + c)),
                  pl.BlockSpec((FFN_CONV, cw), gcol), pl.BlockSpec((FFN_CONV, cw), vcol),
                  pl.BlockSpec((1, cw), gcol), pl.BlockSpec((1, cw), vcol),
                  pl.BlockSpec((cw, d), lambda c: (c, 0)), pl.BlockSpec((1, d), const)],
        out_specs=[pl.BlockSpec((ns, d), const), pl.BlockSpec((ns, cw), gcol), pl.BlockSpec((ns, cw), gcol)],
        out_shape=[jax.ShapeDtypeStruct((ns, d), F32), jax.ShapeDtypeStruct((ns, D_FF), F32),
                   jax.ShapeDtypeStruct((ns, D_FF), F32)],
        scratch_shapes=[pltpu.VMEM((ns, d), F32)],
        compiler_params=_params(("arbitrary",)), name=name,
    )(x, nw.reshape(1, d), w_up, w_up, bufw, bufw, bufw, bufw, conv_w, conv_w, cb, cb, w_down, nf.reshape(1, d))
    u = jnp.concatenate([u_g, u_v], axis=1)
    return out, jnp.stack([buf[:, 1], u], axis=1)


def _ssd_kernel(z_ref, xbc_ref, dc_ref, cw_ref, cb_ref, dbc_ref, alc_ref,
                dsk_ref, nw_ref, y_ref, hout_ref, cs_ref, h_scr, carry_scr, ext_scr, y_scr):
    ci = pl.program_id(1)
    ln = CHUNK_C
    pad = SUBLANES

    @pl.when(ci == 0)
    def _():
        h_scr[...] = jnp.zeros_like(h_scr)
        carry_scr[...] = jnp.zeros_like(carry_scr)

    raw = xbc_ref[...]
    ext_scr[0:pad, :] = carry_scr[...]
    ext_scr[pad:pad + ln, :] = raw
    carry_scr[...] = ext_scr[ln:ln + pad, :]
    y = cb_ref[...] + cw_ref[CONV_C - 1:CONV_C, :] * raw
    for j in range(CONV_C - 1):
        off = pad - (CONV_C - 1) + j
        y = y + cw_ref[j:j + 1, :] * ext_scr[off:off + ln, :]
    xbc = _silu(y)
    bw = NG_C * DS_C
    xs, bm, cm = xbc[:, :DI_C], xbc[:, DI_C:DI_C + bw], xbc[:, DI_C + bw:]

    row = lax.broadcasted_iota(jnp.int32, (ln, ln), 0)
    col = lax.broadcasted_iota(jnp.int32, (ln, ln), 1)
    tril = row >= col
    lane_lo = lax.broadcasted_iota(jnp.int32, (ln, LANES), 1) < HD_C
    row_lo = lax.broadcasted_iota(jnp.int32, (2 * HD_C, DS_C), 0) < HD_C
    dt_c = _softplus(dc_ref[...] + dbc_ref[...])
    ac_c = dt_c * (-jnp.exp(alc_ref[...]))
    dt_r = jnp.transpose(dt_c)[0:NH_C, :]
    ac_r = jnp.transpose(ac_c)[0:NH_C, :]
    acum_c = _dot(tril.astype(F32), ac_c, NN, HIGHEST)
    acum_r = _dot(ac_r, (row <= col).astype(F32), NN, HIGHEST)
    hpg = NH_C // NG_C
    for g in range(NG_C):
        bg = bm[:, g * DS_C:(g + 1) * DS_C]
        cg = cm[:, g * DS_C:(g + 1) * DS_C]
        cb_mat = _bdot(cg, bg, NT)
        for pi in range(hpg // 2):
            pair = g * (hpg // 2) + pi
            h0, h1 = 2 * pair, 2 * pair + 1
            xp = xs[:, pair * LANES:(pair + 1) * LANES]
            hp = h_scr[pair]
            a0, a1 = acum_c[:, h0:h0 + 1], acum_c[:, h1:h1 + 1]

            def mmat(h, a_col):
                decay = jnp.exp(jnp.where(tril, a_col - acum_r[h:h + 1, :], -jnp.inf))
                return cb_mat * decay * dt_r[h:h + 1, :]

            yy = _bdot(mmat(h0, a0), jnp.where(lane_lo, xp, 0.0)) + _bdot(mmat(h1, a1), jnp.where(lane_lo, 0.0, xp))
            yy = yy + jnp.where(lane_lo, jnp.exp(a0), jnp.exp(a1)) * _bdot(cg, hp, NT)
            e0, e1 = a0[ln - 1:ln, :], a1[ln - 1:ln, :]
            wend = jnp.where(lane_lo, jnp.exp(e0 - a0) * dt_c[:, h0:h0 + 1], jnp.exp(e1 - a1) * dt_c[:, h1:h1 + 1])
            h_scr[pair] = jnp.where(row_lo, jnp.exp(e0), jnp.exp(e1)) * hp + _bdot(xp * wend, bg, TN)
            y_scr[:, pair * LANES:(pair + 1) * LANES] = yy + dsk_ref[:, pair * LANES:(pair + 1) * LANES] * xp
    gated = y_scr[...] * _silu(z_ref[...])
    gw = DI_C // NG_C
    for g in range(NG_C):
        sl = slice(g * gw, (g + 1) * gw)
        y_ref[:, sl] = _rms(gated[:, sl], nw_ref[:, sl])
    cs_ref[0] = carry_scr[...]

    @pl.when(ci == pl.num_programs(1) - 1)
    def _():
        hout_ref[0] = h_scr[...]


def _head_lanes(v, width):
    return jnp.repeat(v.astype(F32), width).reshape(1, -1)


def _ssd_prompt(z, xbc, d_col, conv_w, conv_b, dt_bias, a_log, d_skip, norm_w, batch, seq):
    m = z.shape[0]
    ln = CHUNK_C
    nc = seq // ln
    npair = NH_C // 2
    lane8 = lambda v: jnp.zeros((1, LANES), F32).at[0, :NH_C].set(v)
    row = lambda b, c: (b * nc + c, 0)
    const = lambda b, c: (0, 0)
    y, h1, cs = pl.pallas_call(
        _ssd_kernel, grid=(batch, nc),
        in_specs=[pl.BlockSpec((ln, DI_C), row), pl.BlockSpec((ln, CONV_DIM_C), row),
                  pl.BlockSpec((ln, LANES), row),
                  pl.BlockSpec((CONV_C, CONV_DIM_C), const), pl.BlockSpec((1, CONV_DIM_C), const),
                  pl.BlockSpec((1, LANES), const), pl.BlockSpec((1, LANES), const),
                  pl.BlockSpec((1, DI_C), const), pl.BlockSpec((1, DI_C), const)],
        out_specs=[pl.BlockSpec((ln, DI_C), row),
                   pl.BlockSpec((1, npair, 2 * HD_C, DS_C), lambda b, c: (b, 0, 0, 0)),
                   pl.BlockSpec((1, SUBLANES, CONV_DIM_C), lambda b, c: (b, 0, 0))],
        out_shape=[jax.ShapeDtypeStruct((m, DI_C), F32),
                   jax.ShapeDtypeStruct((batch, npair, 2 * HD_C, DS_C), F32),
                   jax.ShapeDtypeStruct((batch, SUBLANES, CONV_DIM_C), F32)],
        scratch_shapes=[pltpu.VMEM((npair, 2 * HD_C, DS_C), F32), pltpu.VMEM((SUBLANES, CONV_DIM_C), F32),
                        pltpu.VMEM((ln + SUBLANES, CONV_DIM_C), F32), pltpu.VMEM((ln, DI_C), F32)],
        compiler_params=_params(("arbitrary", "arbitrary")), name="ssd_prompt",
    )(z, xbc, d_col, conv_w, conv_b.reshape(1, -1), lane8(dt_bias), lane8(a_log),
      _head_lanes(d_skip, HD_C), norm_w.reshape(1, DI_C))
    return y, h1.reshape(batch, NH_C, HD_C, DS_C), cs[:, SUBLANES - (CONV_C - 1):, :]


def _lane_bcast_col(row_vec):
    return jnp.transpose(jnp.broadcast_to(row_vec, (LANES, LANES)))


def _ssd_step_kernel(z_ref, x_ref, cs_ref, d_ref, cw_ref, cb_ref, db_ref, al_ref, dsk_ref, nw_ref, h_ref,
                     y_ref, hout_ref):
    y = cb_ref[...] + cw_ref[CONV_C - 1:CONV_C, :] * x_ref[0]
    for j in range(CONV_C - 1):
        y = y + cw_ref[j:j + 1, :] * cs_ref[0, j:j + 1, :]
    xbc = _silu(y)
    bw = NG_C * DS_C
    dt = _softplus(d_ref[0] + db_ref[...])
    ea = jnp.exp(dt * (-jnp.exp(al_ref[...])))
    row_lo = lax.broadcasted_iota(jnp.int32, (LANES, 1), 0) < HD_C
    lane_lo = lax.broadcasted_iota(jnp.int32, (1, LANES), 1) < HD_C
    hpg = NH_C // NG_C
    parts = []
    for pair in range(NH_C // 2):
        g = (2 * pair) // hpg
        h0, h1 = 2 * pair, 2 * pair + 1
        xp = xbc[:, pair * LANES:(pair + 1) * LANES]
        bg = xbc[:, DI_C + g * DS_C:DI_C + (g + 1) * DS_C]
        cg = xbc[:, DI_C + bw + g * DS_C:DI_C + bw + (g + 1) * DS_C]
        hp = h_ref[0, pair]
        x_col = _lane_bcast_col(xp)
        ea_col = jnp.where(row_lo, ea[:, h0:h0 + 1], ea[:, h1:h1 + 1])
        dt_col = jnp.where(row_lo, dt[:, h0:h0 + 1], dt[:, h1:h1 + 1])
        hout_ref[0, pair] = ea_col * hp + (dt_col * x_col) * bg
        y_inter = jnp.sum(hp * cg, axis=1, keepdims=True)
        y_col = ea_col * y_inter
        y_row = jnp.transpose(jnp.broadcast_to(y_col, (LANES, LANES)))[0:1, :]
        dt_row = jnp.where(lane_lo, dt[:, h0:h0 + 1], dt[:, h1:h1 + 1])
        cb = jnp.sum(cg * bg, axis=1, keepdims=True)
        parts.append(y_row + (cb * dt_row) * xp + dsk_ref[:, pair * LANES:(pair + 1) * LANES] * xp)
    gated = jnp.concatenate(parts, axis=1) * _silu(z_ref[0])
    gw = DI_C // NG_C
    for g in range(NG_C):
        sl = slice(g * gw, (g + 1) * gw)
        y_ref[0, :, sl] = _rms(gated[:, sl], nw_ref[:, sl])


def _ssd_step(z, xbc, d_col, conv_state, conv_w, conv_b, dt_bias, a_log, d_skip, norm_w, h0):
    ns = z.shape[0]
    npair = NH_C // 2
    lane8 = lambda v: jnp.zeros((1, LANES), F32).at[0, :NH_C].set(v)
    s3 = lambda s: (s, 0, 0)
    s4 = lambda s: (s, 0, 0, 0)
    const = lambda s: (0, 0)
    y, h1 = pl.pallas_call(
        _ssd_step_kernel, grid=(ns,),
        in_specs=[pl.BlockSpec((1, 1, DI_C), s3), pl.BlockSpec((1, 1, CONV_DIM_C), s3),
                  pl.BlockSpec((1, CONV_C - 1, CONV_DIM_C), s3), pl.BlockSpec((1, 1, LANES), s3),
                  pl.BlockSpec((CONV_C, CONV_DIM_C), const), pl.BlockSpec((1, CONV_DIM_C), const),
                  pl.BlockSpec((1, LANES), const), pl.BlockSpec((1, LANES), const),
                  pl.BlockSpec((1, DI_C), const), pl.BlockSpec((1, DI_C), const),
                  pl.BlockSpec((1, npair, 2 * HD_C, DS_C), s4)],
        out_specs=[pl.BlockSpec((1, 1, DI_C), s3), pl.BlockSpec((1, npair, 2 * HD_C, DS_C), s4)],
        out_shape=[jax.ShapeDtypeStruct((ns, 1, DI_C), F32), jax.ShapeDtypeStruct((ns, npair, 2 * HD_C, DS_C), F32)],
        compiler_params=_params(("arbitrary",)), name="ssd_step",
    )(z.reshape(ns, 1, DI_C), xbc.reshape(ns, 1, CONV_DIM_C), conv_state, d_col.reshape(ns, 1, LANES),
      conv_w, conv_b.reshape(1, -1), lane8(dt_bias), lane8(a_log), _head_lanes(d_skip, HD_C),
      norm_w.reshape(1, DI_C), h0.reshape(ns, npair, 2 * HD_C, DS_C))
    return y.reshape(ns, DI_C), h1.reshape(ns, NH_C, HD_C, DS_C)


def _head_mask(width):
    lane = lax.broadcasted_iota(jnp.int32, (1, width), 1)
    return [(lane >= h * HD_D) & (lane < (h + 1) * HD_D) for h in range(width // HD_D)]


SWA_TILE = 2048


def _swa_kernel(dil, q_ref, kt_ref, kc_ref, vt_ref, vc_ref, acc_ref, m_ref, l_ref, kbuf, vbuf):
    t = pl.program_id(1)
    nk = SWA_KEYS
    tail = nk * dil
    tile = q_ref.shape[0]
    kbuf[0:tail, :] = kt_ref[...]
    kbuf[tail:tail + tile, :] = kc_ref[...]
    vbuf[0:tail, :] = vt_ref[...]
    vbuf[tail:tail + tile, :] = vc_ref[...]
    qi = lax.broadcasted_iota(jnp.int32, (nk, 2 * nk), 0)
    kj = lax.broadcasted_iota(jnp.int32, (nk, 2 * nk), 1)
    dist = nk + qi - kj
    band = (dist >= 0) & (dist <= nk)
    heads = _head_mask(LANES)

    for j in range(tile // tail):
        first = j * tail
        ok = band if j > 0 else band & ((t > 0) | (kj >= nk))
        q_win = q_ref.at[pl.ds(first, tail), :]
        k_win = kbuf.at[pl.ds(first, 2 * tail), :]
        v_win = vbuf.at[pl.ds(first, 2 * tail), :]
        outs = [o.at[pl.ds(first, tail), :] for o in (acc_ref, m_ref, l_ref)]
        for r in range(dil):
            q = q_win[pl.ds(r, nk, stride=dil), :] * (HD_D ** -0.5)
            kk = k_win[pl.ds(r, 2 * nk, stride=dil), :].astype(BF16)
            vv = v_win[pl.ds(r, 2 * nk, stride=dil), :].astype(BF16)
            acc = jnp.zeros((nk, LANES), F32)
            mm = jnp.zeros((nk, LANES), F32)
            ll = jnp.zeros((nk, LANES), F32)
            for hm in heads:
                s = _dot(jnp.where(hm, q, 0.0).astype(BF16), kk, NT)
                s = jnp.where(ok, s, -jnp.inf)
                mx = jnp.max(s, axis=1, keepdims=True)
                p = jnp.exp(s - mx)
                acc = jnp.where(hm, _dot(p.astype(BF16), vv), acc)
                mm = jnp.where(hm, mx, mm)
                ll = jnp.where(hm, jnp.sum(p, axis=1, keepdims=True), ll)
            for o, val in zip(outs, (acc, mm, ll)):
                o[pl.ds(r, nk, stride=dil), :] = val


def _swa_prompt(qd, kd, vd, g, dil, batch, seq):
    m = qd.shape[0]
    tile = min(SWA_TILE, seq)
    tail = SWA_KEYS * dil
    assert seq % tile == 0 and tile % tail == 0
    nt = seq // tile
    per = tile // tail
    halves = SWA_W // LANES
    cur = pl.BlockSpec((tile, LANES), lambda b, t, f: (b * nt + t, g * halves + f))
    prev = pl.BlockSpec((tail, LANES), lambda b, t, f: (jnp.maximum((b * nt + t) * per - 1, 0), g * halves + f))
    ospec = pl.BlockSpec((tile, LANES), lambda b, t, f: (b * nt + t, f))
    oshape = jax.ShapeDtypeStruct((m, SWA_W), F32)
    return pl.pallas_call(
        functools.partial(_swa_kernel, dil), grid=(batch, nt, halves),
        in_specs=[cur, prev, cur, prev, cur], out_specs=[ospec] * 3, out_shape=[oshape] * 3,
        scratch_shapes=[pltpu.VMEM((tail + tile, LANES), F32), pltpu.VMEM((tail + tile, LANES), F32)],
        compiler_params=_params(("arbitrary", "arbitrary", "arbitrary")), name=f"swa_prompt_{g}",
    )(qd, kd, kd, vd, vd)


def _swa_step_kernel(q_ref, kn_ref, vn_ref, c0_ref, c1_ref, c2_ref, o_ref):
    scale = HD_D ** -0.5
    ngrp = len(SWA_GROUPS)
    stats = []
    for g, (c_ref, (win, dil)) in enumerate(zip((c0_ref, c1_ref, c2_ref), SWA_GROUPS)):
        on_stride = lax.broadcasted_iota(jnp.int32, (1, win), 1) % dil == 0
        per_head = []
        for h in range(HPG_D):
            sl = slice((g * HPG_D + h) * HD_D, (g * HPG_D + h + 1) * HD_D)
            q = q_ref[0, sl, :] * scale
            s = jnp.sum(c_ref[0, 0, h] * q, axis=0, keepdims=True)
            s = jnp.where(on_stride, s, -jnp.inf)
            sn = jnp.sum(kn_ref[0, sl, :] * q, axis=0, keepdims=True)
            mx = jnp.maximum(jnp.max(s, axis=1, keepdims=True), sn)
            p = jnp.exp(s - mx)
            pn = jnp.exp(sn - mx)
            den = jnp.sum(p, axis=1, keepdims=True) + pn
            acc = jnp.sum(c_ref[0, 1, h] * p, axis=1, keepdims=True) + pn * vn_ref[0, sl, :]
            per_head.append((acc, mx, den))
        stats.append(per_head)
    for h in range(HPG_D):
        mx = functools.reduce(jnp.maximum, [stats[g][h][1] for g in range(ngrp)])
        wgt = [jnp.exp(stats[g][h][1] - mx) for g in range(ngrp)]
        num = functools.reduce(jnp.add, [wgt[g] * stats[g][h][0] for g in range(ngrp)])
        den = functools.reduce(jnp.add, [wgt[g] * stats[g][h][2] for g in range(ngrp)])
        o_ref[0, h * HD_D:(h + 1) * HD_D, :] = num / den


def _swa_step(qd, kd, vd, caches):
    ns = qd.shape[0]
    w3 = qd.shape[1]
    views, cspecs = [], []
    for (win, dil), cache in zip(SWA_GROUPS, caches):
        assert cache.shape[1] == win and win // dil == SWA_KEYS
        views.append(jnp.transpose(cache, (0, 2, 3, 4, 1)))
        cspecs.append(pl.BlockSpec((1, 2, HPG_D, HD_D, win), lambda s: (s, 0, 0, 0, 0)))
    s3 = lambda s: (s, 0, 0)
    col = pl.BlockSpec((1, w3, 1), s3)
    out = pl.pallas_call(
        _swa_step_kernel, grid=(ns,), in_specs=[col, col, col] + cspecs,
        out_specs=pl.BlockSpec((1, SWA_W, 1), s3), out_shape=jax.ShapeDtypeStruct((ns, SWA_W, 1), F32),
        compiler_params=_params(("arbitrary",)), name="swa_step",
    )(qd.reshape(ns, w3, 1), kd.reshape(ns, w3, 1), vd.reshape(ns, w3, 1), *views)
    return out.reshape(ns, SWA_W)


def _swa_merge_kernel(*refs):
    ins, o_ref = refs[:-1], refs[-1]
    ng = len(ins) // 3
    mx = ins[1][...]
    for g in range(1, ng):
        mx = jnp.maximum(mx, ins[3 * g + 1][...])
    num = jnp.zeros_like(mx)
    den = jnp.zeros_like(mx)
    for g in range(ng):
        wgt = jnp.exp(ins[3 * g + 1][...] - mx)
        num = num + wgt * ins[3 * g][...]
        den = den + wgt * ins[3 * g + 2][...]
    o_ref[...] = num / den


def _swa_merge(parts, tm):
    flat = [a for grp in parts for a in grp]
    m = flat[0].shape[0]
    spec = pl.BlockSpec((tm, SWA_W), lambda i: (i, 0))
    return pl.pallas_call(
        _swa_merge_kernel, grid=(m // tm,), in_specs=[spec] * len(flat), out_specs=spec,
        out_shape=jax.ShapeDtypeStruct((m, SWA_W), F32),
        compiler_params=_params(("arbitrary",)), name="swa_merge",
    )(*flat)


def _split_cols(w, sizes):
    out, start = [], 0
    for s in sizes:
        out.append(w[:, start:start + s])
        start += s
    return out


def _trunk(x3, pos0, state, wts, tm):
    batch, seq, d = x3.shape
    m = batch * seq
    x = x3.reshape(m, d)
    pos = pos0 + jnp.arange(seq, dtype=jnp.int32)
    prompt = state is None
    if not prompt:
        pos = jnp.broadcast_to(pos, (m,))
    tabs_b, half_b = _rope_tables(pos, HD_B)
    tabs_d, half_d = _rope_tables(pos, HD_D)
    wa = NH_A * DK_A

    a_main, qb, kb, vb, g_col = _inproj(
        x, wts["norm_mix"][0], tabs_b, wts["l0_w"], (0, half_b, half_b, 0, 0), tm, "inproj_l0")
    if prompt:
        ha, c1, n1, m1 = _mlstm_prompt(a_main, g_col, wts["b_gates"], wts["norm_mlstm"], batch, seq)
        hb = _moba_prompt(qb, kb, vb, batch, seq)
    else:
        c0, n0, m0, k_cache, v_cache, page_table = state["even"]
        ha, c1, n1, m1 = _mlstm_step(a_main, g_col, wts["b_gates"], wts["norm_mlstm"], c0, n0, m0)
        ha = ha.reshape(m, wa)
        _, sel = _moba_select(qb, k_cache, page_table)
        hb = _moba_step(qb, kb, vb, k_cache, v_cache, page_table, sel[:, :NH_B, :MOBA_TOPK])
    m1 = m1[:, :, 0]
    ffn = wts["ffn"]
    x = _outproj(ha, hb, wts["l0_out_a"], wts["l0_out_b"], x, tm, "outproj_l0")
    if prompt:
        x, fbuf0 = _ffn_prompt(x, wts["norm_ffn"][0], *ffn[0], wts["norm_final"], False, batch, seq,
                               min(tm, 256), "ffn_l0")
    else:
        x, fbuf0 = _ffn_step(x, wts["norm_ffn"][0], *ffn[0], wts["norm_final"], False, state["ffn"][0], "ffn_step_l0")

    z, xbc, qd, kd, vd, d_col = _inproj(
        x, wts["norm_mix"][1], tabs_d, wts["l1_w"], (0, 0, half_d, half_d, 0, 0), tm, "inproj_l1")
    ssd_w = wts["ssd"]
    if prompt:
        yc, h1, conv1 = _ssd_prompt(z, xbc, d_col, *ssd_w, batch, seq)
        parts = [_swa_prompt(qd, kd, vd, g, dil, batch, seq) for g, (_, dil) in enumerate(SWA_GROUPS)]
        od = _swa_merge(parts, tm)
    else:
        h0, conv0, caches = state["odd"]
        yc, h1 = _ssd_step(z, xbc, d_col, conv0, *ssd_w, h0)
        conv1 = jnp.concatenate([conv0[:, 1:], xbc[:, None, :]], axis=1)
        od = _swa_step(qd, kd, vd, caches)
    x = _outproj(yc, od, wts["l1_out_a"], wts["l1_out_b"], x, tm, "outproj_l1")
    if prompt:
        x, fbuf1 = _ffn_prompt(x, wts["norm_ffn"][1], *ffn[1], wts["norm_final"], True, batch, seq,
                               min(tm, 256), "ffn_l1")
    else:
        x, fbuf1 = _ffn_step(x, wts["norm_ffn"][1], *ffn[1], wts["norm_final"], True, state["ffn"][1], "ffn_step_l1")

    kr = kb.reshape(batch, seq, KVH_B, HD_B)
    vr = vb.reshape(batch, seq, KVH_B, HD_B)
    rows = []
    for g, (win, _) in enumerate(SWA_GROUPS):
        keep = min(win, seq) if prompt else seq
        kg = kd.reshape(batch, seq, -1)[:, seq - keep:, g * SWA_W:(g + 1) * SWA_W].reshape(batch, keep, HPG_D, HD_D)
        vg = vd.reshape(batch, seq, -1)[:, seq - keep:, g * SWA_W:(g + 1) * SWA_W].reshape(batch, keep, HPG_D, HD_D)
        rows.append(jnp.stack([kg, vg], axis=2))
    return (x.reshape(batch, seq, d), (c1, n1, m1, kr, vr), (h1, conv1, rows[0], rows[1], rows[2]),
            jnp.stack([fbuf0, fbuf1]))


def kernel(x_prompt, x_sample, state_l0_mlstm_c, state_l0_mlstm_n, state_l0_mlstm_m, cache_l0_moba_k, cache_l0_moba_v, state_l1_ssd_h, state_l1_ssd_conv, cache_l1_swa_kv0, cache_l1_swa_kv1, cache_l1_swa_kv2, state_ffn_conv, page_table, norm_mix, norm_ffn, norm_final, w_in_l0, b_gates_l0, norm_mlstm_l0, w_out_l0, w_in_l1, conv_w_l1, conv_b_l1, dt_bias_l1, a_log_l1, d_skip_l1, norm_ssd_l1, w_out_l1, ffn_up, ffn_conv_w, ffn_conv_b, ffn_down):
    wa = NH_A * DK_A
    qa, ka, va, oa, ia, fa, qb, kb, vb = _split_cols(
        w_in_l0, (wa, wa, wa, wa, NH_A, NH_A, NH_B * HD_B, KVH_B * HD_B, KVH_B * HD_B))
    wd = 3 * SWA_W
    z, xbc, dtr, qd, kd, vd = _split_cols(w_in_l1, (DI_C, CONV_DIM_C, NH_C, wd, wd, wd))
    bf = lambda a: a.astype(BF16)
    wts = {
        "norm_mix": norm_mix, "norm_ffn": norm_ffn, "norm_final": norm_final,
        "l0_w": [bf(jnp.concatenate([qa, ka, va, oa], axis=1)), bf(qb), bf(kb), bf(vb),
                 bf(_pad_lanes(jnp.concatenate([ia, fa], axis=1)))],
        "b_gates": b_gates_l0, "norm_mlstm": norm_mlstm_l0,
        "l0_out_a": bf(w_out_l0[:wa]), "l0_out_b": bf(w_out_l0[wa:]),
        "l1_w": [bf(z), bf(xbc), bf(qd), bf(kd), bf(vd), bf(_pad_lanes(dtr))],
        "ssd": (conv_w_l1, conv_b_l1, dt_bias_l1, a_log_l1, d_skip_l1, norm_ssd_l1),
        "l1_out_a": bf(w_out_l1[:DI_C]), "l1_out_b": bf(w_out_l1[DI_C:]),
        "ffn": [(bf(ffn_up[l]), ffn_conv_w[l], ffn_conv_b[l], bf(ffn_down[l])) for l in range(ffn_up.shape[0])],
    }
    n_seq, n_pages = page_table.shape
    past_len = n_pages * cache_l0_moba_k.shape[1]
    y_p, ev_p, od_p, ffn_p = _trunk(x_prompt, 0, None, wts, 512)
    state = {
        "even": (state_l0_mlstm_c, state_l0_mlstm_n, state_l0_mlstm_m, cache_l0_moba_k, cache_l0_moba_v, page_table),
        "odd": (state_l1_ssd_h, state_l1_ssd_conv, (cache_l1_swa_kv0, cache_l1_swa_kv1, cache_l1_swa_kv2)),
        "ffn": state_ffn_conv,
    }
    y_s, ev_s, od_s, ffn_s = _trunk(x_sample, past_len, state, wts, x_sample.shape[0] * x_sample.shape[1])
    c_p, n_p, m_p, k_p, v_p = ev_p
    c_s, n_s, m_s, k_s, v_s = ev_s
    h_p, conv_p, sw0_p, sw1_p, sw2_p = od_p
    h_s, conv_s, sw0_s, sw1_s, sw2_s = od_s
    return (y_p, y_s, c_p, c_s, n_p, n_s, m_p, m_s, k_p, k_s, v_p, v_s, h_p, h_s, conv_p, conv_s,
            sw0_p, sw0_s, sw1_p, sw1_s, sw2_p, sw2_s, ffn_p, ffn_s)
```
